```python
import jax, jax.numpy as jnp
from jax import lax
import numpy as np

D_MODEL = 2048
BATCH = 4
SEQ = 4096
DEPTH = 2

MEM_LEN = 256
HEAD_DIM = 64
DIL_GROUPS = ((128, 1), (512, 4), (2048, 16))
A_HEADS_PER_GROUP = 8
A_HEADS = A_HEADS_PER_GROUP * len(DIL_GROUPS)
BAND_BLOCK = 128
CHUNK = 128
B_GROUPS = 8
B_WIDTH = 1024
B_GROUP_DIM = B_WIDTH // B_GROUPS
C_HEADS = D_MODEL // HEAD_DIM
MOBA_BLOCK = 256
MOBA_TOPK = 3
X_HEADS = 4
X_HEAD_DIM = D_MODEL // X_HEADS
D_FF = ((8 * D_MODEL // 3 + 127) // 128) * 128
DEEPNORM_ALPHA = (2 * DEPTH) ** 0.25
DEEPNORM_BETA = (8 * DEPTH) ** -0.25
LN_EPS = 1e-5

A_QKV_WIDTH = 3 * A_HEADS * HEAD_DIM
MIX0_IN = A_QKV_WIDTH + 2 * B_WIDTH
MIX0_OUT = A_HEADS_PER_GROUP * HEAD_DIM + B_WIDTH
MIX1_IN = 3 * C_HEADS * HEAD_DIM
MIX1_OUT = C_HEADS * HEAD_DIM

kernel_name = 'hybrid_dilated_gmlp_moba_deepnorm'


def layer_norm(x, g, b):
    xf = x.astype(jnp.float32)
    mu = jnp.mean(xf, axis=-1, keepdims=True)
    var = jnp.mean(jnp.square(xf - mu), axis=-1, keepdims=True)
    y = (xf - mu) * lax.rsqrt(var + LN_EPS) * g.astype(jnp.float32) + b.astype(jnp.float32)
    return y.astype(x.dtype)


def deepnorm_residual(x, fx, g, b):
    return layer_norm(DEEPNORM_ALPHA * x + fx, g, b)


def swiglu_ffn(x, w_in, w_out):
    gate, up = jnp.split(x @ w_in, 2, axis=-1)
    return (jax.nn.silu(gate) * up) @ w_out


def dilated_window_attention(q, k, v, window, dilation):
    B, S, H, hd = q.shape
    L = S // dilation
    Lp = -(-L // BAND_BLOCK) * BAND_BLOCK
    nblk = Lp // BAND_BLOCK
    max_off = window // dilation

    def to_blocks(t):
        t = t.reshape(B, L, dilation, H, hd).transpose(0, 2, 3, 1, 4)
        t = jnp.pad(t, ((0, 0), (0, 0), (0, 0), (0, Lp - L), (0, 0)))
        return t.reshape(B, dilation, H, nblk, BAND_BLOCK, hd)

    def with_prev(t):
        prev = jnp.pad(t, ((0, 0), (0, 0), (0, 0), (1, 0), (0, 0), (0, 0)))[:, :, :, :-1]
        return jnp.concatenate([prev, t], axis=4)

    qb = to_blocks(q)
    kc = with_prev(to_blocks(k))
    vc = with_prev(to_blocks(v))
    s = jnp.einsum('bdhnqc,bdhnkc->bdhnqk', qb, kc).astype(jnp.float32) * (hd ** -0.5)
    qi = jnp.arange(BAND_BLOCK)[:, None]
    ki = jnp.arange(2 * BAND_BLOCK)[None, :]
    off = qi + BAND_BLOCK - ki
    band = (off >= 0) & (off <= max_off)
    has_prev = (jnp.arange(nblk) > 0)[:, None, None] | (ki >= BAND_BLOCK)[None]
    mask = band[None] & has_prev
    s = jnp.where(mask, s, -jnp.inf)
    lse = jax.nn.logsumexp(s, axis=-1)
    p = jnp.exp(s - lse[..., None]).astype(v.dtype)
    o = jnp.einsum('bdhnqk,bdhnkc->bdhnqc', p, vc)
    o = o.reshape(B, dilation, H, Lp, hd)[:, :, :, :L].transpose(0, 3, 1, 2, 4).reshape(B, S, H, hd)
    lse = lse.reshape(B, dilation, H, Lp)[:, :, :, :L].transpose(0, 3, 1, 2).reshape(B, S, H)
    return o, lse


def mixer_a(q, k, v):
    outs, lses = [], []
    for g, (window, dilation) in enumerate(DIL_GROUPS):
        sl = slice(g * A_HEADS_PER_GROUP, (g + 1) * A_HEADS_PER_GROUP)
        o, l = dilated_window_attention(q[:, :, sl], k[:, :, sl], v[:, :, sl], window, dilation)
        outs.append(o)
        lses.append(l)
    w = jax.nn.softmax(jnp.stack(lses, axis=0), axis=0).astype(q.dtype)
    return jnp.einsum('gbsh,gbshc->bshc', w, jnp.stack(outs, axis=0))


def mixer_b(u, v, ln_g, ln_b, w_s, b_s):
    Bsz, S, _ = u.shape
    u = jax.nn.gelu(u)
    v = layer_norm(jax.nn.gelu(v), ln_g, ln_b)
    nc = S // CHUNK
    vc = v.reshape(Bsz, nc, CHUNK, B_GROUPS, B_GROUP_DIM)
    tri = jnp.tril(jnp.ones((CHUNK, CHUNK), dtype=bool))
    w = jnp.where(tri[None], w_s, 0)
    mixed = jnp.einsum('gij,bnjgc->bnigc', w, vc) + b_s.T[None, None, :, :, None]
    return u * mixed.reshape(Bsz, S, B_WIDTH)


def moba_attention(q, k, v):
    B, S, H, hd = q.shape
    Sp = -(-S // MOBA_BLOCK) * MOBA_BLOCK
    nb = Sp // MOBA_BLOCK
    kt = min(MOBA_TOPK, nb)
    scale = hd ** -0.5

    def to_blocks(t):
        t = jnp.pad(t, ((0, 0), (0, Sp - S), (0, 0), (0, 0))).transpose(0, 2, 1, 3)
        return t.reshape(B * H, nb, MOBA_BLOCK, hd)

    qb, kb, vb = to_blocks(q), to_blocks(k), to_blocks(v)
    kmean = jnp.mean(kb.astype(jnp.float32), axis=2).astype(kb.dtype)
    gate = jnp.einsum('znqc,zmc->znqm', qb, kmean).astype(jnp.float32)
    past = jnp.arange(nb)[None, :] < jnp.arange(nb)[:, None]
    gate = jnp.where(past[None, :, None, :], gate, -jnp.inf)
    top_val, top_idx = lax.top_k(gate, kt)
    sel_ok = jnp.isfinite(top_val)
    causal = jnp.tril(jnp.ones((MOBA_BLOCK, MOBA_BLOCK), dtype=bool))

    def attend_block(args):
        z, qc, idx, ok, k_own, v_own = args
        k_sel = kb[z][idx]
        v_sel = vb[z][idx]
        s_sel = jnp.einsum('qc,qjkc->qjk', qc, k_sel).astype(jnp.float32) * scale
        s_sel = jnp.where(ok[:, :, None], s_sel, -jnp.inf).reshape(MOBA_BLOCK, kt * MOBA_BLOCK)
        s_own = jnp.where(causal, (qc @ k_own.T).astype(jnp.float32) * scale, -jnp.inf)
        p = jax.nn.softmax(jnp.concatenate([s_sel, s_own], axis=-1), axis=-1).astype(v_own.dtype)
        p_sel = p[:, :kt * MOBA_BLOCK].reshape(MOBA_BLOCK, kt, MOBA_BLOCK)
        p_own = p[:, kt * MOBA_BLOCK:]
        return jnp.einsum('qjk,qjkc->qc', p_sel, v_sel) + p_own @ v_own

    n_items = B * H * nb
    z_idx = jnp.repeat(jnp.arange(B * H), nb)
    out = lax.map(attend_block, (z_idx,
                                 qb.reshape(n_items, MOBA_BLOCK, hd),
                                 top_idx.reshape(n_items, MOBA_BLOCK, kt),
                                 sel_ok.reshape(n_items, MOBA_BLOCK, kt),
                                 kb.reshape(n_items, MOBA_BLOCK, hd),
                                 vb.reshape(n_items, MOBA_BLOCK, hd)))
    return out.reshape(B, H, Sp, hd).transpose(0, 2, 1, 3)[:, :S]


def token_mix_even(x, w_in, gmlp_ln_g, gmlp_ln_b, gmlp_w_s, gmlp_b_s, w_out):
    B, S, _ = x.shape
    h = x @ w_in
    qkv, u, v = jnp.split(h, [A_QKV_WIDTH, A_QKV_WIDTH + B_WIDTH], axis=-1)
    qkv = qkv.reshape(B, S, 3, A_HEADS, HEAD_DIM)
    a_out = mixer_a(qkv[:, :, 0], qkv[:, :, 1], qkv[:, :, 2]).reshape(B, S, A_HEADS_PER_GROUP * HEAD_DIM)
    b_out = mixer_b(u, v, gmlp_ln_g, gmlp_ln_b, gmlp_w_s, gmlp_b_s)
    return jnp.concatenate([a_out, b_out], axis=-1) @ w_out


def token_mix_odd(x, w_in, w_out):
    B, S, _ = x.shape
    qkv = (x @ w_in).reshape(B, S, 3, C_HEADS, HEAD_DIM)
    o = moba_attention(qkv[:, :, 0], qkv[:, :, 1], qkv[:, :, 2])
    return o.reshape(B, S, MIX1_OUT) @ w_out


def memory_cross_attention(x, mem, w_q, w_kv, w_o):
    B, S, _ = x.shape
    M = mem.shape[1]
    q = (x @ w_q).reshape(B, S, X_HEADS, X_HEAD_DIM)
    k, v = jnp.split(mem @ w_kv, 2, axis=-1)
    k = k.reshape(B, M, X_HEADS, X_HEAD_DIM)
    v = v.reshape(B, M, X_HEADS, X_HEAD_DIM)
    s = jnp.einsum('bshc,bmhc->bhsm', q, k).astype(jnp.float32) * (X_HEAD_DIM ** -0.5)
    p = jax.nn.softmax(s, axis=-1).astype(v.dtype)
    o = jnp.einsum('bhsm,bmhc->bshc', p, v).reshape(B, S, D_MODEL)
    return o @ w_o


def hybrid_layer(x, mem, p, even):
    x = deepnorm_residual(x, 0.5 * swiglu_ffn(x, *p['ffn1']), *p['ln1'])
    mixed = token_mix_even(x, *p['mix']) if even else token_mix_odd(x, *p['mix'])
    x = deepnorm_residual(x, mixed, *p['ln2'])
    x = deepnorm_residual(x, memory_cross_attention(x, mem, *p['mem']), *p['ln3'])
    x = deepnorm_residual(x, 0.5 * swiglu_ffn(x, *p['ffn2']), *p['ln4'])
    return x


def setup_inputs(seed: int = 0) -> dict:
    key = jax.random.key(seed)
    keys = iter(jax.random.split(key, 64))

    def normal(shape, scale):
        return jax.random.normal(next(keys), shape, jnp.float32) * scale

    def gain(n):
        return 1.0 + normal((n,), 0.02)

    def bias(n):
        return normal((n,), 0.02)

    inp = {'x': normal((BATCH, SEQ, D_MODEL), 1.0),
           'mem': normal((BATCH, MEM_LEN, D_MODEL), 1.0)}
    for i in range(DEPTH):
        p = 'l%d_' % i
        inp[p + 'ffn1_w_in'] = normal((D_MODEL, 2 * D_FF), D_MODEL ** -0.5)
        inp[p + 'ffn1_w_out'] = normal((D_FF, D_MODEL), D_FF ** -0.5 * DEEPNORM_BETA)
        inp[p + 'ln1_g'] = gain(D_MODEL)
        inp[p + 'ln1_b'] = bias(D_MODEL)
        if i % 2 == 0:
            inp[p + 'mix_w_in'] = normal((D_MODEL, MIX0_IN), D_MODEL ** -0.5)
            inp[p + 'gmlp_ln_g'] = gain(B_WIDTH)
            inp[p + 'gmlp_ln_b'] = bias(B_WIDTH)
            inp[p + 'gmlp_w_s'] = normal((B_GROUPS, CHUNK, CHUNK), CHUNK ** -0.5)
            inp[p + 'gmlp_b_s'] = 1.0 + normal((B_GROUPS, CHUNK), 0.02)
            inp[p + 'mix_w_out'] = normal((MIX0_OUT, D_MODEL), MIX0_OUT ** -0.5 * DEEPNORM_BETA)
        else:
            inp[p + 'mix_w_in'] = normal((D_MODEL, MIX1_IN), D_MODEL ** -0.5)
            inp[p + 'mix_w_out'] = normal((MIX1_OUT, D_MODEL), MIX1_OUT ** -0.5 * DEEPNORM_BETA)
        inp[p + 'ln2_g'] = gain(D_MODEL)
        inp[p + 'ln2_b'] = bias(D_MODEL)
        inp[p + 'mem_w_q'] = normal((D_MODEL, D_MODEL), D_MODEL ** -0.5)
        inp[p + 'mem_w_kv'] = normal((D_MODEL, 2 * D_MODEL), D_MODEL ** -0.5)
        inp[p + 'mem_w_o'] = normal((D_MODEL, D_MODEL), D_MODEL ** -0.5 * DEEPNORM_BETA)
        inp[p + 'ln3_g'] = gain(D_MODEL)
        inp[p + 'ln3_b'] = bias(D_MODEL)
        inp[p + 'ffn2_w_in'] = normal((D_MODEL, 2 * D_FF), D_MODEL ** -0.5)
        inp[p + 'ffn2_w_out'] = normal((D_FF, D_MODEL), D_FF ** -0.5 * DEEPNORM_BETA)
        inp[p + 'ln4_g'] = gain(D_MODEL)
        inp[p + 'ln4_b'] = bias(D_MODEL)
    return inp


def reference(x, mem,
              l0_ffn1_w_in, l0_ffn1_w_out, l0_ln1_g, l0_ln1_b,
              l0_mix_w_in, l0_gmlp_ln_g, l0_gmlp_ln_b, l0_gmlp_w_s, l0_gmlp_b_s, l0_mix_w_out,
              l0_ln2_g, l0_ln2_b, l0_mem_w_q, l0_mem_w_kv, l0_mem_w_o, l0_ln3_g, l0_ln3_b,
              l0_ffn2_w_in, l0_ffn2_w_out, l0_ln4_g, l0_ln4_b,
              l1_ffn1_w_in, l1_ffn1_w_out, l1_ln1_g, l1_ln1_b,
              l1_mix_w_in, l1_mix_w_out,
              l1_ln2_g, l1_ln2_b, l1_mem_w_q, l1_mem_w_kv, l1_mem_w_o, l1_ln3_g, l1_ln3_b,
              l1_ffn2_w_in, l1_ffn2_w_out, l1_ln4_g, l1_ln4_b):
    layers = [
        dict(ffn1=(l0_ffn1_w_in, l0_ffn1_w_out), ln1=(l0_ln1_g, l0_ln1_b),
             mix=(l0_mix_w_in, l0_gmlp_ln_g, l0_gmlp_ln_b, l0_gmlp_w_s, l0_gmlp_b_s, l0_mix_w_out),
             ln2=(l0_ln2_g, l0_ln2_b), mem=(l0_mem_w_q, l0_mem_w_kv, l0_mem_w_o),
             ln3=(l0_ln3_g, l0_ln3_b), ffn2=(l0_ffn2_w_in, l0_ffn2_w_out), ln4=(l0_ln4_g, l0_ln4_b)),
        dict(ffn1=(l1_ffn1_w_in, l1_ffn1_w_out), ln1=(l1_ln1_g, l1_ln1_b),
             mix=(l1_mix_w_in, l1_mix_w_out),
             ln2=(l1_ln2_g, l1_ln2_b), mem=(l1_mem_w_q, l1_mem_w_kv, l1_mem_w_o),
             ln3=(l1_ln3_g, l1_ln3_b), ffn2=(l1_ffn2_w_in, l1_ffn2_w_out), ln4=(l1_ln4_g, l1_ln4_b)),
    ]
    for i in range(DEPTH):
        x = hybrid_layer(x, mem, layers[i], i % 2 == 0)
    return x
```

```python
import functools

import jax
import jax.numpy as jnp
from jax import lax
from jax.experimental import pallas as pl
from jax.experimental.pallas import tpu as pltpu

D_MODEL = 2048
DEPTH = 2
HEAD_DIM = 64
HEAD_PAIR = 2 * HEAD_DIM
DIL_GROUPS = ((128, 1), (512, 4), (2048, 16))
A_HEADS_PER_GROUP = 8
A_HEADS = A_HEADS_PER_GROUP * len(DIL_GROUPS)
BAND_BLOCK = 128
CHUNK = 128
B_GROUPS = 8
B_WIDTH = 1024
B_GROUP_DIM = B_WIDTH // B_GROUPS
C_HEADS = D_MODEL // HEAD_DIM
MOBA_BLOCK = 256
MOBA_TOPK = 3
X_HEADS = 4
X_HEAD_DIM = D_MODEL // X_HEADS
D_FF = 5504
DEEPNORM_ALPHA = (2 * DEPTH) ** 0.25
LN_EPS = 1e-5
A_QKV_WIDTH = 3 * A_HEADS * HEAD_DIM

LANES = 128
FF_TILE = 512
D_FF_PAD = -(-D_FF // FF_TILE) * FF_TILE
ROW_TILE = 512
VMEM_LIMIT = 56 * 1024 * 1024
NEG_BIG = -1e30

BF16 = jnp.bfloat16
F32 = jnp.float32


def _params(*sem):
    return pltpu.CompilerParams(dimension_semantics=sem, vmem_limit_bytes=VMEM_LIMIT)


def _dot(a, b):
    return jnp.dot(a, b, preferred_element_type=F32)


def _dot_nt(a, b):
    return lax.dot_general(a, b, (((1,), (1,)), ((), ())), preferred_element_type=F32)


def _layer_norm(y, g, b):
    mu = jnp.mean(y, axis=-1, keepdims=True)
    yc = y - mu
    var = jnp.mean(yc * yc, axis=-1, keepdims=True)
    return yc * lax.rsqrt(var + LN_EPS) * g + b


def _ffn_kernel(x_ref, wg_ref, wu_ref, wo_ref, g_ref, b_ref, o_ref, xb_ref, acc_ref):
    f = pl.program_id(1)

    @pl.when(f == 0)
    def _():
        xb_ref[...] = x_ref[...].astype(BF16)
        acc_ref[...] = jnp.zeros_like(acc_ref)

    xb = xb_ref[...]
    gate = _dot(xb, wg_ref[...])
    up = _dot(xb, wu_ref[...])
    act = (gate * jax.nn.sigmoid(gate)) * up
    acc_ref[...] += _dot(act.astype(BF16), wo_ref[...])

    @pl.when(f == pl.num_programs(1) - 1)
    def _():
        y = DEEPNORM_ALPHA * x_ref[...] + 0.5 * acc_ref[...]
        o_ref[...] = _layer_norm(y, g_ref[...], b_ref[...])


def _ffn_deepnorm(x, w_in_p, w_out_p, g, b):
    n, d = x.shape
    nf = D_FF_PAD // FF_TILE
    return pl.pallas_call(
        _ffn_kernel,
        grid=(n // ROW_TILE, nf),
        in_specs=[
            pl.BlockSpec((ROW_TILE, d), lambda i, f: (i, 0)),
            pl.BlockSpec((d, FF_TILE), lambda i, f: (0, f)),
            pl.BlockSpec((d, FF_TILE), lambda i, f: (0, f + nf)),
            pl.BlockSpec((FF_TILE, d), lambda i, f: (f, 0)),
            pl.BlockSpec((1, d), lambda i, f: (0, 0)),
            pl.BlockSpec((1, d), lambda i, f: (0, 0)),
        ],
        out_specs=pl.BlockSpec((ROW_TILE, d), lambda i, f: (i, 0)),
        out_shape=jax.ShapeDtypeStruct((n, d), F32),
        scratch_shapes=[pltpu.VMEM((ROW_TILE, d), BF16), pltpu.VMEM((ROW_TILE, d), F32)],
        compiler_params=_params("parallel", "arbitrary"),
        name="ffn_deepnorm",
    )(x, w_in_p, w_in_p, w_out_p, g.reshape(1, d), b.reshape(1, d))


def _proj_kernel(x_ref, w_ref, o_ref):
    o_ref[...] = _dot(x_ref[...].astype(BF16), w_ref[...]).astype(o_ref.dtype)


def _project(x, w, out_dtype, col_tile=512, row_tile=1024):
    n, k = x.shape
    m = w.shape[1]
    row_tile = min(row_tile, n)
    return pl.pallas_call(
        _proj_kernel,
        grid=(n // row_tile, m // col_tile),
        in_specs=[
            pl.BlockSpec((row_tile, k), lambda i, j: (i, 0)),
            pl.BlockSpec((k, col_tile), lambda i, j: (0, j)),
        ],
        out_specs=pl.BlockSpec((row_tile, col_tile), lambda i, j: (i, j)),
        out_shape=jax.ShapeDtypeStruct((n, m), out_dtype),
        compiler_params=_params("parallel", "arbitrary"),
        name="project",
    )(x, w)


def _outproj_kernel(a_ref, w_ref, x_ref, g_ref, b_ref, o_ref):
    y = DEEPNORM_ALPHA * x_ref[...] + _dot(a_ref[...], w_ref[...])
    o_ref[...] = _layer_norm(y, g_ref[...], b_ref[...])


def _outproj_deepnorm(a, w, x, g, b):
    n, k = a.shape
    d = w.shape[1]
    return pl.pallas_call(
        _outproj_kernel,
        grid=(n // ROW_TILE,),
        in_specs=[
            pl.BlockSpec((ROW_TILE, k), lambda i: (i, 0)),
            pl.BlockSpec((k, d), lambda i: (0, 0)),
            pl.BlockSpec((ROW_TILE, d), lambda i: (i, 0)),
            pl.BlockSpec((1, d), lambda i: (0, 0)),
            pl.BlockSpec((1, d), lambda i: (0, 0)),
        ],
        out_specs=pl.BlockSpec((ROW_TILE, d), lambda i: (i, 0)),
        out_shape=jax.ShapeDtypeStruct((n, d), F32),
        compiler_params=_params("parallel"),
        name="outproj_deepnorm",
    )(a, w, x, g.reshape(1, d), b.reshape(1, d))


def _dilated_kernel(*refs, seq):
    n_g = len(DIL_GROUPS)
    qkv_refs = refs[:3 * n_g]
    o_ref = refs[3 * n_g]
    o_sc, lse_sc, bias_sc = refs[3 * n_g + 1:]

    bb = BAND_BLOCK
    qi = lax.broadcasted_iota(jnp.int32, (bb, 2 * bb), 0)
    ki = lax.broadcasted_iota(jnp.int32, (bb, 2 * bb), 1)
    for first, delta in ((0, bb), (1, 0)):
        off = qi + delta - ki
        bias_sc[first] = jnp.where((off >= 0) & (off <= bb), 0.0, -jnp.inf).astype(F32)

    lane = lax.broadcasted_iota(jnp.int32, (bb, HEAD_PAIR), 1)
    head0 = lane < HEAD_DIM
    scale = HEAD_DIM ** -0.5

    for g, (window, dil) in enumerate(DIL_GROUPS):
        assert window // dil == bb
        q_ref, k_ref, v_ref = qkv_refs[3 * g:3 * g + 3]
        n_blk = seq // (dil * bb)

        def body(it, carry, q_ref=q_ref, k_ref=k_ref, v_ref=v_ref, dil=dil, g=g):
            blk = it // dil
            phase = it - blk * dil
            is_first = (blk == 0).astype(jnp.int32)
            q_start = blk * (bb * dil) + phase
            k_start = jnp.maximum(blk - 1, 0) * (bb * dil) + phase
            if dil == 1:
                q_rows, k_rows = pl.ds(q_start, bb), pl.ds(k_start, 2 * bb)
            else:
                q_rows = pl.ds(q_start, bb, stride=dil)
                k_rows = pl.ds(k_start, 2 * bb, stride=dil)
            q = q_ref[q_rows, :] * scale
            k = k_ref[k_rows, :].astype(BF16)
            v = v_ref[k_rows, :].astype(BF16)
            bias = bias_sc[is_first]
            outs, lses = [], []
            for h0 in (True, False):
                qh = jnp.where(head0 if h0 else ~head0, q, 0.0).astype(BF16)
                s = _dot_nt(qh, k) + bias
                m = jnp.max(s, axis=-1, keepdims=True)
                p = jnp.exp(s - m)
                l = jnp.sum(p, axis=-1, keepdims=True)
                outs.append(_dot(p.astype(BF16), v) / l)
                lses.append(m + jnp.log(l))
            o_sc[g, q_rows, :] = jnp.where(head0, outs[0], outs[1])
            lse_sc[g, q_rows, :] = jnp.where(head0, lses[0], lses[1])
            return carry

        lax.fori_loop(0, n_blk * dil, body, 0)

    lse = [lse_sc[g] for g in range(n_g)]
    top = functools.reduce(jnp.maximum, lse)
    w = [jnp.exp(x - top) for x in lse]
    den = functools.reduce(jnp.add, w)
    acc = functools.reduce(jnp.add, [w[g] * o_sc[g] for g in range(n_g)])
    o_ref[...] = (acc / den).astype(o_ref.dtype)


def _dilated_attention(qkv, batch, seq):
    n_g = len(DIL_GROUPS)
    pairs = A_HEADS_PER_GROUP // 2
    blocks_per_part = A_HEADS * HEAD_DIM // HEAD_PAIR
    in_specs = []
    for g in range(n_g):
        for part in range(3):
            base = part * blocks_per_part + g * pairs
            in_specs.append(pl.BlockSpec((seq, HEAD_PAIR), lambda b, j, base=base: (b, base + j)))
    return pl.pallas_call(
        functools.partial(_dilated_kernel, seq=seq),
        grid=(batch, pairs),
        in_specs=in_specs,
        out_specs=pl.BlockSpec((seq, HEAD_PAIR), lambda b, j: (b, j)),
        out_shape=jax.ShapeDtypeStruct((batch * seq, pairs * HEAD_PAIR), BF16),
        scratch_shapes=[
            pltpu.VMEM((n_g, seq, HEAD_PAIR), F32),
            pltpu.VMEM((n_g, seq, HEAD_PAIR), F32),
            pltpu.VMEM((2, BAND_BLOCK, 2 * BAND_BLOCK), F32),
        ],
        compiler_params=_params("parallel", "parallel"),
        name="dilated_attention",
    )(*([qkv] * (3 * n_g)))


def _gmlp_kernel(u_ref, v_ref, g_ref, b_ref, ws_ref, bias_ref, o_ref, *, chunks):
    u = jax.nn.gelu(u_ref[...])
    v = _layer_norm(jax.nn.gelu(v_ref[...]), g_ref[...], b_ref[...]).astype(BF16)
    row = lax.broadcasted_iota(jnp.int32, (CHUNK, CHUNK), 0)
    col = lax.broadcasted_iota(jnp.int32, (CHUNK, CHUNK), 1)
    tril = row >= col
    bias = bias_ref[...]
    for grp in range(B_GROUPS):
        w = jnp.where(tril, ws_ref[grp], 0.0).astype(BF16)
        cols = slice(grp * B_GROUP_DIM, (grp + 1) * B_GROUP_DIM)
        for c in range(chunks):
            rows = slice(c * CHUNK, (c + 1) * CHUNK)
            mixed = _dot(w, v[rows, cols]) + bias[:, cols]
            o_ref[rows, cols] = (u[rows, cols] * mixed).astype(o_ref.dtype)


def _gmlp(uv, ln_g, ln_b, w_s, b_s, chunks=4):
    n = uv.shape[0]
    rows = chunks * CHUNK
    bias_full = jnp.repeat(b_s.T, B_GROUP_DIM, axis=1)
    return pl.pallas_call(
        functools.partial(_gmlp_kernel, chunks=chunks),
        grid=(n // rows,),
        in_specs=[
            pl.BlockSpec((rows, B_WIDTH), lambda i: (i, 0)),
            pl.BlockSpec((rows, B_WIDTH), lambda i: (i, 1)),
            pl.BlockSpec((1, B_WIDTH), lambda i: (0, 0)),
            pl.BlockSpec((1, B_WIDTH), lambda i: (0, 0)),
            pl.BlockSpec((B_GROUPS, CHUNK, CHUNK), lambda i: (0, 0, 0)),
            pl.BlockSpec((CHUNK, B_WIDTH), lambda i: (0, 0)),
        ],
        out_specs=pl.BlockSpec((rows, B_WIDTH), lambda i: (i, 0)),
        out_shape=jax.ShapeDtypeStruct((n, B_WIDTH), BF16),
        compiler_params=_params("parallel"),
        name="gmlp",
    )(uv, uv, ln_g.reshape(1, B_WIDTH), ln_b.reshape(1, B_WIDTH), w_s, bias_full)


def _moba_kernel(q_ref, k_ref, v_ref, o_ref, kmean_ref, *, n_blocks):
    blk = MOBA_BLOCK
    n = pl.program_id(2)
    n_rows = kmean_ref.shape[0]

    @pl.when(n == 0)
    def _():
        kmean_ref[...] = jnp.zeros_like(kmean_ref)
        for m in range(n_blocks):
            kb = k_ref[pl.ds(m * blk, blk), :].astype(F32)
            kmean_ref[pl.ds(m, 1), :] = jnp.mean(kb, axis=0, keepdims=True)

    lane = lax.broadcasted_iota(jnp.int32, (blk, HEAD_PAIR), 1)
    head0 = lane < HEAD_DIM
    lane_m = lax.broadcasted_iota(jnp.int32, (n_rows, HEAD_PAIR), 1)
    q = q_ref[...]
    kmean = kmean_ref[...]
    cand = lax.broadcasted_iota(jnp.int32, (n_rows, blk), 0)
    past = cand < n
    row = lax.broadcasted_iota(jnp.int32, (blk, blk), 0)
    col = lax.broadcasted_iota(jnp.int32, (blk, blk), 1)
    causal = row >= col
    onehot_lane = lax.broadcasted_iota(jnp.int32, (blk, LANES), 1)
    k_own = k_ref[pl.ds(pl.multiple_of(n * blk, blk), blk), :]
    v_own = v_ref[pl.ds(pl.multiple_of(n * blk, blk), blk), :]
    pad_rows = jnp.zeros((LANES - n_rows, blk), F32)

    q_aug, state = [], []
    for h0 in (True, False):
        hm = head0 if h0 else ~head0
        qh = jnp.where(hm, q, 0)
        hm_rows = (lane_m < HEAD_DIM) if h0 else (lane_m >= HEAD_DIM)
        kmh = jnp.where(hm_rows, kmean, 0.0).astype(BF16)
        gate = _dot_nt(kmh, qh)
        rank = jnp.zeros((n_rows, blk), jnp.int32)
        for m2 in range(n_blocks):
            g2 = gate[m2:m2 + 1, :]
            ahead = (g2 > gate) | ((g2 == gate) & (m2 < cand))
            rank = rank + jnp.where(ahead & (m2 < n), 1, 0)
        chosen = past & (rank < MOBA_TOPK) & (jnp.abs(gate) < jnp.inf)
        sel_bias = jnp.where(chosen, 0.0, NEG_BIG).astype(F32)
        sel_bias = jnp.concatenate([sel_bias, pad_rows], axis=0)
        sel_bias_t = jnp.transpose(sel_bias).astype(BF16)
        qs = (qh.astype(F32) * (HEAD_DIM ** -0.5)).astype(BF16)
        q_aug.append(jnp.concatenate([qs, sel_bias_t], axis=1))
        s = jnp.where(causal, _dot_nt(qs, k_own), -jnp.inf)
        m_i = jnp.max(s, axis=-1, keepdims=True)
        p = jnp.exp(s - m_i)
        l_i = jnp.sum(p, axis=-1, keepdims=True)
        acc = _dot(p.astype(BF16), v_own)
        state += [m_i, l_i, acc]

    def body(m, carry):
        k_m = k_ref[pl.ds(pl.multiple_of(m * blk, blk), blk), :]
        v_m = v_ref[pl.ds(pl.multiple_of(m * blk, blk), blk), :]
        k_aug = jnp.concatenate([k_m, (onehot_lane == m).astype(BF16)], axis=1)
        out = []
        for h in range(2):
            m_i, l_i, acc = carry[3 * h:3 * h + 3]
            s = _dot_nt(q_aug[h], k_aug)
            m_new = jnp.maximum(m_i, jnp.max(s, axis=-1, keepdims=True))
            alpha = jnp.exp(m_i - m_new)
            p = jnp.exp(s - m_new)
            l_new = alpha * l_i + jnp.sum(p, axis=-1, keepdims=True)
            acc_new = alpha * acc + _dot(p.astype(BF16), v_m)
            out += [m_new, l_new, acc_new]
        return tuple(out)

    state = lax.fori_loop(0, n, body, tuple(state))
    o0 = state[2] / state[1]
    o1 = state[5] / state[4]
    o_ref[...] = jnp.where(head0, o0, o1).astype(o_ref.dtype)


def _moba(qkv, batch, seq):
    assert seq % MOBA_BLOCK == 0
    n_blocks = seq // MOBA_BLOCK
    n_rows = -(-n_blocks // 8) * 8
    assert n_rows <= LANES
    pairs = C_HEADS // 2
    return pl.pallas_call(
        functools.partial(_moba_kernel, n_blocks=n_blocks),
        grid=(batch, pairs, n_blocks),
        in_specs=[
            pl.BlockSpec((MOBA_BLOCK, HEAD_PAIR), lambda b, j, n: (b * n_blocks + n, j)),
            pl.BlockSpec((seq, HEAD_PAIR), lambda b, j, n: (b, pairs + j)),
            pl.BlockSpec((seq, HEAD_PAIR), lambda b, j, n: (b, 2 * pairs + j)),
        ],
        out_specs=pl.BlockSpec((MOBA_BLOCK, HEAD_PAIR), lambda b, j, n: (b * n_blocks + n, j)),
        out_shape=jax.ShapeDtypeStruct((batch * seq, pairs * HEAD_PAIR), BF16),
        scratch_shapes=[pltpu.VMEM((n_rows, HEAD_PAIR), F32)],
        compiler_params=_params("parallel", "parallel", "arbitrary"),
        name="moba",
    )(qkv, qkv, qkv)


def _memattn_kernel(q_ref, k_ref, v_ref, o_ref):
    s = _dot_nt(q_ref[...], k_ref[...]) * (X_HEAD_DIM ** -0.5)
    m = jnp.max(s, axis=-1, keepdims=True)
    p = jnp.exp(s - m)
    l = jnp.sum(p, axis=-1, keepdims=True)
    o_ref[...] = (_dot(p.astype(BF16), v_ref[...]) / l).astype(o_ref.dtype)


def _mem_attention(q, kv, batch, seq, mem_len, q_tile=1024):
    q_tile = min(q_tile, seq)
    tiles = seq // q_tile
    return pl.pallas_call(
        _memattn_kernel,
        grid=(batch, tiles, X_HEADS),
        in_specs=[
            pl.BlockSpec((q_tile, X_HEAD_DIM), lambda b, i, h: (b * tiles + i, h)),
            pl.BlockSpec((mem_len, X_HEAD_DIM), lambda b, i, h: (b, h)),
            pl.BlockSpec((mem_len, X_HEAD_DIM), lambda b, i, h: (b, X_HEADS + h)),
        ],
        out_specs=pl.BlockSpec((q_tile, X_HEAD_DIM), lambda b, i, h: (b * tiles + i, h)),
        out_shape=jax.ShapeDtypeStruct((batch * seq, D_MODEL), BF16),
        compiler_params=_params("parallel", "parallel", "arbitrary"),
        name="mem_attention",
    )(q, kv, kv)


def _prep_ffn(w_in, w_out):
    pad = D_FF_PAD - D_FF
    gate, up = w_in[:, :D_FF], w_in[:, D_FF:]
    w_in_p = jnp.concatenate([jnp.pad(gate, ((0, 0), (0, pad))), jnp.pad(up, ((0, 0), (0, pad)))], axis=1)
    return w_in_p.astype(BF16), jnp.pad(w_out, ((0, pad), (0, 0))).astype(BF16)


def _mem_block(x, mem2, w_q, w_kv, w_o, g, b, batch, seq, mem_len):
    q = _project(x, w_q.astype(BF16), BF16)
    kv = _project(mem2, w_kv.astype(BF16), BF16)
    o = _mem_attention(q, kv, batch, seq, mem_len)
    return _outproj_deepnorm(o, w_o.astype(BF16), x, g, b)


def kernel(x, mem, l0_ffn1_w_in, l0_ffn1_w_out, l0_ln1_g, l0_ln1_b, l0_mix_w_in, l0_gmlp_ln_g, l0_gmlp_ln_b, l0_gmlp_w_s, l0_gmlp_b_s, l0_mix_w_out, l0_ln2_g, l0_ln2_b, l0_mem_w_q, l0_mem_w_kv, l0_mem_w_o, l0_ln3_g, l0_ln3_b, l0_ffn2_w_in, l0_ffn2_w_out, l0_ln4_g, l0_ln4_b, l1_ffn1_w_in, l1_ffn1_w_out, l1_ln1_g, l1_ln1_b, l1_mix_w_in, l1_mix_w_out, l1_ln2_g, l1_ln2_b, l1_mem_w_q, l1_mem_w_kv, l1_mem_w_o, l1_ln3_g, l1_ln3_b, l1_ffn2_w_in, l1_ffn2_w_out, l1_ln4_g, l1_ln4_b):
    batch, seq, d = x.shape
    mem_len = mem.shape[1]
    h = x.reshape(batch * seq, d)
    mem2 = mem.reshape(batch * mem_len, d)

    h = _ffn_deepnorm(h, *_prep_ffn(l0_ffn1_w_in, l0_ffn1_w_out), l0_ln1_g, l0_ln1_b)
    w_in = l0_mix_w_in.astype(BF16)
    qkv = _project(h, w_in[:, :A_QKV_WIDTH], F32)
    uv = _project(h, w_in[:, A_QKV_WIDTH:], F32)
    a_out = _dilated_attention(qkv, batch, seq)
    b_out = _gmlp(uv, l0_gmlp_ln_g, l0_gmlp_ln_b, l0_gmlp_w_s, l0_gmlp_b_s)
    mixed = jnp.concatenate([a_out, b_out], axis=1)
    h = _outproj_deepnorm(mixed, l0_mix_w_out.astype(BF16), h, l0_ln2_g, l0_ln2_b)
    h = _mem_block(h, mem2, l0_mem_w_q, l0_mem_w_kv, l0_mem_w_o, l0_ln3_g, l0_ln3_b, batch, seq, mem_len)
    h = _ffn_deepnorm(h, *_prep_ffn(l0_ffn2_w_in, l0_ffn2_w_out), l0_ln4_g, l0_ln4_b)

    h = _ffn_deepnorm(h, *_prep_ffn(l1_ffn1_w_in, l1_ffn1_w_out), l1_ln1_g, l1_ln1_b)
    qkv = _project(h, l1_mix_w_in.astype(BF16), BF16)
    o = _moba(qkv, batch, seq)
    h = _outproj_deepnorm(o, l1_mix_w_out.astype(BF16), h, l1_ln2_g, l1_ln2_b)
    h = _mem_block(h, mem2, l1_mem_w_q, l1_mem_w_kv, l1_mem_w_o, l1_ln3_g, l1_ln3_b, batch, seq, mem_len)
    h = _ffn_deepnorm(h, *_prep_ffn(l1_ffn2_w_in, l1_ffn2_w_out), l1_ln4_g, l1_ln4_b)
    return h.reshape(batch, seq, d)
```

```python
import functools

import jax
import jax.numpy as jnp
from jax import lax
from jax.experimental import pallas as pl
from jax.experimental.pallas import tpu as pltpu

D_MODEL = 2048
DEPTH = 2
HEAD_DIM = 64
HEAD_PAIR = 2 * HEAD_DIM
DIL_GROUPS = ((128, 1), (512, 4), (2048, 16))
A_HEADS_PER_GROUP = 8
A_HEADS = A_HEADS_PER_GROUP * len(DIL_GROUPS)
BAND_BLOCK = 128
CHUNK = 128
B_GROUPS = 8
B_WIDTH = 1024
B_GROUP_DIM = B_WIDTH // B_GROUPS
C_HEADS = D_MODEL // HEAD_DIM
MOBA_BLOCK = 256
MOBA_TOPK = 3
X_HEADS = 4
X_HEAD_DIM = D_MODEL // X_HEADS
D_FF = 5504
DEEPNORM_ALPHA = (2 * DEPTH) ** 0.25
LN_EPS = 1e-5
A_QKV_WIDTH = 3 * A_HEADS * HEAD_DIM

LANES = 128
FF_TILE = 512
D_FF_PAD = -(-D_FF // FF_TILE) * FF_TILE
ROW_TILE = 512
VMEM_LIMIT = 56 * 1024 * 1024
NEG_BIG = -1e30
M_INIT = -1e29

BF16 = jnp.bfloat16
F32 = jnp.float32


def _params(*sem):
    return pltpu.CompilerParams(dimension_semantics=sem, vmem_limit_bytes=VMEM_LIMIT)


def _dot(a, b):
    return jnp.dot(a, b, preferred_element_type=F32)


def _dot_nt(a, b):
    return lax.dot_general(a, b, (((1,), (1,)), ((), ())), preferred_element_type=F32)


def _layer_norm(y, g, b):
    mu = jnp.mean(y, axis=-1, keepdims=True)
    yc = y - mu
    var = jnp.mean(yc * yc, axis=-1, keepdims=True)
    return yc * lax.rsqrt(var + LN_EPS) * g + b


def _ffn_kernel(x_ref, wg_ref, wu_ref, wo_ref, g_ref, b_ref, o_ref, xb_ref, acc_ref):
    f = pl.program_id(1)

    @pl.when(f == 0)
    def _():
        xb_ref[...] = x_ref[...].astype(BF16)
        acc_ref[...] = jnp.zeros_like(acc_ref)

    xb = xb_ref[...]
    gate = _dot(xb, wg_ref[...])
    up = _dot(xb, wu_ref[...])
    act = (gate * jax.nn.sigmoid(gate)) * up
    acc_ref[...] += _dot(act.astype(BF16), wo_ref[...])

    @pl.when(f == pl.num_programs(1) - 1)
    def _():
        y = DEEPNORM_ALPHA * x_ref[...] + 0.5 * acc_ref[...]
        o_ref[...] = _layer_norm(y, g_ref[...], b_ref[...])


def _ffn_deepnorm(x, w_in_p, w_out_p, g, b):
    n, d = x.shape
    nf = D_FF_PAD // FF_TILE
    return pl.pallas_call(
        _ffn_kernel,
        grid=(n // ROW_TILE, nf),
        in_specs=[
            pl.BlockSpec((ROW_TILE, d), lambda i, f: (i, 0)),
            pl.BlockSpec((d, FF_TILE), lambda i, f: (0, f)),
            pl.BlockSpec((d, FF_TILE), lambda i, f: (0, f + nf)),
            pl.BlockSpec((FF_TILE, d), lambda i, f: (f, 0)),
            pl.BlockSpec((1, d), lambda i, f: (0, 0)),
            pl.BlockSpec((1, d), lambda i, f: (0, 0)),
        ],
        out_specs=pl.BlockSpec((ROW_TILE, d), lambda i, f: (i, 0)),
        out_shape=jax.ShapeDtypeStruct((n, d), F32),
        scratch_shapes=[pltpu.VMEM((ROW_TILE, d), BF16), pltpu.VMEM((ROW_TILE, d), F32)],
        compiler_params=_params("parallel", "arbitrary"),
        name="ffn_deepnorm",
    )(x, w_in_p, w_in_p, w_out_p, g.reshape(1, d), b.reshape(1, d))


def _proj_kernel(x_ref, w_ref, o_ref):
    o_ref[...] = _dot(x_ref[...].astype(BF16), w_ref[...]).astype(o_ref.dtype)


def _project(x, w, out_dtype, col_tile=512, row_tile=1024):
    n, k = x.shape
    m = w.shape[1]
    row_tile = min(row_tile, n)
    return pl.pallas_call(
        _proj_kernel,
        grid=(n // row_tile, m // col_tile),
        in_specs=[
            pl.BlockSpec((row_tile, k), lambda i, j: (i, 0)),
            pl.BlockSpec((k, col_tile), lambda i, j: (0, j)),
        ],
        out_specs=pl.BlockSpec((row_tile, col_tile), lambda i, j: (i, j)),
        out_shape=jax.ShapeDtypeStruct((n, m), out_dtype),
        compiler_params=_params("parallel", "arbitrary"),
        name="project",
    )(x, w)


def _outproj_kernel(a_ref, w_ref, x_ref, g_ref, b_ref, o_ref):
    y = DEEPNORM_ALPHA * x_ref[...] + _dot(a_ref[...], w_ref[...])
    o_ref[...] = _layer_norm(y, g_ref[...], b_ref[...])


def _outproj_deepnorm(a, w, x, g, b):
    n, k = a.shape
    d = w.shape[1]
    return pl.pallas_call(
        _outproj_kernel,
        grid=(n // ROW_TILE,),
        in_specs=[
            pl.BlockSpec((ROW_TILE, k), lambda i: (i, 0)),
            pl.BlockSpec((k, d), lambda i: (0, 0)),
            pl.BlockSpec((ROW_TILE, d), lambda i: (i, 0)),
            pl.BlockSpec((1, d), lambda i: (0, 0)),
            pl.BlockSpec((1, d), lambda i: (0, 0)),
        ],
        out_specs=pl.BlockSpec((ROW_TILE, d), lambda i: (i, 0)),
        out_shape=jax.ShapeDtypeStruct((n, d), F32),
        compiler_params=_params("parallel"),
        name="outproj_deepnorm",
    )(a, w, x, g.reshape(1, d), b.reshape(1, d))


def _dilated_kernel(*refs, seq):
    n_g = len(DIL_GROUPS)
    qkv_refs = refs[:3 * n_g]
    o_ref = refs[3 * n_g]
    o_sc, lse_sc, bias_sc = refs[3 * n_g + 1:]

    bb = BAND_BLOCK
    qi = lax.broadcasted_iota(jnp.int32, (bb, 2 * bb), 0)
    ki = lax.broadcasted_iota(jnp.int32, (bb, 2 * bb), 1)
    for first, delta in ((0, bb), (1, 0)):
        off = qi + delta - ki
        bias_sc[first] = jnp.where((off >= 0) & (off <= bb), 0.0, -jnp.inf).astype(F32)

    lane = lax.broadcasted_iota(jnp.int32, (bb, HEAD_PAIR), 1)
    head0 = lane < HEAD_DIM
    scale = HEAD_DIM ** -0.5

    for g, (window, dil) in enumerate(DIL_GROUPS):
        assert window // dil == bb
        q_ref, k_ref, v_ref = qkv_refs[3 * g:3 * g + 3]
        n_blk = seq // (dil * bb)

        def body(it, carry, q_ref=q_ref, k_ref=k_ref, v_ref=v_ref, dil=dil, g=g):
            blk = it // dil
            phase = it - blk * dil
            is_first = (blk == 0).astype(jnp.int32)
            q_start = blk * (bb * dil) + phase
            k_start = jnp.maximum(blk - 1, 0) * (bb * dil) + phase
            if dil == 1:
                q_rows, k_rows = pl.ds(q_start, bb), pl.ds(k_start, 2 * bb)
            else:
                q_rows = pl.ds(q_start, bb, stride=dil)
                k_rows = pl.ds(k_start, 2 * bb, stride=dil)
            q = q_ref[q_rows, :] * scale
            k = k_ref[k_rows, :].astype(BF16)
            v = v_ref[k_rows, :].astype(BF16)
            bias = bias_sc[is_first]
            outs, lses = [], []
            for h0 in (True, False):
                qh = jnp.where(head0 if h0 else ~head0, q, 0.0).astype(BF16)
                s = _dot_nt(qh, k) + bias
                m = jnp.max(s, axis=-1, keepdims=True)
                p = jnp.exp(s - m)
                l = jnp.sum(p, axis=-1, keepdims=True)
                outs.append(_dot(p.astype(BF16), v) / l)
                lses.append(m + jnp.log(l))
            o_sc[g, q_rows, :] = jnp.where(head0, outs[0], outs[1])
            lse_sc[g, q_rows, :] = jnp.where(head0, lses[0], lses[1])
            return carry

        lax.fori_loop(0, n_blk * dil, body, 0, unroll=8)

    lse = [lse_sc[g] for g in range(n_g)]
    top = functools.reduce(jnp.maximum, lse)
    w = [jnp.exp(x - top) for x in lse]
    den = functools.reduce(jnp.add, w)
    acc = functools.reduce(jnp.add, [w[g] * o_sc[g] for g in range(n_g)])
    o_ref[...] = (acc / den).astype(o_ref.dtype)


def _dilated_attention(qkv, batch, seq):
    n_g = len(DIL_GROUPS)
    pairs = A_HEADS_PER_GROUP // 2
    blocks_per_part = A_HEADS * HEAD_DIM // HEAD_PAIR
    in_specs = []
    for g in range(n_g):
        for part in range(3):
            base = part * blocks_per_part + g * pairs
            in_specs.append(pl.BlockSpec((seq, HEAD_PAIR), lambda b, j, base=base: (b, base + j)))
    return pl.pallas_call(
        functools.partial(_dilated_kernel, seq=seq),
        grid=(batch, pairs),
        in_specs=in_specs,
        out_specs=pl.BlockSpec((seq, HEAD_PAIR), lambda b, j: (b, j)),
        out_shape=jax.ShapeDtypeStruct((batch * seq, pairs * HEAD_PAIR), BF16),
        scratch_shapes=[
            pltpu.VMEM((n_g, seq, HEAD_PAIR), F32),
            pltpu.VMEM((n_g, seq, HEAD_PAIR), F32),
            pltpu.VMEM((2, BAND_BLOCK, 2 * BAND_BLOCK), F32),
        ],
        compiler_params=_params("parallel", "parallel"),
        name="dilated_attention",
    )(*([qkv] * (3 * n_g)))


def _gmlp_kernel(u_ref, v_ref, g_ref, b_ref, ws_ref, bias_ref, o_ref, *, chunks):
    u = jax.nn.gelu(u_ref[...])
    v = _layer_norm(jax.nn.gelu(v_ref[...]), g_ref[...], b_ref[...]).astype(BF16)
    row = lax.broadcasted_iota(jnp.int32, (CHUNK, CHUNK), 0)
    col = lax.broadcasted_iota(jnp.int32, (CHUNK, CHUNK), 1)
    tril = row >= col
    bias = bias_ref[...]
    for grp in range(B_GROUPS):
        w = jnp.where(tril, ws_ref[grp], 0.0).astype(BF16)
        cols = slice(grp * B_GROUP_DIM, (grp + 1) * B_GROUP_DIM)
        for c in range(chunks):
            rows = slice(c * CHUNK, (c + 1) * CHUNK)
            mixed = _dot(w, v[rows, cols]) + bias[:, cols]
            o_ref[rows, cols] = (u[rows, cols] * mixed).astype(o_ref.dtype)


def _gmlp(uv, ln_g, ln_b, w_s, b_s, chunks=4):
    n = uv.shape[0]
    rows = chunks * CHUNK
    bias_full = jnp.repeat(b_s.T, B_GROUP_DIM, axis=1)
    return pl.pallas_call(
        functools.partial(_gmlp_kernel, chunks=chunks),
        grid=(n // rows,),
        in_specs=[
            pl.BlockSpec((rows, B_WIDTH), lambda i: (i, 0)),
            pl.BlockSpec((rows, B_WIDTH), lambda i: (i, 1)),
            pl.BlockSpec((1, B_WIDTH), lambda i: (0, 0)),
            pl.BlockSpec((1, B_WIDTH), lambda i: (0, 0)),
            pl.BlockSpec((B_GROUPS, CHUNK, CHUNK), lambda i: (0, 0, 0)),
            pl.BlockSpec((CHUNK, B_WIDTH), lambda i: (0, 0)),
        ],
        out_specs=pl.BlockSpec((rows, B_WIDTH), lambda i: (i, 0)),
        out_shape=jax.ShapeDtypeStruct((n, B_WIDTH), BF16),
        compiler_params=_params("parallel"),
        name="gmlp",
    )(uv, uv, ln_g.reshape(1, B_WIDTH), ln_b.reshape(1, B_WIDTH), w_s, bias_full)


def _moba_kernel(q_ref, k_ref, v_ref, o_ref, kmean_ref, vt_ref, *, n_blocks):
    blk = MOBA_BLOCK
    n = pl.program_id(2)
    n_rows = kmean_ref.shape[0]

    @pl.when(n == 0)
    def _():
        kmean_ref[...] = jnp.zeros_like(kmean_ref)
        for m in range(n_blocks):
            kb = k_ref[pl.ds(m * blk, blk), :].astype(F32)
            kmean_ref[pl.ds(m, 1), :] = jnp.mean(kb, axis=0, keepdims=True)
            vb = v_ref[pl.ds(m * blk, blk), :].astype(F32)
            vt_ref[m] = jnp.transpose(vb).astype(BF16)

    qt = jnp.transpose(q_ref[...].astype(F32)) * (HEAD_DIM ** -0.5)
    chan = lax.broadcasted_iota(jnp.int32, (HEAD_PAIR, blk), 0)
    kmean = kmean_ref[...].astype(BF16)
    cand = lax.broadcasted_iota(jnp.int32, (n_rows, blk), 0)
    past = cand < n
    key_i = lax.broadcasted_iota(jnp.int32, (blk, blk), 0)
    qry_i = lax.broadcasted_iota(jnp.int32, (blk, blk), 1)
    causal = key_i <= qry_i
    onehot_lane = lax.broadcasted_iota(jnp.int32, (blk, LANES), 1)
    pad_rows = jnp.zeros((LANES - n_rows, blk), BF16)

    w_aug = []
    for h in range(2):
        qt_h = jnp.where((chan < HEAD_DIM) == (h == 0), qt, 0.0).astype(BF16)
        gate = _dot(kmean, qt_h)
        rank = jnp.zeros((n_rows, blk), jnp.int32)
        for m2 in range(n_blocks):
            g2 = gate[m2:m2 + 1, :]
            ahead = (g2 > gate) | ((g2 == gate) & (m2 < cand))
            rank = rank + jnp.where(ahead & (m2 < n), 1, 0)
        chosen = (past & (rank < MOBA_TOPK) & (jnp.abs(gate) < jnp.inf)) | (cand == n)
        sel_bias = jnp.where(chosen, 0.0, NEG_BIG).astype(BF16)
        w_aug.append(jnp.concatenate([qt_h, sel_bias, pad_rows], axis=0))

    def scores(m):
        k_m = k_ref[pl.ds(pl.multiple_of(m * blk, blk), blk), :]
        k_aug = jnp.concatenate([k_m, (onehot_lane == m).astype(BF16)], axis=1)
        return [_dot(k_aug, w_aug[h]) for h in range(2)]

    def flash_update(m_i, l_i, acc, st, vt_m):
        m_new = jnp.maximum(m_i, jnp.max(st, axis=0, keepdims=True))
        alpha = jnp.exp(m_i - m_new)
        p = jnp.exp(st - m_new)
        l_new = alpha * l_i + jnp.sum(p, axis=0, keepdims=True)
        return [m_new, l_new, alpha * acc + _dot(vt_m, p.astype(BF16))]

    def body(m, carry):
        st_next = scores(m + 1)
        out = []
        for h in range(2):
            vt_m = vt_ref[m, h * HEAD_DIM:(h + 1) * HEAD_DIM, :]
            out += flash_update(*carry[3 * h:3 * h + 3], carry[6 + h], vt_m)
        return tuple(out) + tuple(st_next)

    init = [jnp.full((1, blk), M_INIT, F32), jnp.zeros((1, blk), F32), jnp.zeros((HEAD_DIM, blk), F32)]
    state = lax.fori_loop(0, n, body, tuple(init + init + scores(0)))
    outs = []
    for h in range(2):
        st = jnp.where(causal, state[6 + h], NEG_BIG)
        vt_n = vt_ref[n, h * HEAD_DIM:(h + 1) * HEAD_DIM, :]
        _, l_f, acc_f = flash_update(*state[3 * h:3 * h + 3], st, vt_n)
        outs.append(acc_f / l_f)
    o_ref[...] = jnp.transpose(jnp.concatenate(outs, axis=0)).astype(o_ref.dtype)


def _moba(qkv, batch, seq):
    assert seq % MOBA_BLOCK == 0
    n_blocks = seq // MOBA_BLOCK
    n_rows = -(-n_blocks // 8) * 8
    assert n_rows <= LANES
    pairs = C_HEADS // 2
    return pl.pallas_call(
        functools.partial(_moba_kernel, n_blocks=n_blocks),
        grid=(batch, pairs, n_blocks),
        in_specs=[
            pl.BlockSpec((MOBA_BLOCK, HEAD_PAIR), lambda b, j, n: (b * n_blocks + n, j)),
            pl.BlockSpec((seq, HEAD_PAIR), lambda b, j, n: (b, pairs + j)),
            pl.BlockSpec((seq, HEAD_PAIR), lambda b, j, n: (b, 2 * pairs + j)),
        ],
        out_specs=pl.BlockSpec((MOBA_BLOCK, HEAD_PAIR), lambda b, j, n: (b * n_blocks + n, j)),
        out_shape=jax.ShapeDtypeStruct((batch * seq, pairs * HEAD_PAIR), BF16),
        scratch_shapes=[pltpu.VMEM((n_rows, HEAD_PAIR), F32),
                        pltpu.VMEM((n_blocks, HEAD_PAIR, MOBA_BLOCK), BF16)],
        compiler_params=_params("parallel", "parallel", "arbitrary"),
        name="moba",
    )(qkv, qkv, qkv)


def _memattn_kernel(q_ref, k_ref, v_ref, o_ref):
    s = _dot_nt(q_ref[...], k_ref[...]) * (X_HEAD_DIM ** -0.5)
    m = jnp.max(s, axis=-1, keepdims=True)
    p = jnp.exp(s - m)
    l = jnp.sum(p, axis=-1, keepdims=True)
    o_ref[...] = (_dot(p.astype(BF16), v_ref[...]) / l).astype(o_ref.dtype)


def _mem_attention(q, kv, batch, seq, mem_len, q_tile=1024):
    q_tile = min(q_tile, seq)
    tiles = seq // q_tile
    return pl.pallas_call(
        _memattn_kernel,
        grid=(batch, tiles, X_HEADS),
        in_specs=[
            pl.BlockSpec((q_tile, X_HEAD_DIM), lambda b, i, h: (b * tiles + i, h)),
            pl.BlockSpec((mem_len, X_HEAD_DIM), lambda b, i, h: (b, h)),
            pl.BlockSpec((mem_len, X_HEAD_DIM), lambda b, i, h: (b, X_HEADS + h)),
        ],
        out_specs=pl.BlockSpec((q_tile, X_HEAD_DIM), lambda b, i, h: (b * tiles + i, h)),
        out_shape=jax.ShapeDtypeStruct((batch * seq, D_MODEL), BF16),
        compiler_params=_params("parallel", "parallel", "arbitrary"),
        name="mem_attention",
    )(q, kv, kv)


def _prep_ffn(w_in, w_out):
    pad = D_FF_PAD - D_FF
    gate, up = w_in[:, :D_FF], w_in[:, D_FF:]
    w_in_p = jnp.concatenate([jnp.pad(gate, ((0, 0), (0, pad))), jnp.pad(up, ((0, 0), (0, pad)))], axis=1)
    return w_in_p.astype(BF16), jnp.pad(w_out, ((0, pad), (0, 0))).astype(BF16)


def _mem_block(x, mem2, w_q, w_kv, w_o, g, b, batch, seq, mem_len):
    q = _project(x, w_q.astype(BF16), BF16)
    kv = _project(mem2, w_kv.astype(BF16), BF16)
    o = _mem_attention(q, kv, batch, seq, mem_len)
    return _outproj_deepnorm(o, w_o.astype(BF16), x, g, b)


def kernel(x, mem, l0_ffn1_w_in, l0_ffn1_w_out, l0_ln1_g, l0_ln1_b, l0_mix_w_in, l0_gmlp_ln_g, l0_gmlp_ln_b, l0_gmlp_w_s, l0_gmlp_b_s, l0_mix_w_out, l0_ln2_g, l0_ln2_b, l0_mem_w_q, l0_mem_w_kv, l0_mem_w_o, l0_ln3_g, l0_ln3_b, l0_ffn2_w_in, l0_ffn2_w_out, l0_ln4_g, l0_ln4_b, l1_ffn1_w_in, l1_ffn1_w_out, l1_ln1_g, l1_ln1_b, l1_mix_w_in, l1_mix_w_out, l1_ln2_g, l1_ln2_b, l1_mem_w_q, l1_mem_w_kv, l1_mem_w_o, l1_ln3_g, l1_ln3_b, l1_ffn2_w_in, l1_ffn2_w_out, l1_ln4_g, l1_ln4_b):
    batch, seq, d = x.shape
    mem_len = mem.shape[1]
    h = x.reshape(batch * seq, d)
    mem2 = mem.reshape(batch * mem_len, d)

    h = _ffn_deepnorm(h, *_prep_ffn(l0_ffn1_w_in, l0_ffn1_w_out), l0_ln1_g, l0_ln1_b)
    w_in = l0_mix_w_in.astype(BF16)
    qkv = _project(h, w_in[:, :A_QKV_WIDTH], F32)
    uv = _project(h, w_in[:, A_QKV_WIDTH:], F32)
    a_out = _dilated_attention(qkv, batch, seq)
    b_out = _gmlp(uv, l0_gmlp_ln_g, l0_gmlp_ln_b, l0_gmlp_w_s, l0_gmlp_b_s)
    mixed = jnp.concatenate([a_out, b_out], axis=1)
    h = _outproj_deepnorm(mixed, l0_mix_w_out.astype(BF16), h, l0_ln2_g, l0_ln2_b)
    h = _mem_block(h, mem2, l0_mem_w_q, l0_mem_w_kv, l0_mem_w_o, l0_ln3_g, l0_ln3_b, batch, seq, mem_len)
    h = _ffn_deepnorm(h, *_prep_ffn(l0_ffn2_w_in, l0_ffn2_w_out), l0_ln4_g, l0_ln4_b)

    h = _ffn_deepnorm(h, *_prep_ffn(l1_ffn1_w_in, l1_ffn1_w_out), l1_ln1_g, l1_ln1_b)
    qkv = _project(h, l1_mix_w_in.astype(BF16), BF16)
    o = _moba(qkv, batch, seq)
    h = _outproj_deepnorm(o, l1_mix_w_out.astype(BF16), h, l1_ln2_g, l1_ln2_b)
    h = _mem_block(h, mem2, l1_mem_w_q, l1_mem_w_kv, l1_mem_w_o, l1_ln3_g, l1_ln3_b, batch, seq, mem_len)
    h = _ffn_deepnorm(h, *_prep_ffn(l1_ffn2_w_in, l1_ffn2_w_out), l1_ln4_g, l1_ln4_b)
    return h.reshape(batch, seq, d)
```

```python
import functools

import jax
import jax.numpy as jnp
from jax import lax
from jax.experimental import pallas as pl
from jax.experimental.pallas import tpu as pltpu

D_MODEL = 2048
DEPTH = 2
HEAD_DIM = 64
HEAD_PAIR = 2 * HEAD_DIM
DIL_GROUPS = ((128, 1), (512, 4), (2048, 16))
A_HEADS_PER_GROUP = 8
A_HEADS = A_HEADS_PER_GROUP * len(DIL_GROUPS)
BAND_BLOCK = 128
CHUNK = 128
B_GROUPS = 8
B_WIDTH = 1024
B_GROUP_DIM = B_WIDTH // B_GROUPS
C_HEADS = D_MODEL // HEAD_DIM
MOBA_BLOCK = 256
MOBA_TOPK = 3
X_HEADS = 4
X_HEAD_DIM = D_MODEL // X_HEADS
D_FF = 5504
DEEPNORM_ALPHA = (2 * DEPTH) ** 0.25
LN_EPS = 1e-5
A_QKV_WIDTH = 3 * A_HEADS * HEAD_DIM

LANES = 128
FF_TILE = 512
D_FF_PAD = -(-D_FF // FF_TILE) * FF_TILE
ROW_TILE = 512
VMEM_LIMIT = 56 * 1024 * 1024
NEG_BIG = -1e30
M_INIT = -1e29

BF16 = jnp.bfloat16
F32 = jnp.float32


def _params(*sem):
    return pltpu.CompilerParams(dimension_semantics=sem, vmem_limit_bytes=VMEM_LIMIT)


def _dot(a, b):
    return jnp.dot(a, b, preferred_element_type=F32)


def _dot_nt(a, b):
    return lax.dot_general(a, b, (((1,), (1,)), ((), ())), preferred_element_type=F32)


def _layer_norm(y, g, b):
    mu = jnp.mean(y, axis=-1, keepdims=True)
    yc = y - mu
    var = jnp.mean(yc * yc, axis=-1, keepdims=True)
    return yc * lax.rsqrt(var + LN_EPS) * g + b


def _ffn_kernel(x_ref, wg_ref, wu_ref, wo_ref, g_ref, b_ref, o_ref, xb_ref, acc_ref):
    f = pl.program_id(1)

    @pl.when(f == 0)
    def _():
        xb_ref[...] = x_ref[...].astype(BF16)
        acc_ref[...] = jnp.zeros_like(acc_ref)

    xb = xb_ref[...]
    gate = _dot(xb, wg_ref[...])
    up = _dot(xb, wu_ref[...])
    act = (gate * jax.nn.sigmoid(gate)) * up
    acc_ref[...] += _dot(act.astype(BF16), wo_ref[...])

    @pl.when(f == pl.num_programs(1) - 1)
    def _():
        y = DEEPNORM_ALPHA * x_ref[...] + 0.5 * acc_ref[...]
        o_ref[...] = _layer_norm(y, g_ref[...], b_ref[...])


def _ffn_deepnorm(x, w_in_p, w_out_p, g, b):
    n, d = x.shape
    nf = D_FF_PAD // FF_TILE
    return pl.pallas_call(
        _ffn_kernel,
        grid=(n // ROW_TILE, nf),
        in_specs=[
            pl.BlockSpec((ROW_TILE, d), lambda i, f: (i, 0)),
            pl.BlockSpec((d, FF_TILE), lambda i, f: (0, f)),
            pl.BlockSpec((d, FF_TILE), lambda i, f: (0, f + nf)),
            pl.BlockSpec((FF_TILE, d), lambda i, f: (f, 0)),
            pl.BlockSpec((1, d), lambda i, f: (0, 0)),
            pl.BlockSpec((1, d), lambda i, f: (0, 0)),
        ],
        out_specs=pl.BlockSpec((ROW_TILE, d), lambda i, f: (i, 0)),
        out_shape=jax.ShapeDtypeStruct((n, d), F32),
        scratch_shapes=[pltpu.VMEM((ROW_TILE, d), BF16), pltpu.VMEM((ROW_TILE, d), F32)],
        compiler_params=_params("parallel", "arbitrary"),
        name="ffn_deepnorm",
    )(x, w_in_p, w_in_p, w_out_p, g.reshape(1, d), b.reshape(1, d))


def _proj_kernel(x_ref, w_ref, o_ref):
    o_ref[...] = _dot(x_ref[...].astype(BF16), w_ref[...]).astype(o_ref.dtype)


def _project(x, w, out_dtype, col_tile=512, row_tile=1024):
    n, k = x.shape
    m = w.shape[1]
    row_tile = min(row_tile, n)
    return pl.pallas_call(
        _proj_kernel,
        grid=(n // row_tile, m // col_tile),
        in_specs=[
            pl.BlockSpec((row_tile, k), lambda i, j: (i, 0)),
            pl.BlockSpec((k, col_tile), lambda i, j: (0, j)),
        ],
        out_specs=pl.BlockSpec((row_tile, col_tile), lambda i, j: (i, j)),
        out_shape=jax.ShapeDtypeStruct((n, m), out_dtype),
        compiler_params=_params("parallel", "arbitrary"),
        name="project",
    )(x, w)


def _outproj_kernel(a_ref, w_ref, x_ref, g_ref, b_ref, o_ref):
    y = DEEPNORM_ALPHA * x_ref[...] + _dot(a_ref[...], w_ref[...])
    o_ref[...] = _layer_norm(y, g_ref[...], b_ref[...])


def _outproj_deepnorm(a, w, x, g, b):
    n, k = a.shape
    d = w.shape[1]
    return pl.pallas_call(
        _outproj_kernel,
        grid=(n // ROW_TILE,),
        in_specs=[
            pl.BlockSpec((ROW_TILE, k), lambda i: (i, 0)),
            pl.BlockSpec((k, d), lambda i: (0, 0)),
            pl.BlockSpec((ROW_TILE, d), lambda i: (i, 0)),
            pl.BlockSpec((1, d), lambda i: (0, 0)),
            pl.BlockSpec((1, d), lambda i: (0, 0)),
        ],
        out_specs=pl.BlockSpec((ROW_TILE, d), lambda i: (i, 0)),
        out_shape=jax.ShapeDtypeStruct((n, d), F32),
        compiler_params=_params("parallel"),
        name="outproj_deepnorm",
    )(a, w, x, g.reshape(1, d), b.reshape(1, d))


def _dilated_kernel(*refs, seq):
    n_g = len(DIL_GROUPS)
    qkv_refs = refs[:3 * n_g]
    o_ref = refs[3 * n_g]
    o_sc, lse_sc, bias_sc = refs[3 * n_g + 1:]

    bb = BAND_BLOCK
    qi = lax.broadcasted_iota(jnp.int32, (bb, 2 * bb), 0)
    ki = lax.broadcasted_iota(jnp.int32, (bb, 2 * bb), 1)
    for first, delta in ((0, bb), (1, 0)):
        off = qi + delta - ki
        bias_sc[first] = jnp.where((off >= 0) & (off <= bb), 0.0, -jnp.inf).astype(F32)

    lane = lax.broadcasted_iota(jnp.int32, (bb, HEAD_PAIR), 1)
    head0 = lane < HEAD_DIM
    scale = HEAD_DIM ** -0.5

    for g, (window, dil) in enumerate(DIL_GROUPS):
        assert window // dil == bb
        q_ref, k_ref, v_ref = qkv_refs[3 * g:3 * g + 3]
        n_blk = seq // (dil * bb)

        def body(it, carry, q_ref=q_ref, k_ref=k_ref, v_ref=v_ref, dil=dil, g=g):
            blk = it // dil
            phase = it - blk * dil
            is_first = jnp.where(blk == 0, 1, 0)
            q_start = blk * (bb * dil) + phase
            k_start = jnp.maximum(blk - 1, 0) * (bb * dil) + phase
            if dil == 1:
                q_rows, k_rows = pl.ds(q_start, bb), pl.ds(k_start, 2 * bb)
            else:
                q_rows = pl.ds(q_start, bb, stride=dil)
                k_rows = pl.ds(k_start, 2 * bb, stride=dil)
            q = q_ref[q_rows, :] * scale
            k = k_ref[k_rows, :].astype(BF16)
            v = v_ref[k_rows, :].astype(BF16)
            bias = bias_sc[is_first]
            outs, lses = [], []
            for h0 in (True, False):
                qh = jnp.where(head0 if h0 else ~head0, q, 0.0).astype(BF16)
                s = _dot_nt(qh, k) + bias
                m = jnp.max(s, axis=-1, keepdims=True)
                p = jnp.exp(s - m)
                l = jnp.sum(p, axis=-1, keepdims=True)
                outs.append(_dot(p.astype(BF16), v) / l)
                lses.append(m + jnp.log(l))
            o_sc[g, q_rows, :] = jnp.where(head0, outs[0], outs[1])
            lse_sc[g, q_rows, :] = jnp.where(head0, lses[0], lses[1])
            return carry

        lax.fori_loop(0, n_blk * dil, body, 0, unroll=8)

    lse = [lse_sc[g] for g in range(n_g)]
    top = functools.reduce(jnp.maximum, lse)
    w = [jnp.exp(x - top) for x in lse]
    den = functools.reduce(jnp.add, w)
    acc = functools.reduce(jnp.add, [w[g] * o_sc[g] for g in range(n_g)])
    o_ref[...] = (acc / den).astype(o_ref.dtype)


def _dilated_attention(qkv, batch, seq):
    n_g = len(DIL_GROUPS)
    pairs = A_HEADS_PER_GROUP // 2
    blocks_per_part = A_HEADS * HEAD_DIM // HEAD_PAIR
    in_specs = []
    for g in range(n_g):
        for part in range(3):
            base = part * blocks_per_part + g * pairs
            in_specs.append(pl.BlockSpec((seq, HEAD_PAIR), lambda b, j, base=base: (b, base + j)))
    return pl.pallas_call(
        functools.partial(_dilated_kernel, seq=seq),
        grid=(batch, pairs),
        in_specs=in_specs,
        out_specs=pl.BlockSpec((seq, HEAD_PAIR), lambda b, j: (b, j)),
        out_shape=jax.ShapeDtypeStruct((batch * seq, pairs * HEAD_PAIR), BF16),
        scratch_shapes=[
            pltpu.VMEM((n_g, seq, HEAD_PAIR), F32),
            pltpu.VMEM((n_g, seq, HEAD_PAIR), F32),
            pltpu.VMEM((2, BAND_BLOCK, 2 * BAND_BLOCK), F32),
        ],
        compiler_params=_params("parallel", "parallel"),
        name="dilated_attention",
    )(*([qkv] * (3 * n_g)))


def _gmlp_kernel(u_ref, v_ref, g_ref, b_ref, ws_ref, bias_ref, o_ref, *, chunks):
    u = jax.nn.gelu(u_ref[...])
    v = _layer_norm(jax.nn.gelu(v_ref[...]), g_ref[...], b_ref[...]).astype(BF16)
    row = lax.broadcasted_iota(jnp.int32, (CHUNK, CHUNK), 0)
    col = lax.broadcasted_iota(jnp.int32, (CHUNK, CHUNK), 1)
    tril = row >= col
    bias = bias_ref[...]
    for grp in range(B_GROUPS):
        w = jnp.where(tril, ws_ref[grp], 0.0).astype(BF16)
        cols = slice(grp * B_GROUP_DIM, (grp + 1) * B_GROUP_DIM)
        for c in range(chunks):
            rows = slice(c * CHUNK, (c + 1) * CHUNK)
            mixed = _dot(w, v[rows, cols]) + bias[:, cols]
            o_ref[rows, cols] = (u[rows, cols] * mixed).astype(o_ref.dtype)


def _gmlp(uv, ln_g, ln_b, w_s, b_s, chunks=4):
    n = uv.shape[0]
    rows = chunks * CHUNK
    bias_full = jnp.repeat(b_s.T, B_GROUP_DIM, axis=1)
    return pl.pallas_call(
        functools.partial(_gmlp_kernel, chunks=chunks),
        grid=(n // rows,),
        in_specs=[
            pl.BlockSpec((rows, B_WIDTH), lambda i: (i, 0)),
            pl.BlockSpec((rows, B_WIDTH), lambda i: (i, 1)),
            pl.BlockSpec((1, B_WIDTH), lambda i: (0, 0)),
            pl.BlockSpec((1, B_WIDTH), lambda i: (0, 0)),
            pl.BlockSpec((B_GROUPS, CHUNK, CHUNK), lambda i: (0, 0, 0)),
            pl.BlockSpec((CHUNK, B_WIDTH), lambda i: (0, 0)),
        ],
        out_specs=pl.BlockSpec((rows, B_WIDTH), lambda i: (i, 0)),
        out_shape=jax.ShapeDtypeStruct((n, B_WIDTH), BF16),
        compiler_params=_params("parallel"),
        name="gmlp",
    )(uv, uv, ln_g.reshape(1, B_WIDTH), ln_b.reshape(1, B_WIDTH), w_s, bias_full)


VT_ROWS = HEAD_DIM + 16


def _moba_seq_kernel(q_ref, k_ref, v_ref, o_ref, kmean_sc, qt_sc, vt_sc, selb_sc, st_sc, *, n_blocks):
    blk = MOBA_BLOCK
    n_rows = kmean_sc.shape[0]
    scale = HEAD_DIM ** -0.5

    kmean_sc[...] = jnp.zeros_like(kmean_sc)
    ones_rows = (lax.broadcasted_iota(jnp.int32, (VT_ROWS - HEAD_DIM, blk), 0) == 0).astype(BF16)

    def prep(m, carry):
        rows = pl.ds(pl.multiple_of(m * blk, blk), blk)
        kmean_sc[pl.ds(m, 1), :] = jnp.mean(k_ref[rows, :].astype(F32), axis=0, keepdims=True)
        qt_sc[m] = (jnp.transpose(q_ref[rows, :].astype(F32)) * scale).astype(BF16)
        vt = jnp.transpose(v_ref[rows, :].astype(F32)).astype(BF16)
        for h in range(2):
            vt_sc[m, h] = jnp.concatenate([vt[h * HEAD_DIM:(h + 1) * HEAD_DIM], ones_rows], axis=0)
        return carry

    lax.fori_loop(0, n_blocks, prep, 0, unroll=2)

    chan = lax.broadcasted_iota(jnp.int32, (HEAD_PAIR, blk), 0)
    head_rows = [chan < HEAD_DIM, chan >= HEAD_DIM]
    kmean = kmean_sc[...].astype(BF16)
    cand = lax.broadcasted_iota(jnp.int32, (n_rows, blk), 0)

    def select(n, carry):
        qt = qt_sc[n]
        for h in range(2):
            gate = _dot(kmean, jnp.where(head_rows[h], qt, 0))
            rank = jnp.zeros((n_rows, blk), jnp.int32)
            for m2 in range(n_blocks):
                g2 = gate[m2:m2 + 1, :]
                ahead = (g2 > gate) | ((g2 == gate) & (m2 < cand))
                rank = rank + jnp.where(ahead & (m2 < n), 1, 0)
            chosen = ((cand < n) & (rank < MOBA_TOPK) & (jnp.abs(gate) < jnp.inf)) | (cand == n)
            selb_sc[n, h] = jnp.where(chosen, 0.0, NEG_BIG).astype(BF16)
        return carry

    lax.fori_loop(0, n_blocks, select, 0, unroll=2)

    key_i = lax.broadcasted_iota(jnp.int32, (blk, blk), 0)
    qry_i = lax.broadcasted_iota(jnp.int32, (blk, blk), 1)
    causal = key_i <= qry_i
    onehot_lane = lax.broadcasted_iota(jnp.int32, (blk, LANES), 1)
    pad_rows = jnp.zeros((LANES - n_rows, blk), BF16)

    def softmax_stage(m_run, st):
        m_new = jnp.maximum(m_run, jnp.max(st, axis=0, keepdims=True))
        return m_new, jnp.exp(m_run - m_new), jnp.exp(st - m_new).astype(BF16)

    def query_block(n, carry):
        qt = qt_sc[n]
        w_aug = [jnp.concatenate([jnp.where(head_rows[h], qt, 0), selb_sc[n, h], pad_rows], axis=0)
                 for h in range(2)]

        def scores(m):
            k_m = k_ref[pl.ds(pl.multiple_of(m * blk, blk), blk), :]
            k_aug = jnp.concatenate([k_m, (onehot_lane == m).astype(BF16)], axis=1)
            return [_dot(k_aug, w_aug[h]) for h in range(2)]

        def body(i, state):
            m_run, acc = state
            st_cur = [st_sc[i % 2, h] for h in range(2)]
            st_next = scores(i + 1)
            out_m, out_acc = [], []
            for h in range(2):
                st_sc[(i + 1) % 2, h] = st_next[h]
                m_new, alpha, p = softmax_stage(m_run[h], st_cur[h])
                out_m.append(m_new)
                out_acc.append(alpha * acc[h] + _dot(vt_sc[i, h], p))
            return out_m, out_acc

        for h, st in enumerate(scores(0)):
            st_sc[0, h] = st
        init = ([jnp.full((1, blk), M_INIT, F32)] * 2, [jnp.zeros((VT_ROWS, blk), F32)] * 2)
        m_run, acc = lax.fori_loop(0, n, body, init)
        outs = []
        for h in range(2):
            _, alpha, p = softmax_stage(m_run[h], jnp.where(causal, st_sc[n % 2, h], NEG_BIG))
            acc_h = alpha * acc[h] + _dot(vt_sc[n, h], p)
            outs.append(acc_h[:HEAD_DIM] / acc_h[HEAD_DIM:HEAD_DIM + 1])
        rows = pl.ds(pl.multiple_of(n * blk, blk), blk)
        o_ref[rows, :] = jnp.transpose(jnp.concatenate(outs, axis=0)).astype(o_ref.dtype)
        return carry

    lax.fori_loop(0, n_blocks, query_block, 0)


def _moba(qkv, batch, seq):
    assert seq % MOBA_BLOCK == 0
    n_blocks = seq // MOBA_BLOCK
    n_rows = -(-n_blocks // 8) * 8
    assert n_rows <= LANES
    pairs = C_HEADS // 2
    return pl.pallas_call(
        functools.partial(_moba_seq_kernel, n_blocks=n_blocks),
        grid=(batch, pairs),
        in_specs=[
            pl.BlockSpec((seq, HEAD_PAIR), lambda b, j: (b, j)),
            pl.BlockSpec((seq, HEAD_PAIR), lambda b, j: (b, pairs + j)),
            pl.BlockSpec((seq, HEAD_PAIR), lambda b, j: (b, 2 * pairs + j)),
        ],
        out_specs=pl.BlockSpec((seq, HEAD_PAIR), lambda b, j: (b, j)),
        out_shape=jax.ShapeDtypeStruct((batch * seq, pairs * HEAD_PAIR), BF16),
        scratch_shapes=[
            pltpu.VMEM((n_rows, HEAD_PAIR), F32),
            pltpu.VMEM((n_blocks, HEAD_PAIR, MOBA_BLOCK), BF16),
            pltpu.VMEM((n_blocks, 2, VT_ROWS, MOBA_BLOCK), BF16),
            pltpu.VMEM((n_blocks, 2, n_rows, MOBA_BLOCK), BF16),
            pltpu.VMEM((2, 2, MOBA_BLOCK, MOBA_BLOCK), F32),
        ],
        compiler_params=_params("parallel", "parallel"),
        name="moba",
    )(qkv, qkv, qkv)


def _memattn_kernel(q_ref, k_ref, v_ref, o_ref):
    s = _dot_nt(q_ref[...], k_ref[...]) * (X_HEAD_DIM ** -0.5)
    m = jnp.max(s, axis=-1, keepdims=True)
    p = jnp.exp(s - m)
    l = jnp.sum(p, axis=-1, keepdims=True)
    o_ref[...] = (_dot(p.astype(BF16), v_ref[...]) / l).astype(o_ref.dtype)


def _mem_attention(q, kv, batch, seq, mem_len, q_tile=1024):
    q_tile = min(q_tile, seq)
    tiles = seq // q_tile
    return pl.pallas_call(
        _memattn_kernel,
        grid=(batch, tiles, X_HEADS),
        in_specs=[
            pl.BlockSpec((q_tile, X_HEAD_DIM), lambda b, i, h: (b * tiles + i, h)),
            pl.BlockSpec((mem_len, X_HEAD_DIM), lambda b, i, h: (b, h)),
            pl.BlockSpec((mem_len, X_HEAD_DIM), lambda b, i, h: (b, X_HEADS + h)),
        ],
        out_specs=pl.BlockSpec((q_tile, X_HEAD_DIM), lambda b, i, h: (b * tiles + i, h)),
        out_shape=jax.ShapeDtypeStruct((batch * seq, D_MODEL), BF16),
        compiler_params=_params("parallel", "parallel", "arbitrary"),
        name="mem_attention",
    )(q, kv, kv)


def _prep_ffn(w_in, w_out):
    pad = D_FF_PAD - D_FF
    gate, up = w_in[:, :D_FF], w_in[:, D_FF:]
    w_in_p = jnp.concatenate([jnp.pad(gate, ((0, 0), (0, pad))), jnp.pad(up, ((0, 0), (0, pad)))], axis=1)
    return w_in_p.astype(BF16), jnp.pad(w_out, ((0, pad), (0, 0))).astype(BF16)


def _mem_block(x, mem2, w_q, w_kv, w_o, g, b, batch, seq, mem_len):
    q = _project(x, w_q.astype(BF16), BF16)
    kv = _project(mem2, w_kv.astype(BF16), BF16)
    o = _mem_attention(q, kv, batch, seq, mem_len)
    return _outproj_deepnorm(o, w_o.astype(BF16), x, g, b)


def kernel(x, mem, l0_ffn1_w_in, l0_ffn1_w_out, l0_ln1_g, l0_ln1_b, l0_mix_w_in, l0_gmlp_ln_g, l0_gmlp_ln_b, l0_gmlp_w_s, l0_gmlp_b_s, l0_mix_w_out, l0_ln2_g, l0_ln2_b, l0_mem_w_q, l0_mem_w_kv, l0_mem_w_o, l0_ln3_g, l0_ln3_b, l0_ffn2_w_in, l0_ffn2_w_out, l0_ln4_g, l0_ln4_b, l1_ffn1_w_in, l1_ffn1_w_out, l1_ln1_g, l1_ln1_b, l1_mix_w_in, l1_mix_w_out, l1_ln2_g, l1_ln2_b, l1_mem_w_q, l1_mem_w_kv, l1_mem_w_o, l1_ln3_g, l1_ln3_b, l1_ffn2_w_in, l1_ffn2_w_out, l1_ln4_g, l1_ln4_b):
    batch, seq, d = x.shape
    mem_len = mem.shape[1]
    h = x.reshape(batch * seq, d)
    mem2 = mem.reshape(batch * mem_len, d)

    h = _ffn_deepnorm(h, *_prep_ffn(l0_ffn1_w_in, l0_ffn1_w_out), l0_ln1_g, l0_ln1_b)
    w_in = l0_mix_w_in.astype(BF16)
    qkv = _project(h, w_in[:, :A_QKV_WIDTH], F32)
    uv = _project(h, w_in[:, A_QKV_WIDTH:], F32)
    a_out = _dilated_attention(qkv, batch, seq)
    b_out = _gmlp(uv, l0_gmlp_ln_g, l0_gmlp_ln_b, l0_gmlp_w_s, l0_gmlp_b_s)
    mixed = jnp.concatenate([a_out, b_out], axis=1)
    h = _outproj_deepnorm(mixed, l0_mix_w_out.astype(BF16), h, l0_ln2_g, l0_ln2_b)
    h = _mem_block(h, mem2, l0_mem_w_q, l0_mem_w_kv, l0_mem_w_o, l0_ln3_g, l0_ln3_b, batch, seq, mem_len)
    h = _ffn_deepnorm(h, *_prep_ffn(l0_ffn2_w_in, l0_ffn2_w_out), l0_ln4_g, l0_ln4_b)

    h = _ffn_deepnorm(h, *_prep_ffn(l1_ffn1_w_in, l1_ffn1_w_out), l1_ln1_g, l1_ln1_b)
    qkv = _project(h, l1_mix_w_in.astype(BF16), BF16)
    o = _moba(qkv, batch, seq)
    h = _outproj_deepnorm(o, l1_mix_w_out.astype(BF16), h, l1_ln2_g, l1_ln2_b)
    h = _mem_block(h, mem2, l1_mem_w_q, l1_mem_w_kv, l1_mem_w_o, l1_ln3_g, l1_ln3_b, batch, seq, mem_len)
    h = _ffn_deepnorm(h, *_prep_ffn(l1_ffn2_w_in, l1_ffn2_w_out), l1_ln4_g, l1_ln4_b)
    return h.reshape(batch, seq, d)
```

```python
import functools

import jax
import jax.numpy as jnp
from jax import lax
from jax.experimental import pallas as pl
from jax.experimental.pallas import tpu as pltpu

D_MODEL = 2048
DEPTH = 2
HEAD_DIM = 64
HEAD_PAIR = 2 * HEAD_DIM
DIL_GROUPS = ((128, 1), (512, 4), (2048, 16))
A_HEADS_PER_GROUP = 8
A_HEADS = A_HEADS_PER_GROUP * len(DIL_GROUPS)
BAND_BLOCK = 128
CHUNK = 128
B_GROUPS = 8
B_WIDTH = 1024
B_GROUP_DIM = B_WIDTH // B_GROUPS
C_HEADS = D_MODEL // HEAD_DIM
MOBA_BLOCK = 256
MOBA_TOPK = 3
X_HEADS = 4
X_HEAD_DIM = D_MODEL // X_HEADS
D_FF = 5504
DEEPNORM_ALPHA = (2 * DEPTH) ** 0.25
LN_EPS = 1e-5
A_QKV_WIDTH = 3 * A_HEADS * HEAD_DIM

LANES = 128
FF_TILE = 512
D_FF_PAD = -(-D_FF // FF_TILE) * FF_TILE
ROW_TILE = 512
VMEM_LIMIT = 56 * 1024 * 1024
NEG_BIG = -1e30
M_INIT = -1e29

BF16 = jnp.bfloat16
F32 = jnp.float32


def _params(*sem):
    return pltpu.CompilerParams(dimension_semantics=sem, vmem_limit_bytes=VMEM_LIMIT)


def _dot(a, b):
    return jnp.dot(a, b, preferred_element_type=F32)


def _dot_nt(a, b):
    return lax.dot_general(a, b, (((1,), (1,)), ((), ())), preferred_element_type=F32)


def _layer_norm(y, g, b):
    mu = jnp.mean(y, axis=-1, keepdims=True)
    yc = y - mu
    var = jnp.mean(yc * yc, axis=-1, keepdims=True)
    return yc * lax.rsqrt(var + LN_EPS) * g + b


def _ffn_kernel(x_ref, wg_ref, wu_ref, wo_ref, g_ref, b_ref, o_ref, xb_ref, acc_ref, act_ref):
    f = pl.program_id(1)
    nf = pl.num_programs(1) - 1

    def hidden():
        xb = xb_ref[...]
        gate = _dot(xb, wg_ref[...])
        up = _dot(xb, wu_ref[...])
        act_ref[f % 2] = ((gate * jax.nn.sigmoid(gate)) * up).astype(BF16)

    def contribution():
        return _dot(act_ref[(f - 1) % 2], wo_ref[...])

    @pl.when(f == 0)
    def _():
        xb_ref[...] = x_ref[...].astype(BF16)
        acc_ref[...] = jnp.zeros_like(acc_ref)
        hidden()

    @pl.when(jnp.logical_and(f > 0, f < nf))
    def _():
        hidden()
        acc_ref[...] += contribution()

    @pl.when(f == nf)
    def _():
        y = DEEPNORM_ALPHA * x_ref[...] + 0.5 * (acc_ref[...] + contribution())
        o_ref[...] = _layer_norm(y, g_ref[...], b_ref[...])


def _ffn_deepnorm(x, w_in_p, w_out_p, g, b):
    n, d = x.shape
    nf = D_FF_PAD // FF_TILE
    return pl.pallas_call(
        _ffn_kernel,
        grid=(n // ROW_TILE, nf + 1),
        in_specs=[
            pl.BlockSpec((ROW_TILE, d), lambda i, f: (i, 0)),
            pl.BlockSpec((d, FF_TILE), lambda i, f: (0, jnp.minimum(f, nf - 1))),
            pl.BlockSpec((d, FF_TILE), lambda i, f: (0, jnp.minimum(f, nf - 1) + nf)),
            pl.BlockSpec((FF_TILE, d), lambda i, f: (jnp.maximum(f - 1, 0), 0)),
            pl.BlockSpec((1, d), lambda i, f: (0, 0)),
            pl.BlockSpec((1, d), lambda i, f: (0, 0)),
        ],
        out_specs=pl.BlockSpec((ROW_TILE, d), lambda i, f: (i, 0)),
        out_shape=jax.ShapeDtypeStruct((n, d), F32),
        scratch_shapes=[
            pltpu.VMEM((ROW_TILE, d), BF16),
            pltpu.VMEM((ROW_TILE, d), F32),
            pltpu.VMEM((2, ROW_TILE, FF_TILE), BF16),
        ],
        compiler_params=_params("parallel", "arbitrary"),
        name="ffn_deepnorm",
    )(x, w_in_p, w_in_p, w_out_p, g.reshape(1, d), b.reshape(1, d))


def _proj_kernel(x_ref, w_ref, o_ref):
    o_ref[...] = _dot(x_ref[...].astype(BF16), w_ref[...]).astype(o_ref.dtype)


PROJ_MAX_COLS = 2048


def _project(x, w, out_dtype, row_tile=1024):
    n, k = x.shape
    m = w.shape[1]
    row_tile = min(row_tile, n)
    col_tile = max(c for c in range(LANES, PROJ_MAX_COLS + 1, LANES) if m % c == 0)
    return pl.pallas_call(
        _proj_kernel,
        grid=(n // row_tile, m // col_tile),
        in_specs=[
            pl.BlockSpec((row_tile, k), lambda i, j: (i, 0)),
            pl.BlockSpec((k, col_tile), lambda i, j: (0, j)),
        ],
        out_specs=pl.BlockSpec((row_tile, col_tile), lambda i, j: (i, j)),
        out_shape=jax.ShapeDtypeStruct((n, m), out_dtype),
        compiler_params=_params("parallel", "arbitrary"),
        name="project",
    )(x, w)


def _outproj_kernel(a_ref, w_ref, x_ref, g_ref, b_ref, o_ref):
    y = DEEPNORM_ALPHA * x_ref[...] + _dot(a_ref[...], w_ref[...])
    o_ref[...] = _layer_norm(y, g_ref[...], b_ref[...])


def _outproj_deepnorm(a, w, x, g, b):
    n, k = a.shape
    d = w.shape[1]
    return pl.pallas_call(
        _outproj_kernel,
        grid=(n // ROW_TILE,),
        in_specs=[
            pl.BlockSpec((ROW_TILE, k), lambda i: (i, 0)),
            pl.BlockSpec((k, d), lambda i: (0, 0)),
            pl.BlockSpec((ROW_TILE, d), lambda i: (i, 0)),
            pl.BlockSpec((1, d), lambda i: (0, 0)),
            pl.BlockSpec((1, d), lambda i: (0, 0)),
        ],
        out_specs=pl.BlockSpec((ROW_TILE, d), lambda i: (i, 0)),
        out_shape=jax.ShapeDtypeStruct((n, d), F32),
        compiler_params=_params("parallel"),
        name="outproj_deepnorm",
    )(a, w, x, g.reshape(1, d), b.reshape(1, d))


def _dilated_kernel(*refs, seq):
    n_g = len(DIL_GROUPS)
    qkv_refs = refs[:3 * n_g]
    o_ref = refs[3 * n_g]
    o_sc, lse_sc, bias_sc = refs[3 * n_g + 1:]

    bb = BAND_BLOCK
    qi = lax.broadcasted_iota(jnp.int32, (bb, 2 * bb), 0)
    ki = lax.broadcasted_iota(jnp.int32, (bb, 2 * bb), 1)
    for first, delta in ((0, bb), (1, 0)):
        off = qi + delta - ki
        bias_sc[first] = jnp.where((off >= 0) & (off <= bb), 0.0, -jnp.inf).astype(F32)

    lane = lax.broadcasted_iota(jnp.int32, (bb, HEAD_PAIR), 1)
    head0 = lane < HEAD_DIM
    scale = HEAD_DIM ** -0.5

    for g, (window, dil) in enumerate(DIL_GROUPS):
        assert window // dil == bb
        q_ref, k_ref, v_ref = qkv_refs[3 * g:3 * g + 3]
        n_blk = seq // (dil * bb)

        def body(it, carry, q_ref=q_ref, k_ref=k_ref, v_ref=v_ref, dil=dil, g=g):
            blk = it // dil
            phase = it - blk * dil
            is_first = jnp.where(blk == 0, 1, 0)
            q_start = blk * (bb * dil) + phase
            k_start = jnp.maximum(blk - 1, 0) * (bb * dil) + phase
            if dil == 1:
                q_rows, k_rows = pl.ds(q_start, bb), pl.ds(k_start, 2 * bb)
            else:
                q_rows = pl.ds(q_start, bb, stride=dil)
                k_rows = pl.ds(k_start, 2 * bb, stride=dil)
            q = q_ref[q_rows, :] * scale
            k = k_ref[k_rows, :].astype(BF16)
            v = v_ref[k_rows, :].astype(BF16)
            bias = bias_sc[is_first]
            outs, lses = [], []
            for h0 in (True, False):
                qh = jnp.where(head0 if h0 else ~head0, q, 0.0).astype(BF16)
                s = _dot_nt(qh, k) + bias
                m = jnp.max(s, axis=-1, keepdims=True)
                p = jnp.exp(s - m)
                l = jnp.sum(p, axis=-1, keepdims=True)
                outs.append(_dot(p.astype(BF16), v) / l)
                lses.append(m + jnp.log(l))
            o_sc[g, q_rows, :] = jnp.where(head0, outs[0], outs[1])
            lse_sc[g, q_rows, :] = jnp.where(head0, lses[0], lses[1])
            return carry

        lax.fori_loop(0, n_blk * dil, body, 0, unroll=8)

    lse = [lse_sc[g] for g in range(n_g)]
    top = functools.reduce(jnp.maximum, lse)
    w = [jnp.exp(x - top) for x in lse]
    den = functools.reduce(jnp.add, w)
    acc = functools.reduce(jnp.add, [w[g] * o_sc[g] for g in range(n_g)])
    o_ref[...] = (acc / den).astype(o_ref.dtype)


def _dilated_attention(qkv, batch, seq):
    n_g = len(DIL_GROUPS)
    pairs = A_HEADS_PER_GROUP // 2
    blocks_per_part = A_HEADS * HEAD_DIM // HEAD_PAIR
    in_specs = []
    for g in range(n_g):
        for part in range(3):
            base = part * blocks_per_part + g * pairs
            in_specs.append(pl.BlockSpec((seq, HEAD_PAIR), lambda b, j, base=base: (b, base + j)))
    return pl.pallas_call(
        functools.partial(_dilated_kernel, seq=seq),
        grid=(batch, pairs),
        in_specs=in_specs,
        out_specs=pl.BlockSpec((seq, HEAD_PAIR), lambda b, j: (b, j)),
        out_shape=jax.ShapeDtypeStruct((batch * seq, pairs * HEAD_PAIR), BF16),
        scratch_shapes=[
            pltpu.VMEM((n_g, seq, HEAD_PAIR), F32),
            pltpu.VMEM((n_g, seq, HEAD_PAIR), F32),
            pltpu.VMEM((2, BAND_BLOCK, 2 * BAND_BLOCK), F32),
        ],
        compiler_params=_params("parallel", "parallel"),
        name="dilated_attention",
    )(*([qkv] * (3 * n_g)))


def _gmlp_kernel(u_ref, v_ref, g_ref, b_ref, ws_ref, bias_ref, o_ref, *, chunks):
    u = jax.nn.gelu(u_ref[...])
    v = _layer_norm(jax.nn.gelu(v_ref[...]), g_ref[...], b_ref[...]).astype(BF16)
    row = lax.broadcasted_iota(jnp.int32, (CHUNK, CHUNK), 0)
    col = lax.broadcasted_iota(jnp.int32, (CHUNK, CHUNK), 1)
    tril = row >= col
    bias = bias_ref[...]
    for grp in range(B_GROUPS):
        w = jnp.where(tril, ws_ref[grp], 0.0).astype(BF16)
        cols = slice(grp * B_GROUP_DIM, (grp + 1) * B_GROUP_DIM)
        for c in range(chunks):
            rows = slice(c * CHUNK, (c + 1) * CHUNK)
            mixed = _dot(w, v[rows, cols]) + bias[:, cols]
            o_ref[rows, cols] = (u[rows, cols] * mixed).astype(o_ref.dtype)


def _gmlp(uv, ln_g, ln_b, w_s, b_s, chunks=4):
    n = uv.shape[0]
    rows = chunks * CHUNK
    bias_full = jnp.repeat(b_s.T, B_GROUP_DIM, axis=1)
    return pl.pallas_call(
        functools.partial(_gmlp_kernel, chunks=chunks),
        grid=(n // rows,),
        in_specs=[
            pl.BlockSpec((rows, B_WIDTH), lambda i: (i, 0)),
            pl.BlockSpec((rows, B_WIDTH), lambda i: (i, 1)),
            pl.BlockSpec((1, B_WIDTH), lambda i: (0, 0)),
            pl.BlockSpec((1, B_WIDTH), lambda i: (0, 0)),
            pl.BlockSpec((B_GROUPS, CHUNK, CHUNK), lambda i: (0, 0, 0)),
            pl.BlockSpec((CHUNK, B_WIDTH), lambda i: (0, 0)),
        ],
        out_specs=pl.BlockSpec((rows, B_WIDTH), lambda i: (i, 0)),
        out_shape=jax.ShapeDtypeStruct((n, B_WIDTH), BF16),
        compiler_params=_params("parallel"),
        name="gmlp",
    )(uv, uv, ln_g.reshape(1, B_WIDTH), ln_b.reshape(1, B_WIDTH), w_s, bias_full)


VT_ROWS = HEAD_DIM + 16


def _moba_seq_kernel(q_ref, k_ref, v_ref, o_ref, kmean_sc, qt_sc, vt_sc, selb_sc, st_sc, *, n_blocks):
    blk = MOBA_BLOCK
    n_rows = kmean_sc.shape[0]
    scale = HEAD_DIM ** -0.5

    kmean_sc[...] = jnp.zeros_like(kmean_sc)
    ones_rows = (lax.broadcasted_iota(jnp.int32, (VT_ROWS - HEAD_DIM, blk), 0) == 0).astype(BF16)

    def prep(m, carry):
        rows = pl.ds(pl.multiple_of(m * blk, blk), blk)
        kmean_sc[pl.ds(m, 1), :] = jnp.mean(k_ref[rows, :].astype(F32), axis=0, keepdims=True)
        qt_sc[m] = (jnp.transpose(q_ref[rows, :].astype(F32)) * scale).astype(BF16)
        vt = jnp.transpose(v_ref[rows, :].astype(F32)).astype(BF16)
        for h in range(2):
            vt_sc[m, h] = jnp.concatenate([vt[h * HEAD_DIM:(h + 1) * HEAD_DIM], ones_rows], axis=0)
        return carry

    lax.fori_loop(0, n_blocks, prep, 0, unroll=2)

    chan = lax.broadcasted_iota(jnp.int32, (HEAD_PAIR, blk), 0)
    head_rows = [chan < HEAD_DIM, chan >= HEAD_DIM]
    kmean = kmean_sc[...].astype(BF16)
    cand = lax.broadcasted_iota(jnp.int32, (n_rows, blk), 0)

    def select(n, carry):
        qt = qt_sc[n]
        for h in range(2):
            gate = _dot(kmean, jnp.where(head_rows[h], qt, 0))
            rank = jnp.zeros((n_rows, blk), jnp.int32)
            for m2 in range(n_blocks):
                g2 = gate[m2:m2 + 1, :]
                ahead = (g2 > gate) | ((g2 == gate) & (m2 < cand))
                rank = rank + jnp.where(ahead & (m2 < n), 1, 0)
            chosen = ((cand < n) & (rank < MOBA_TOPK) & (jnp.abs(gate) < jnp.inf)) | (cand == n)
            selb_sc[n, h] = jnp.where(chosen, 0.0, NEG_BIG).astype(BF16)
        return carry

    lax.fori_loop(0, n_blocks, select, 0, unroll=2)

    key_i = lax.broadcasted_iota(jnp.int32, (blk, blk), 0)
    qry_i = lax.broadcasted_iota(jnp.int32, (blk, blk), 1)
    causal = key_i <= qry_i
    onehot_lane = lax.broadcasted_iota(jnp.int32, (blk, LANES), 1)
    pad_rows = jnp.zeros((LANES - n_rows, blk), BF16)

    def softmax_stage(m_run, st):
        m_new = jnp.maximum(m_run, jnp.max(st, axis=0, keepdims=True))
        return m_new, jnp.exp(m_run - m_new), jnp.exp(st - m_new).astype(BF16)

    def query_block(n, carry):
        qt = qt_sc[n]
        w_aug = [jnp.concatenate([jnp.where(head_rows[h], qt, 0), selb_sc[n, h], pad_rows], axis=0)
                 for h in range(2)]

        def scores(m):
            k_m = k_ref[pl.ds(pl.multiple_of(m * blk, blk), blk), :]
            k_aug = jnp.concatenate([k_m, (onehot_lane == m).astype(BF16)], axis=1)
            return [_dot(k_aug, w_aug[h]) for h in range(2)]

        def body(i, state):
            m_run, acc = state
            st_cur = [st_sc[i % 2, h] for h in range(2)]
            st_next = scores(i + 1)
            out_m, out_acc = [], []
            for h in range(2):
                st_sc[(i + 1) % 2, h] = st_next[h]
                m_new, alpha, p = softmax_stage(m_run[h], st_cur[h])
                out_m.append(m_new)
                out_acc.append(alpha * acc[h] + _dot(vt_sc[i, h], p))
            return out_m, out_acc

        for h, st in enumerate(scores(0)):
            st_sc[0, h] = st
        init = ([jnp.full((1, blk), M_INIT, F32)] * 2, [jnp.zeros((VT_ROWS, blk), F32)] * 2)
        m_run, acc = lax.fori_loop(0, n, body, init)
        outs = []
        for h in range(2):
            _, alpha, p = softmax_stage(m_run[h], jnp.where(causal, st_sc[n % 2, h], NEG_BIG))
            acc_h = alpha * acc[h] + _dot(vt_sc[n, h], p)
            outs.append(acc_h[:HEAD_DIM] / acc_h[HEAD_DIM:HEAD_DIM + 1])
        rows = pl.ds(pl.multiple_of(n * blk, blk), blk)
        o_ref[rows, :] = jnp.transpose(jnp.concatenate(outs, axis=0)).astype(o_ref.dtype)
        return carry

    lax.fori_loop(0, n_blocks, query_block, 0)


def _moba(qkv, batch, seq):
    assert seq % MOBA_BLOCK == 0
    n_blocks = seq // MOBA_BLOCK
    n_rows = -(-n_blocks // 8) * 8
    assert n_rows <= LANES
    pairs = C_HEADS // 2
    return pl.pallas_call(
        functools.partial(_moba_seq_kernel, n_blocks=n_blocks),
        grid=(batch, pairs),
        in_specs=[
            pl.BlockSpec((seq, HEAD_PAIR), lambda b, j: (b, j)),
            pl.BlockSpec((seq, HEAD_PAIR), lambda b, j: (b, pairs + j)),
            pl.BlockSpec((seq, HEAD_PAIR), lambda b, j: (b, 2 * pairs + j)),
        ],
        out_specs=pl.BlockSpec((seq, HEAD_PAIR), lambda b, j: (b, j)),
        out_shape=jax.ShapeDtypeStruct((batch * seq, pairs * HEAD_PAIR), BF16),
        scratch_shapes=[
            pltpu.VMEM((n_rows, HEAD_PAIR), F32),
            pltpu.VMEM((n_blocks, HEAD_PAIR, MOBA_BLOCK), BF16),
            pltpu.VMEM((n_blocks, 2, VT_ROWS, MOBA_BLOCK), BF16),
            pltpu.VMEM((n_blocks, 2, n_rows, MOBA_BLOCK), BF16),
            pltpu.VMEM((2, 2, MOBA_BLOCK, MOBA_BLOCK), F32),
        ],
        compiler_params=_params("parallel", "parallel"),
        name="moba",
    )(qkv, qkv, qkv)


def _memattn_kernel(q_ref, k_ref, v_ref, o_ref):
    s = _dot_nt(q_ref[...], k_ref[...]) * (X_HEAD_DIM ** -0.5)
    m = jnp.max(s, axis=-1, keepdims=True)
    p = jnp.exp(s - m)
    l = jnp.sum(p, axis=-1, keepdims=True)
    o_ref[...] = (_dot(p.astype(BF16), v_ref[...]) / l).astype(o_ref.dtype)


def _mem_attention(q, kv, batch, seq, mem_len, q_tile=1024):
    q_tile = min(q_tile, seq)
    tiles = seq // q_tile
    return pl.pallas_call(
        _memattn_kernel,
        grid=(batch, tiles, X_HEADS),
        in_specs=[
            pl.BlockSpec((q_tile, X_HEAD_DIM), lambda b, i, h: (b * tiles + i, h)),
            pl.BlockSpec((mem_len, X_HEAD_DIM), lambda b, i, h: (b, h)),
            pl.BlockSpec((mem_len, X_HEAD_DIM), lambda b, i, h: (b, X_HEADS + h)),
        ],
        out_specs=pl.BlockSpec((q_tile, X_HEAD_DIM), lambda b, i, h: (b * tiles + i, h)),
        out_shape=jax.ShapeDtypeStruct((batch * seq, D_MODEL), BF16),
        compiler_params=_params("parallel", "parallel", "arbitrary"),
        name="mem_attention",
    )(q, kv, kv)


def _prep_ffn(w_in, w_out):
    pad = D_FF_PAD - D_FF
    gate, up = w_in[:, :D_FF], w_in[:, D_FF:]
    w_in_p = jnp.concatenate([jnp.pad(gate, ((0, 0), (0, pad))), jnp.pad(up, ((0, 0), (0, pad)))], axis=1)
    return w_in_p.astype(BF16), jnp.pad(w_out, ((0, pad), (0, 0))).astype(BF16)


def _mem_block(x, mem2, w_q, w_kv, w_o, g, b, batch, seq, mem_len):
    q = _project(x, w_q.astype(BF16), BF16)
    kv = _project(mem2, w_kv.astype(BF16), BF16)
    o = _mem_attention(q, kv, batch, seq, mem_len)
    return _outproj_deepnorm(o, w_o.astype(BF16), x, g, b)


def kernel(x, mem, l0_ffn1_w_in, l0_ffn1_w_out, l0_ln1_g, l0_ln1_b, l0_mix_w_in, l0_gmlp_ln_g, l0_gmlp_ln_b, l0_gmlp_w_s, l0_gmlp_b_s, l0_mix_w_out, l0_ln2_g, l0_ln2_b, l0_mem_w_q, l0_mem_w_kv, l0_mem_w_o, l0_ln3_g, l0_ln3_b, l0_ffn2_w_in, l0_ffn2_w_out, l0_ln4_g, l0_ln4_b, l1_ffn1_w_in, l1_ffn1_w_out, l1_ln1_g, l1_ln1_b, l1_mix_w_in, l1_mix_w_out, l1_ln2_g, l1_ln2_b, l1_mem_w_q, l1_mem_w_kv, l1_mem_w_o, l1_ln3_g, l1_ln3_b, l1_ffn2_w_in, l1_ffn2_w_out, l1_ln4_g, l1_ln4_b):
    batch, seq, d = x.shape
    mem_len = mem.shape[1]
    h = x.reshape(batch * seq, d)
    mem2 = mem.reshape(batch * mem_len, d)

    h = _ffn_deepnorm(h, *_prep_ffn(l0_ffn1_w_in, l0_ffn1_w_out), l0_ln1_g, l0_ln1_b)
    w_in = l0_mix_w_in.astype(BF16)
    qkv = _project(h, w_in[:, :A_QKV_WIDTH], F32)
    uv = _project(h, w_in[:, A_QKV_WIDTH:], F32)
    a_out = _dilated_attention(qkv, batch, seq)
    b_out = _gmlp(uv, l0_gmlp_ln_g, l0_gmlp_ln_b, l0_gmlp_w_s, l0_gmlp_b_s)
    mixed = jnp.concatenate([a_out, b_out], axis=1)
    h = _outproj_deepnorm(mixed, l0_mix_w_out.astype(BF16), h, l0_ln2_g, l0_ln2_b)
    h = _mem_block(h, mem2, l0_mem_w_q, l0_mem_w_kv, l0_mem_w_o, l0_ln3_g, l0_ln3_b, batch, seq, mem_len)
    h = _ffn_deepnorm(h, *_prep_ffn(l0_ffn2_w_in, l0_ffn2_w_out), l0_ln4_g, l0_ln4_b)

    h = _ffn_deepnorm(h, *_prep_ffn(l1_ffn1_w_in, l1_ffn1_w_out), l1_ln1_g, l1_ln1_b)
    qkv = _project(h, l1_mix_w_in.astype(BF16), BF16)
    o = _moba(qkv, batch, seq)
    h = _outproj_deepnorm(o, l1_mix_w_out.astype(BF16), h, l1_ln2_g, l1_ln2_b)
    h = _mem_block(h, mem2, l1_mem_w_q, l1_mem_w_kv, l1_mem_w_o, l1_ln3_g, l1_ln3_b, batch, seq, mem_len)
    h = _ffn_deepnorm(h, *_prep_ffn(l1_ffn2_w_in, l1_ffn2_w_out), l1_ln4_g, l1_ln4_b)
    return h.reshape(batch, seq, d)
```

```python
import functools

import jax
import jax.numpy as jnp
from jax import lax
from jax.experimental import pallas as pl
from jax.experimental.pallas import tpu as pltpu

D_MODEL = 2048
DEPTH = 2
HEAD_DIM = 64
HEAD_PAIR = 2 * HEAD_DIM
DIL_GROUPS = ((128, 1), (512, 4), (2048, 16))
A_HEADS_PER_GROUP = 8
A_HEADS = A_HEADS_PER_GROUP * len(DIL_GROUPS)
BAND_BLOCK = 128
CHUNK = 128
B_GROUPS = 8
B_WIDTH = 1024
B_GROUP_DIM = B_WIDTH // B_GROUPS
C_HEADS = D_MODEL // HEAD_DIM
MOBA_BLOCK = 256
MOBA_TOPK = 3
X_HEADS = 4
X_HEAD_DIM = D_MODEL // X_HEADS
D_FF = 5504
DEEPNORM_ALPHA = (2 * DEPTH) ** 0.25
LN_EPS = 1e-5
A_QKV_WIDTH = 3 * A_HEADS * HEAD_DIM

LANES = 128
FF_TILE = 512
D_FF_PAD = -(-D_FF // FF_TILE) * FF_TILE
ROW_TILE = 512
VMEM_LIMIT = 56 * 1024 * 1024
NEG_BIG = -1e30
M_INIT = -1e29

BF16 = jnp.bfloat16
F32 = jnp.float32


def _params(*sem):
    return pltpu.CompilerParams(dimension_semantics=sem, vmem_limit_bytes=VMEM_LIMIT)


def _dot(a, b):
    return jnp.dot(a, b, preferred_element_type=F32)


def _dot_nt(a, b):
    return lax.dot_general(a, b, (((1,), (1,)), ((), ())), preferred_element_type=F32)


def _layer_norm(y, g, b):
    mu = jnp.mean(y, axis=-1, keepdims=True)
    yc = y - mu
    var = jnp.mean(yc * yc, axis=-1, keepdims=True)
    return yc * lax.rsqrt(var + LN_EPS) * g + b


def _ffn_kernel(x_ref, wg_ref, wu_ref, wo_ref, g_ref, b_ref, o_ref, xb_ref, acc_ref):
    f = pl.program_id(1)

    @pl.when(f == 0)
    def _():
        xb_ref[...] = x_ref[...].astype(BF16)
        acc_ref[...] = jnp.zeros_like(acc_ref)

    xb = xb_ref[...]
    gate = _dot(xb, wg_ref[...])
    up = _dot(xb, wu_ref[...])
    act = (gate * jax.nn.sigmoid(gate)) * up
    acc_ref[...] += _dot(act.astype(BF16), wo_ref[...])

    @pl.when(f == pl.num_programs(1) - 1)
    def _():
        y = DEEPNORM_ALPHA * x_ref[...] + 0.5 * acc_ref[...]
        o_ref[...] = _layer_norm(y, g_ref[...], b_ref[...])


def _ffn_deepnorm(x, w_gate_p, w_up_p, w_out_p, g, b):
    n, d = x.shape
    nf = D_FF_PAD // FF_TILE
    return pl.pallas_call(
        _ffn_kernel,
        grid=(n // ROW_TILE, nf),
        in_specs=[
            pl.BlockSpec((ROW_TILE, d), lambda i, f: (i, 0)),
            pl.BlockSpec((d, FF_TILE), lambda i, f: (0, f)),
            pl.BlockSpec((d, FF_TILE), lambda i, f: (0, f)),
            pl.BlockSpec((FF_TILE, d), lambda i, f: (f, 0)),
            pl.BlockSpec((1, d), lambda i, f: (0, 0)),
            pl.BlockSpec((1, d), lambda i, f: (0, 0)),
        ],
        out_specs=pl.BlockSpec((ROW_TILE, d), lambda i, f: (i, 0)),
        out_shape=jax.ShapeDtypeStruct((n, d), F32),
        scratch_shapes=[pltpu.VMEM((ROW_TILE, d), BF16), pltpu.VMEM((ROW_TILE, d), F32)],
        compiler_params=_params("parallel", "arbitrary"),
        name="ffn_deepnorm",
    )(x, w_gate_p, w_up_p, w_out_p, g.reshape(1, d), b.reshape(1, d))


def _proj_kernel(x_ref, w_ref, o_ref):
    o_ref[...] = _dot(x_ref[...].astype(BF16), w_ref[...]).astype(o_ref.dtype)


PROJ_MAX_COLS = 2048


def _project(x, w, out_dtype, row_tile=1024):
    n, k = x.shape
    m = w.shape[1]
    row_tile = min(row_tile, n)
    col_tile = max(c for c in range(LANES, PROJ_MAX_COLS + 1, LANES) if m % c == 0)
    return pl.pallas_call(
        _proj_kernel,
        grid=(n // row_tile, m // col_tile),
        in_specs=[
            pl.BlockSpec((row_tile, k), lambda i, j: (i, 0)),
            pl.BlockSpec((k, col_tile), lambda i, j: (0, j)),
        ],
        out_specs=pl.BlockSpec((row_tile, col_tile), lambda i, j: (i, j)),
        out_shape=jax.ShapeDtypeStruct((n, m), out_dtype),
        compiler_params=_params("parallel", "arbitrary"),
        name="project",
    )(x, w)


OUTPROJ_CHUNK = 256


def _outproj_kernel(*refs, n_in):
    a_refs, w_refs = refs[:n_in], refs[n_in:2 * n_in]
    x_ref, g_ref, b_ref, o_ref = refs[2 * n_in:]
    for c in range(ROW_TILE // OUTPROJ_CHUNK):
        rows = slice(c * OUTPROJ_CHUNK, (c + 1) * OUTPROJ_CHUNK)
        fx = _dot(a_refs[0][rows, :], w_refs[0][...])
        for a_ref, w_ref in zip(a_refs[1:], w_refs[1:]):
            fx = fx + _dot(a_ref[rows, :], w_ref[...])
        y = DEEPNORM_ALPHA * x_ref[rows, :] + fx
        o_ref[rows, :] = _layer_norm(y, g_ref[...], b_ref[...])


def _outproj_deepnorm(a_list, w_list, x, g, b):
    n, d = x.shape
    n_in = len(a_list)
    in_specs = [pl.BlockSpec((ROW_TILE, a.shape[1]), lambda i: (i, 0)) for a in a_list]
    in_specs += [pl.BlockSpec(w.shape, lambda i: (0, 0)) for w in w_list]
    in_specs += [
        pl.BlockSpec((ROW_TILE, d), lambda i: (i, 0)),
        pl.BlockSpec((1, d), lambda i: (0, 0)),
        pl.BlockSpec((1, d), lambda i: (0, 0)),
    ]
    return pl.pallas_call(
        functools.partial(_outproj_kernel, n_in=n_in),
        grid=(n // ROW_TILE,),
        in_specs=in_specs,
        out_specs=pl.BlockSpec((ROW_TILE, d), lambda i: (i, 0)),
        out_shape=jax.ShapeDtypeStruct((n, d), F32),
        compiler_params=_params("parallel"),
        name="outproj_deepnorm",
    )(*a_list, *w_list, x, g.reshape(1, d), b.reshape(1, d))


def _dilated_kernel(*refs, seq):
    n_g = len(DIL_GROUPS)
    qkv_refs = refs[:3 * n_g]
    o_ref = refs[3 * n_g]
    o_sc, lse_sc, bias_sc = refs[3 * n_g + 1:]

    bb = BAND_BLOCK
    qi = lax.broadcasted_iota(jnp.int32, (bb, 2 * bb), 0)
    ki = lax.broadcasted_iota(jnp.int32, (bb, 2 * bb), 1)
    for first, delta in ((0, bb), (1, 0)):
        off = qi + delta - ki
        bias_sc[first] = jnp.where((off >= 0) & (off <= bb), 0.0, -jnp.inf).astype(F32)

    lane = lax.broadcasted_iota(jnp.int32, (bb, HEAD_PAIR), 1)
    head0 = lane < HEAD_DIM
    scale = HEAD_DIM ** -0.5

    for g, (window, dil) in enumerate(DIL_GROUPS):
        assert window // dil == bb
        q_ref, k_ref, v_ref = qkv_refs[3 * g:3 * g + 3]
        n_blk = seq // (dil * bb)

        def body(it, carry, q_ref=q_ref, k_ref=k_ref, v_ref=v_ref, dil=dil, g=g):
            blk = it // dil
            phase = it - blk * dil
            is_first = jnp.where(blk == 0, 1, 0)
            q_start = blk * (bb * dil) + phase
            k_start = jnp.maximum(blk - 1, 0) * (bb * dil) + phase
            if dil == 1:
                q_rows, k_rows = pl.ds(q_start, bb), pl.ds(k_start, 2 * bb)
            else:
                q_rows = pl.ds(q_start, bb, stride=dil)
                k_rows = pl.ds(k_start, 2 * bb, stride=dil)
            q = q_ref[q_rows, :] * scale
            k = k_ref[k_rows, :].astype(BF16)
            v = v_ref[k_rows, :].astype(BF16)
            bias = bias_sc[is_first]
            outs, lses = [], []
            for h0 in (True, False):
                qh = jnp.where(head0 if h0 else ~head0, q, 0.0).astype(BF16)
                s = _dot_nt(qh, k) + bias
                m = jnp.max(s, axis=-1, keepdims=True)
                p = jnp.exp(s - m)
                l = jnp.sum(p, axis=-1, keepdims=True)
                outs.append(_dot(p.astype(BF16), v) / l)
                lses.append(m + jnp.log(l))
            o_sc[g, q_rows, :] = jnp.where(head0, outs[0], outs[1])
            lse_sc[g, q_rows, :] = jnp.where(head0, lses[0], lses[1])
            return carry

        lax.fori_loop(0, n_blk * dil, body, 0, unroll=8)

    lse = [lse_sc[g] for g in range(n_g)]
    top = functools.reduce(jnp.maximum, lse)
    w = [jnp.exp(x - top) for x in lse]
    den = functools.reduce(jnp.add, w)
    acc = functools.reduce(jnp.add, [w[g] * o_sc[g] for g in range(n_g)])
    o_ref[...] = (acc / den).astype(o_ref.dtype)


def _dilated_attention(qkv, batch, seq):
    n_g = len(DIL_GROUPS)
    pairs = A_HEADS_PER_GROUP // 2
    blocks_per_part = A_HEADS * HEAD_DIM // HEAD_PAIR
    in_specs = []
    for g in range(n_g):
        for part in range(3):
            base = part * blocks_per_part + g * pairs
            in_specs.append(pl.BlockSpec((seq, HEAD_PAIR), lambda b, j, base=base: (b, base + j)))
    return pl.pallas_call(
        functools.partial(_dilated_kernel, seq=seq),
        grid=(batch, pairs),
        in_specs=in_specs,
        out_specs=pl.BlockSpec((seq, HEAD_PAIR), lambda b, j: (b, j)),
        out_shape=jax.ShapeDtypeStruct((batch * seq, pairs * HEAD_PAIR), BF16),
        scratch_shapes=[
            pltpu.VMEM((n_g, seq, HEAD_PAIR), F32),
            pltpu.VMEM((n_g, seq, HEAD_PAIR), F32),
            pltpu.VMEM((2, BAND_BLOCK, 2 * BAND_BLOCK), F32),
        ],
        compiler_params=_params("parallel", "parallel"),
        name="dilated_attention",
    )(*([qkv] * (3 * n_g)))


def _gmlp_kernel(u_ref, v_ref, g_ref, b_ref, ws_ref, bias_ref, o_ref, *, chunks):
    u = jax.nn.gelu(u_ref[...])
    v = _layer_norm(jax.nn.gelu(v_ref[...]), g_ref[...], b_ref[...]).astype(BF16)
    row = lax.broadcasted_iota(jnp.int32, (CHUNK, CHUNK), 0)
    col = lax.broadcasted_iota(jnp.int32, (CHUNK, CHUNK), 1)
    tril = row >= col
    bias = bias_ref[...]
    for grp in range(B_GROUPS):
        w = jnp.where(tril, ws_ref[grp], 0.0).astype(BF16)
        cols = slice(grp * B_GROUP_DIM, (grp + 1) * B_GROUP_DIM)
        for c in range(chunks):
            rows = slice(c * CHUNK, (c + 1) * CHUNK)
            mixed = _dot(w, v[rows, cols]) + bias[:, cols]
            o_ref[rows, cols] = (u[rows, cols] * mixed).astype(o_ref.dtype)


def _gmlp(uv, ln_g, ln_b, w_s, b_s, chunks=4):
    n = uv.shape[0]
    rows = chunks * CHUNK
    bias_full = jnp.repeat(b_s.T, B_GROUP_DIM, axis=1)
    return pl.pallas_call(
        functools.partial(_gmlp_kernel, chunks=chunks),
        grid=(n // rows,),
        in_specs=[
            pl.BlockSpec((rows, B_WIDTH), lambda i: (i, 0)),
            pl.BlockSpec((rows, B_WIDTH), lambda i: (i, 1)),
            pl.BlockSpec((1, B_WIDTH), lambda i: (0, 0)),
            pl.BlockSpec((1, B_WIDTH), lambda i: (0, 0)),
            pl.BlockSpec((B_GROUPS, CHUNK, CHUNK), lambda i: (0, 0, 0)),
            pl.BlockSpec((CHUNK, B_WIDTH), lambda i: (0, 0)),
        ],
        out_specs=pl.BlockSpec((rows, B_WIDTH), lambda i: (i, 0)),
        out_shape=jax.ShapeDtypeStruct((n, B_WIDTH), BF16),
        compiler_params=_params("parallel"),
        name="gmlp",
    )(uv, uv, ln_g.reshape(1, B_WIDTH), ln_b.reshape(1, B_WIDTH), w_s, bias_full)


VT_ROWS = HEAD_DIM + 16


def _moba_seq_kernel(q_ref, k_ref, v_ref, o_ref, kmean_sc, qt_sc, vt_sc, selb_sc, st_sc, *, n_blocks):
    blk = MOBA_BLOCK
    n_rows = kmean_sc.shape[0]
    scale = HEAD_DIM ** -0.5

    kmean_sc[...] = jnp.zeros_like(kmean_sc)
    ones_rows = (lax.broadcasted_iota(jnp.int32, (VT_ROWS - HEAD_DIM, blk), 0) == 0).astype(BF16)

    def prep(m, carry):
        rows = pl.ds(pl.multiple_of(m * blk, blk), blk)
        kmean_sc[pl.ds(m, 1), :] = jnp.mean(k_ref[rows, :].astype(F32), axis=0, keepdims=True)
        qt_sc[m] = (jnp.transpose(q_ref[rows, :].astype(F32)) * scale).astype(BF16)
        vt = jnp.transpose(v_ref[rows, :].astype(F32)).astype(BF16)
        for h in range(2):
            vt_sc[m, h] = jnp.concatenate([vt[h * HEAD_DIM:(h + 1) * HEAD_DIM], ones_rows], axis=0)
        return carry

    lax.fori_loop(0, n_blocks, prep, 0, unroll=2)

    chan = lax.broadcasted_iota(jnp.int32, (HEAD_PAIR, blk), 0)
    head_rows = [chan < HEAD_DIM, chan >= HEAD_DIM]
    kmean = kmean_sc[...].astype(BF16)
    cand = lax.broadcasted_iota(jnp.int32, (n_rows, blk), 0)

    def select(n, carry):
        qt = qt_sc[n]
        for h in range(2):
            gate = _dot(kmean, jnp.where(head_rows[h], qt, 0))
            rank = jnp.zeros((n_rows, blk), jnp.int32)
            for m2 in range(n_blocks):
                g2 = gate[m2:m2 + 1, :]
                ahead = (g2 > gate) | ((g2 == gate) & (m2 < cand))
                rank = rank + jnp.where(ahead & (m2 < n), 1, 0)
            chosen = ((cand < n) & (rank < MOBA_TOPK) & (jnp.abs(gate) < jnp.inf)) | (cand == n)
            selb_sc[n, h] = jnp.where(chosen, 0.0, NEG_BIG).astype(BF16)
        return carry

    lax.fori_loop(0, n_blocks, select, 0, unroll=2)

    key_i = lax.broadcasted_iota(jnp.int32, (blk, blk), 0)
    qry_i = lax.broadcasted_iota(jnp.int32, (blk, blk), 1)
    causal = key_i <= qry_i
    onehot_lane = lax.broadcasted_iota(jnp.int32, (blk, LANES), 1)
    pad_rows = jnp.zeros((LANES - n_rows, blk), BF16)

    def softmax_stage(m_run, st):
        m_new = jnp.maximum(m_run, jnp.max(st, axis=0, keepdims=True))
        return m_new, jnp.exp(m_run - m_new), jnp.exp(st - m_new).astype(BF16)

    def query_block(n, carry):
        qt = qt_sc[n]
        w_aug = [jnp.concatenate([jnp.where(head_rows[h], qt, 0), selb_sc[n, h], pad_rows], axis=0)
                 for h in range(2)]

        def scores(m):
            k_m = k_ref[pl.ds(pl.multiple_of(m * blk, blk), blk), :]
            k_aug = jnp.concatenate([k_m, (onehot_lane == m).astype(BF16)], axis=1)
            return [_dot(k_aug, w_aug[h]) for h in range(2)]

        def body(i, state):
            m_run, acc = state
            st_cur = [st_sc[i % 2, h] for h in range(2)]
            st_next = scores(i + 1)
            out_m, out_acc = [], []
            for h in range(2):
                st_sc[(i + 1) % 2, h] = st_next[h]
                m_new, alpha, p = softmax_stage(m_run[h], st_cur[h])
                out_m.append(m_new)
                out_acc.append(alpha * acc[h] + _dot(vt_sc[i, h], p))
            return out_m, out_acc

        for h, st in enumerate(scores(0)):
            st_sc[0, h] = st
        init = ([jnp.full((1, blk), M_INIT, F32)] * 2, [jnp.zeros((VT_ROWS, blk), F32)] * 2)
        m_run, acc = lax.fori_loop(0, n, body, init)
        outs = []
        for h in range(2):
            _, alpha, p = softmax_stage(m_run[h], jnp.where(causal, st_sc[n % 2, h], NEG_BIG))
            acc_h = alpha * acc[h] + _dot(vt_sc[n, h], p)
            outs.append(acc_h[:HEAD_DIM] / acc_h[HEAD_DIM:HEAD_DIM + 1])
        rows = pl.ds(pl.multiple_of(n * blk, blk), blk)
        o_ref[rows, :] = jnp.transpose(jnp.concatenate(outs, axis=0)).astype(o_ref.dtype)
        return carry

    lax.fori_loop(0, n_blocks, query_block, 0)


def _moba(qkv, batch, seq):
    assert seq % MOBA_BLOCK == 0
    n_blocks = seq // MOBA_BLOCK
    n_rows = -(-n_blocks // 8) * 8
    assert n_rows <= LANES
    pairs = C_HEADS // 2
    return pl.pallas_call(
        functools.partial(_moba_seq_kernel, n_blocks=n_blocks),
        grid=(batch, pairs),
        in_specs=[
            pl.BlockSpec((seq, HEAD_PAIR), lambda b, j: (b, j)),
            pl.BlockSpec((seq, HEAD_PAIR), lambda b, j: (b, pairs + j)),
            pl.BlockSpec((seq, HEAD_PAIR), lambda b, j: (b, 2 * pairs + j)),
        ],
        out_specs=pl.BlockSpec((seq, HEAD_PAIR), lambda b, j: (b, j)),
        out_shape=jax.ShapeDtypeStruct((batch * seq, pairs * HEAD_PAIR), BF16),
        scratch_shapes=[
            pltpu.VMEM((n_rows, HEAD_PAIR), F32),
            pltpu.VMEM((n_blocks, HEAD_PAIR, MOBA_BLOCK), BF16),
            pltpu.VMEM((n_blocks, 2, VT_ROWS, MOBA_BLOCK), BF16),
            pltpu.VMEM((n_blocks, 2, n_rows, MOBA_BLOCK), BF16),
            pltpu.VMEM((2, 2, MOBA_BLOCK, MOBA_BLOCK), F32),
        ],
        compiler_params=_params("parallel", "parallel"),
        name="moba",
    )(qkv, qkv, qkv)


def _memattn_kernel(q_ref, k_ref, v_ref, o_ref):
    s = _dot_nt(q_ref[...], k_ref[...]) * (X_HEAD_DIM ** -0.5)
    m = jnp.max(s, axis=-1, keepdims=True)
    p = jnp.exp(s - m)
    l = jnp.sum(p, axis=-1, keepdims=True)
    o_ref[...] = (_dot(p.astype(BF16), v_ref[...]) / l).astype(o_ref.dtype)


def _mem_attention(q, kv, batch, seq, mem_len, q_tile=1024):
    q_tile = min(q_tile, seq)
    tiles = seq // q_tile
    return pl.pallas_call(
        _memattn_kernel,
        grid=(batch, tiles, X_HEADS),
        in_specs=[
            pl.BlockSpec((q_tile, X_HEAD_DIM), lambda b, i, h: (b * tiles + i, h)),
            pl.BlockSpec((mem_len, X_HEAD_DIM), lambda b, i, h: (b, h)),
            pl.BlockSpec((mem_len, X_HEAD_DIM), lambda b, i, h: (b, X_HEADS + h)),
        ],
        out_specs=pl.BlockSpec((q_tile, X_HEAD_DIM), lambda b, i, h: (b * tiles + i, h)),
        out_shape=jax.ShapeDtypeStruct((batch * seq, D_MODEL), BF16),
        compiler_params=_params("parallel", "parallel", "arbitrary"),
        name="mem_attention",
    )(q, kv, kv)


def _prep_ffn(w_in, w_out):
    pad = D_FF_PAD - D_FF
    w_gate = jnp.pad(w_in[:, :D_FF].astype(BF16), ((0, 0), (0, pad)))
    w_up = jnp.pad(w_in[:, D_FF:].astype(BF16), ((0, 0), (0, pad)))
    return w_gate, w_up, jnp.pad(w_out.astype(BF16), ((0, pad), (0, 0)))


def _mem_block(x, mem2, w_q, w_kv, w_o, g, b, batch, seq, mem_len):
    q = _project(x, w_q.astype(BF16), BF16)
    kv = _project(mem2, w_kv.astype(BF16), BF16)
    o = _mem_attention(q, kv, batch, seq, mem_len)
    return _outproj_deepnorm([o], [w_o.astype(BF16)], x, g, b)


def kernel(x, mem, l0_ffn1_w_in, l0_ffn1_w_out, l0_ln1_g, l0_ln1_b, l0_mix_w_in, l0_gmlp_ln_g, l0_gmlp_ln_b, l0_gmlp_w_s, l0_gmlp_b_s, l0_mix_w_out, l0_ln2_g, l0_ln2_b, l0_mem_w_q, l0_mem_w_kv, l0_mem_w_o, l0_ln3_g, l0_ln3_b, l0_ffn2_w_in, l0_ffn2_w_out, l0_ln4_g, l0_ln4_b, l1_ffn1_w_in, l1_ffn1_w_out, l1_ln1_g, l1_ln1_b, l1_mix_w_in, l1_mix_w_out, l1_ln2_g, l1_ln2_b, l1_mem_w_q, l1_mem_w_kv, l1_mem_w_o, l1_ln3_g, l1_ln3_b, l1_ffn2_w_in, l1_ffn2_w_out, l1_ln4_g, l1_ln4_b):
    batch, seq, d = x.shape
    mem_len = mem.shape[1]
    h = x.reshape(batch * seq, d)
    mem2 = mem.reshape(batch * mem_len, d)

    h = _ffn_deepnorm(h, *_prep_ffn(l0_ffn1_w_in, l0_ffn1_w_out), l0_ln1_g, l0_ln1_b)
    qkv = _project(h, l0_mix_w_in[:, :A_QKV_WIDTH].astype(BF16), F32)
    uv = _project(h, l0_mix_w_in[:, A_QKV_WIDTH:].astype(BF16), F32)
    a_out = _dilated_attention(qkv, batch, seq)
    b_out = _gmlp(uv, l0_gmlp_ln_g, l0_gmlp_ln_b, l0_gmlp_w_s, l0_gmlp_b_s)
    a_width = a_out.shape[1]
    w_mix_out = [l0_mix_w_out[:a_width].astype(BF16), l0_mix_w_out[a_width:].astype(BF16)]
    h = _outproj_deepnorm([a_out, b_out], w_mix_out, h, l0_ln2_g, l0_ln2_b)
    h = _mem_block(h, mem2, l0_mem_w_q, l0_mem_w_kv, l0_mem_w_o, l0_ln3_g, l0_ln3_b, batch, seq, mem_len)
    h = _ffn_deepnorm(h, *_prep_ffn(l0_ffn2_w_in, l0_ffn2_w_out), l0_ln4_g, l0_ln4_b)

    h = _ffn_deepnorm(h, *_prep_ffn(l1_ffn1_w_in, l1_ffn1_w_out), l1_ln1_g, l1_ln1_b)
    qkv = _project(h, l1_mix_w_in.astype(BF16), BF16)
    o = _moba(qkv, batch, seq)
    h = _outproj_deepnorm([o], [l1_mix_w_out.astype(BF16)], h, l1_ln2_g, l1_ln2_b)
    h = _mem_block(h, mem2, l1_mem_w_q, l1_mem_w_kv, l1_mem_w_o, l1_ln3_g, l1_ln3_b, batch, seq, mem_len)
    h = _ffn_deepnorm(h, *_prep_ffn(l1_ffn2_w_in, l1_ffn2_w_out), l1_ln4_g, l1_ln4_b)
    return h.reshape(batch, seq, d)
```

```python
import functools

import jax
import jax.numpy as jnp
from jax import lax
from jax.experimental import pallas as pl
from jax.experimental.pallas import tpu as pltpu

D_MODEL = 2048
DEPTH = 2
HEAD_DIM = 64
HEAD_PAIR = 2 * HEAD_DIM
DIL_GROUPS = ((128, 1), (512, 4), (2048, 16))
A_HEADS_PER_GROUP = 8
A_HEADS = A_HEADS_PER_GROUP * len(DIL_GROUPS)
BAND_BLOCK = 128
CHUNK = 128
B_GROUPS = 8
B_WIDTH = 1024
B_GROUP_DIM = B_WIDTH // B_GROUPS
C_HEADS = D_MODEL // HEAD_DIM
MOBA_BLOCK = 256
MOBA_TOPK = 3
X_HEADS = 4
X_HEAD_DIM = D_MODEL // X_HEADS
D_FF = 5504
DEEPNORM_ALPHA = (2 * DEPTH) ** 0.25
LN_EPS = 1e-5
A_QKV_WIDTH = 3 * A_HEADS * HEAD_DIM

LANES = 128
FF_TILE = 512
ROW_TILE = 512
VMEM_LIMIT = 56 * 1024 * 1024
NEG_BIG = -1e30
M_INIT = -1e29

BF16 = jnp.bfloat16
F32 = jnp.float32


def _params(*sem):
    return pltpu.CompilerParams(dimension_semantics=sem, vmem_limit_bytes=VMEM_LIMIT)


def _dot(a, b):
    return jnp.dot(a, b, preferred_element_type=F32)


def _dot_nt(a, b):
    return lax.dot_general(a, b, (((1,), (1,)), ((), ())), preferred_element_type=F32)


def _layer_norm(y, g, b):
    mu = jnp.mean(y, axis=-1, keepdims=True)
    yc = y - mu
    var = jnp.mean(yc * yc, axis=-1, keepdims=True)
    return yc * lax.rsqrt(var + LN_EPS) * g + b


FF_STEPS = -(-D_FF // FF_TILE)
FF_OVERLAP = FF_STEPS * FF_TILE - D_FF


def _ff_tile_start(f, base=0):
    assert base % LANES == 0 and D_FF % LANES == 0 and FF_TILE % LANES == 0
    start = jnp.where(f == FF_STEPS - 1, (D_FF - FF_TILE) // LANES, f * (FF_TILE // LANES))
    return (start + base // LANES) * LANES


def _ffn_kernel(x_ref, wg_ref, wu_ref, wo_ref, g_ref, b_ref, o_ref, xb_ref, acc_ref):
    f = pl.program_id(1)
    last = f == pl.num_programs(1) - 1

    @pl.when(f == 0)
    def _():
        xb_ref[...] = x_ref[...].astype(BF16)
        acc_ref[...] = jnp.zeros_like(acc_ref)

    xb = xb_ref[...]
    gate = _dot(xb, wg_ref[...])
    up = _dot(xb, wu_ref[...])
    act = (gate * jax.nn.sigmoid(gate)) * up
    col = lax.broadcasted_iota(jnp.int32, act.shape, 1)
    act = jnp.where(col >= jnp.where(last, FF_OVERLAP, 0), act, 0.0)
    acc_ref[...] += _dot(act.astype(BF16), wo_ref[...])

    @pl.when(last)
    def _():
        y = DEEPNORM_ALPHA * x_ref[...] + 0.5 * acc_ref[...]
        o_ref[...] = _layer_norm(y, g_ref[...], b_ref[...])


def _ffn_deepnorm(x, w_in, w_out, g, b):
    n, d = x.shape
    return pl.pallas_call(
        _ffn_kernel,
        grid=(n // ROW_TILE, FF_STEPS),
        in_specs=[
            pl.BlockSpec((ROW_TILE, d), lambda i, f: (i, 0)),
            pl.BlockSpec((pl.Element(d), pl.Element(FF_TILE)), lambda i, f: (0, _ff_tile_start(f))),
            pl.BlockSpec((pl.Element(d), pl.Element(FF_TILE)), lambda i, f: (0, _ff_tile_start(f, D_FF))),
            pl.BlockSpec((pl.Element(FF_TILE), pl.Element(d)), lambda i, f: (_ff_tile_start(f), 0)),
            pl.BlockSpec((1, d), lambda i, f: (0, 0)),
            pl.BlockSpec((1, d), lambda i, f: (0, 0)),
        ],
        out_specs=pl.BlockSpec((ROW_TILE, d), lambda i, f: (i, 0)),
        out_shape=jax.ShapeDtypeStruct((n, d), F32),
        scratch_shapes=[pltpu.VMEM((ROW_TILE, d), BF16), pltpu.VMEM((ROW_TILE, d), F32)],
        compiler_params=_params("parallel", "arbitrary"),
        name="ffn_deepnorm",
    )(x, w_in, w_in, w_out, g.reshape(1, d), b.reshape(1, d))


def _proj_kernel(x_ref, w_ref, o_ref):
    o_ref[...] = _dot(x_ref[...].astype(BF16), w_ref[...]).astype(o_ref.dtype)


PROJ_MAX_COLS = 2048


def _project(x, w, out_dtype, row_tile=1024):
    n, k = x.shape
    m = w.shape[1]
    row_tile = min(row_tile, n)
    col_tile = max(c for c in range(LANES, PROJ_MAX_COLS + 1, LANES) if m % c == 0)
    return pl.pallas_call(
        _proj_kernel,
        grid=(n // row_tile, m // col_tile),
        in_specs=[
            pl.BlockSpec((row_tile, k), lambda i, j: (i, 0)),
            pl.BlockSpec((k, col_tile), lambda i, j: (0, j)),
        ],
        out_specs=pl.BlockSpec((row_tile, col_tile), lambda i, j: (i, j)),
        out_shape=jax.ShapeDtypeStruct((n, m), out_dtype),
        compiler_params=_params("parallel", "arbitrary"),
        name="project",
    )(x, w)


OUTPROJ_CHUNK = 256


def _outproj_kernel(*refs, n_in):
    a_refs, w_refs = refs[:n_in], refs[n_in:2 * n_in]
    x_ref, g_ref, b_ref, o_ref = refs[2 * n_in:]
    for c in range(ROW_TILE // OUTPROJ_CHUNK):
        rows = slice(c * OUTPROJ_CHUNK, (c + 1) * OUTPROJ_CHUNK)
        fx = _dot(a_refs[0][rows, :], w_refs[0][...])
        for a_ref, w_ref in zip(a_refs[1:], w_refs[1:]):
            fx = fx + _dot(a_ref[rows, :], w_ref[...])
        y = DEEPNORM_ALPHA * x_ref[rows, :] + fx
        o_ref[rows, :] = _layer_norm(y, g_ref[...], b_ref[...])


def _outproj_deepnorm(a_list, w_list, x, g, b):
    n, d = x.shape
    n_in = len(a_list)
    in_specs = [pl.BlockSpec((ROW_TILE, a.shape[1]), lambda i: (i, 0)) for a in a_list]
    in_specs += [pl.BlockSpec(w.shape, lambda i: (0, 0)) for w in w_list]
    in_specs += [
        pl.BlockSpec((ROW_TILE, d), lambda i: (i, 0)),
        pl.BlockSpec((1, d), lambda i: (0, 0)),
        pl.BlockSpec((1, d), lambda i: (0, 0)),
    ]
    return pl.pallas_call(
        functools.partial(_outproj_kernel, n_in=n_in),
        grid=(n // ROW_TILE,),
        in_specs=in_specs,
        out_specs=pl.BlockSpec((ROW_TILE, d), lambda i: (i, 0)),
        out_shape=jax.ShapeDtypeStruct((n, d), F32),
        compiler_params=_params("parallel"),
        name="outproj_deepnorm",
    )(*a_list, *w_list, x, g.reshape(1, d), b.reshape(1, d))


def _dilated_kernel(*refs, seq):
    n_g = len(DIL_GROUPS)
    qkv_refs = refs[:3 * n_g]
    o_ref = refs[3 * n_g]
    o_sc, lse_sc, bias_sc = refs[3 * n_g + 1:]

    bb = BAND_BLOCK
    qi = lax.broadcasted_iota(jnp.int32, (bb, 2 * bb), 0)
    ki = lax.broadcasted_iota(jnp.int32, (bb, 2 * bb), 1)
    for first, delta in ((0, bb), (1, 0)):
        off = qi + delta - ki
        bias_sc[first] = jnp.where((off >= 0) & (off <= bb), 0.0, -jnp.inf).astype(F32)

    lane = lax.broadcasted_iota(jnp.int32, (bb, HEAD_PAIR), 1)
    head0 = lane < HEAD_DIM
    scale = HEAD_DIM ** -0.5

    for g, (window, dil) in enumerate(DIL_GROUPS):
        assert window // dil == bb
        q_ref, k_ref, v_ref = qkv_refs[3 * g:3 * g + 3]
        n_blk = seq // (dil * bb)

        def body(it, carry, q_ref=q_ref, k_ref=k_ref, v_ref=v_ref, dil=dil, g=g):
            blk = it // dil
            phase = it - blk * dil
            is_first = jnp.where(blk == 0, 1, 0)
            q_start = blk * (bb * dil) + phase
            k_start = jnp.maximum(blk - 1, 0) * (bb * dil) + phase
            if dil == 1:
                q_rows, k_rows = pl.ds(q_start, bb), pl.ds(k_start, 2 * bb)
            else:
                q_rows = pl.ds(q_start, bb, stride=dil)
                k_rows = pl.ds(k_start, 2 * bb, stride=dil)
            q = q_ref[q_rows, :] * scale
            k = k_ref[k_rows, :].astype(BF16)
            v = v_ref[k_rows, :].astype(BF16)
            bias = bias_sc[is_first]
            outs, lses = [], []
            for h0 in (True, False):
                qh = jnp.where(head0 if h0 else ~head0, q, 0.0).astype(BF16)
                s = _dot_nt(qh, k) + bias
                m = jnp.max(s, axis=-1, keepdims=True)
                p = jnp.exp(s - m)
                l = jnp.sum(p, axis=-1, keepdims=True)
                outs.append(_dot(p.astype(BF16), v) / l)
                lses.append(m + jnp.log(l))
            o_sc[g, q_rows, :] = jnp.where(head0, outs[0], outs[1])
            lse_sc[g, q_rows, :] = jnp.where(head0, lses[0], lses[1])
            return carry

        lax.fori_loop(0, n_blk * dil, body, 0, unroll=8)

    lse = [lse_sc[g] for g in range(n_g)]
    top = functools.reduce(jnp.maximum, lse)
    w = [jnp.exp(x - top) for x in lse]
    den = functools.reduce(jnp.add, w)
    acc = functools.reduce(jnp.add, [w[g] * o_sc[g] for g in range(n_g)])
    o_ref[...] = (acc / den).astype(o_ref.dtype)


def _dilated_attention(qkv, batch, seq):
    n_g = len(DIL_GROUPS)
    pairs = A_HEADS_PER_GROUP // 2
    blocks_per_part = A_HEADS * HEAD_DIM // HEAD_PAIR
    in_specs = []
    for g in range(n_g):
        for part in range(3):
            base = part * blocks_per_part + g * pairs
            in_specs.append(pl.BlockSpec((seq, HEAD_PAIR), lambda b, j, base=base: (b, base + j)))
    return pl.pallas_call(
        functools.partial(_dilated_kernel, seq=seq),
        grid=(batch, pairs),
        in_specs=in_specs,
        out_specs=pl.BlockSpec((seq, HEAD_PAIR), lambda b, j: (b, j)),
        out_shape=jax.ShapeDtypeStruct((batch * seq, pairs * HEAD_PAIR), BF16),
        scratch_shapes=[
            pltpu.VMEM((n_g, seq, HEAD_PAIR), F32),
            pltpu.VMEM((n_g, seq, HEAD_PAIR), F32),
            pltpu.VMEM((2, BAND_BLOCK, 2 * BAND_BLOCK), F32),
        ],
        compiler_params=_params("parallel", "parallel"),
        name="dilated_attention",
    )(*([qkv] * (3 * n_g)))


def _gmlp_kernel(u_ref, v_ref, g_ref, b_ref, ws_ref, bias_ref, o_ref, *, chunks):
    u = jax.nn.gelu(u_ref[...])
    v = _layer_norm(jax.nn.gelu(v_ref[...]), g_ref[...], b_ref[...]).astype(BF16)
    row = lax.broadcasted_iota(jnp.int32, (CHUNK, CHUNK), 0)
    col = lax.broadcasted_iota(jnp.int32, (CHUNK, CHUNK), 1)
    tril = row >= col
    bias = bias_ref[...]
    for grp in range(B_GROUPS):
        w = jnp.where(tril, ws_ref[grp], 0.0).astype(BF16)
        cols = slice(grp * B_GROUP_DIM, (grp + 1) * B_GROUP_DIM)
        for c in range(chunks):
            rows = slice(c * CHUNK, (c + 1) * CHUNK)
            mixed = _dot(w, v[rows, cols]) + bias[:, cols]
            o_ref[rows, cols] = (u[rows, cols] * mixed).astype(o_ref.dtype)


def _gmlp(uv, ln_g, ln_b, w_s, b_s, chunks=4):
    n = uv.shape[0]
    rows = chunks * CHUNK
    bias_full = jnp.repeat(b_s.T, B_GROUP_DIM, axis=1)
    return pl.pallas_call(
        functools.partial(_gmlp_kernel, chunks=chunks),
        grid=(n // rows,),
        in_specs=[
            pl.BlockSpec((rows, B_WIDTH), lambda i: (i, 0)),
            pl.BlockSpec((rows, B_WIDTH), lambda i: (i, 1)),
            pl.BlockSpec((1, B_WIDTH), lambda i: (0, 0)),
            pl.BlockSpec((1, B_WIDTH), lambda i: (0, 0)),
            pl.BlockSpec((B_GROUPS, CHUNK, CHUNK), lambda i: (0, 0, 0)),
            pl.BlockSpec((CHUNK, B_WIDTH), lambda i: (0, 0)),
        ],
        out_specs=pl.BlockSpec((rows, B_WIDTH), lambda i: (i, 0)),
        out_shape=jax.ShapeDtypeStruct((n, B_WIDTH), BF16),
        compiler_params=_params("parallel"),
        name="gmlp",
    )(uv, uv, ln_g.reshape(1, B_WIDTH), ln_b.reshape(1, B_WIDTH), w_s, bias_full)


VT_ROWS = HEAD_DIM + 16


def _moba_seq_kernel(q_ref, k_ref, v_ref, o_ref, kmean_sc, qt_sc, vt_sc, selb_sc, st_sc, *, n_blocks):
    blk = MOBA_BLOCK
    n_rows = kmean_sc.shape[0]
    scale = HEAD_DIM ** -0.5

    kmean_sc[...] = jnp.zeros_like(kmean_sc)
    ones_rows = (lax.broadcasted_iota(jnp.int32, (VT_ROWS - HEAD_DIM, blk), 0) == 0).astype(BF16)

    def prep(m, carry):
        rows = pl.ds(pl.multiple_of(m * blk, blk), blk)
        kmean_sc[pl.ds(m, 1), :] = jnp.mean(k_ref[rows, :].astype(F32), axis=0, keepdims=True)
        qt_sc[m] = (jnp.transpose(q_ref[rows, :].astype(F32)) * scale).astype(BF16)
        vt = jnp.transpose(v_ref[rows, :].astype(F32)).astype(BF16)
        for h in range(2):
            vt_sc[m, h] = jnp.concatenate([vt[h * HEAD_DIM:(h + 1) * HEAD_DIM], ones_rows], axis=0)
        return carry

    lax.fori_loop(0, n_blocks, prep, 0, unroll=2)

    chan = lax.broadcasted_iota(jnp.int32, (HEAD_PAIR, blk), 0)
    head_rows = [chan < HEAD_DIM, chan >= HEAD_DIM]
    kmean = kmean_sc[...].astype(BF16)
    cand = lax.broadcasted_iota(jnp.int32, (n_rows, blk), 0)

    def select(n, carry):
        qt = qt_sc[n]
        for h in range(2):
            gate = _dot(kmean, jnp.where(head_rows[h], qt, 0))
            rank = jnp.zeros((n_rows, blk), jnp.int32)
            for m2 in range(n_blocks):
                g2 = gate[m2:m2 + 1, :]
                ahead = (g2 > gate) | ((g2 == gate) & (m2 < cand))
                rank = rank + jnp.where(ahead & (m2 < n), 1, 0)
            chosen = ((cand < n) & (rank < MOBA_TOPK) & (jnp.abs(gate) < jnp.inf)) | (cand == n)
            selb_sc[n, h] = jnp.where(chosen, 0.0, NEG_BIG).astype(BF16)
        return carry

    lax.fori_loop(0, n_blocks, select, 0, unroll=2)

    key_i = lax.broadcasted_iota(jnp.int32, (blk, blk), 0)
    qry_i = lax.broadcasted_iota(jnp.int32, (blk, blk), 1)
    causal = key_i <= qry_i
    onehot_lane = lax.broadcasted_iota(jnp.int32, (blk, LANES), 1)
    pad_rows = jnp.zeros((LANES - n_rows, blk), BF16)

    def softmax_stage(m_run, st):
        m_new = jnp.maximum(m_run, jnp.max(st, axis=0, keepdims=True))
        return m_new, jnp.exp(m_run - m_new), jnp.exp(st - m_new).astype(BF16)

    def query_block(n, carry):
        qt = qt_sc[n]
        w_aug = [jnp.concatenate([jnp.where(head_rows[h], qt, 0), selb_sc[n, h], pad_rows], axis=0)
                 for h in range(2)]

        def scores(m):
            k_m = k_ref[pl.ds(pl.multiple_of(m * blk, blk), blk), :]
            k_aug = jnp.concatenate([k_m, (onehot_lane == m).astype(BF16)], axis=1)
            return [_dot(k_aug, w_aug[h]) for h in range(2)]

        def body(i, state):
            m_run, acc = state
            st_cur = [st_sc[i % 2, h] for h in range(2)]
            st_next = scores(i + 1)
            out_m, out_acc = [], []
            for h in range(2):
                st_sc[(i + 1) % 2, h] = st_next[h]
                m_new, alpha, p = softmax_stage(m_run[h], st_cur[h])
                out_m.append(m_new)
                out_acc.append(alpha * acc[h] + _dot(vt_sc[i, h], p))
            return out_m, out_acc

        for h, st in enumerate(scores(0)):
            st_sc[0, h] = st
        init = ([jnp.full((1, blk), M_INIT, F32)] * 2, [jnp.zeros((VT_ROWS, blk), F32)] * 2)
        m_run, acc = lax.fori_loop(0, n, body, init)
        outs = []
        for h in range(2):
            _, alpha, p = softmax_stage(m_run[h], jnp.where(causal, st_sc[n % 2, h], NEG_BIG))
            acc_h = alpha * acc[h] + _dot(vt_sc[n, h], p)
            outs.append(acc_h[:HEAD_DIM] / acc_h[HEAD_DIM:HEAD_DIM + 1])
        rows = pl.ds(pl.multiple_of(n * blk, blk), blk)
        o_ref[rows, :] = jnp.transpose(jnp.concatenate(outs, axis=0)).astype(o_ref.dtype)
        return carry

    lax.fori_loop(0, n_blocks, query_block, 0)


def _moba(qkv, batch, seq):
    assert seq % MOBA_BLOCK == 0
    n_blocks = seq // MOBA_BLOCK
    n_rows = -(-n_blocks // 8) * 8
    assert n_rows <= LANES
    pairs = C_HEADS // 2
    return pl.pallas_call(
        functools.partial(_moba_seq_kernel, n_blocks=n_blocks),
        grid=(batch, pairs),
        in_specs=[
            pl.BlockSpec((seq, HEAD_PAIR), lambda b, j: (b, j)),
            pl.BlockSpec((seq, HEAD_PAIR), lambda b, j: (b, pairs + j)),
            pl.BlockSpec((seq, HEAD_PAIR), lambda b, j: (b, 2 * pairs + j)),
        ],
        out_specs=pl.BlockSpec((seq, HEAD_PAIR), lambda b, j: (b, j)),
        out_shape=jax.ShapeDtypeStruct((batch * seq, pairs * HEAD_PAIR), BF16),
        scratch_shapes=[
            pltpu.VMEM((n_rows, HEAD_PAIR), F32),
            pltpu.VMEM((n_blocks, HEAD_PAIR, MOBA_BLOCK), BF16),
            pltpu.VMEM((n_blocks, 2, VT_ROWS, MOBA_BLOCK), BF16),
            pltpu.VMEM((n_blocks, 2, n_rows, MOBA_BLOCK), BF16),
            pltpu.VMEM((2, 2, MOBA_BLOCK, MOBA_BLOCK), F32),
        ],
        compiler_params=_params("parallel", "parallel"),
        name="moba",
    )(qkv, qkv, qkv)


def _memattn_kernel(q_ref, k_ref, v_ref, o_ref):
    s = _dot_nt(q_ref[...], k_ref[...]) * (X_HEAD_DIM ** -0.5)
    m = jnp.max(s, axis=-1, keepdims=True)
    p = jnp.exp(s - m)
    l = jnp.sum(p, axis=-1, keepdims=True)
    o_ref[...] = (_dot(p.astype(BF16), v_ref[...]) / l).astype(o_ref.dtype)


def _mem_attention(q, kv, batch, seq, mem_len, q_tile=1024):
    q_tile = min(q_tile, seq)
    tiles = seq // q_tile
    return pl.pallas_call(
        _memattn_kernel,
        grid=(batch, tiles, X_HEADS),
        in_specs=[
            pl.BlockSpec((q_tile, X_HEAD_DIM), lambda b, i, h: (b * tiles + i, h)),
            pl.BlockSpec((mem_len, X_HEAD_DIM), lambda b, i, h: (b, h)),
            pl.BlockSpec((mem_len, X_HEAD_DIM), lambda b, i, h: (b, X_HEADS + h)),
        ],
        out_specs=pl.BlockSpec((q_tile, X_HEAD_DIM), lambda b, i, h: (b * tiles + i, h)),
        out_shape=jax.ShapeDtypeStruct((batch * seq, D_MODEL), BF16),
        compiler_params=_params("parallel", "parallel", "arbitrary"),
        name="mem_attention",
    )(q, kv, kv)


def _prep_ffn(w_in, w_out):
    return w_in.astype(BF16), w_out.astype(BF16)


def _mem_block(x, mem2, w_q, w_kv, w_o, g, b, batch, seq, mem_len):
    q = _project(x, w_q.astype(BF16), BF16)
    kv = _project(mem2, w_kv.astype(BF16), BF16)
    o = _mem_attention(q, kv, batch, seq, mem_len)
    return _outproj_deepnorm([o], [w_o.astype(BF16)], x, g, b)


def kernel(x, mem, l0_ffn1_w_in, l0_ffn1_w_out, l0_ln1_g, l0_ln1_b, l0_mix_w_in, l0_gmlp_ln_g, l0_gmlp_ln_b, l0_gmlp_w_s, l0_gmlp_b_s, l0_mix_w_out, l0_ln2_g, l0_ln2_b, l0_mem_w_q, l0_mem_w_kv, l0_mem_w_o, l0_ln3_g, l0_ln3_b, l0_ffn2_w_in, l0_ffn2_w_out, l0_ln4_g, l0_ln4_b, l1_ffn1_w_in, l1_ffn1_w_out, l1_ln1_g, l1_ln1_b, l1_mix_w_in, l1_mix_w_out, l1_ln2_g, l1_ln2_b, l1_mem_w_q, l1_mem_w_kv, l1_mem_w_o, l1_ln3_g, l1_ln3_b, l1_ffn2_w_in, l1_ffn2_w_out, l1_ln4_g, l1_ln4_b):
    batch, seq, d = x.shape
    mem_len = mem.shape[1]
    h = x.reshape(batch * seq, d)
    mem2 = mem.reshape(batch * mem_len, d)

    h = _ffn_deepnorm(h, *_prep_ffn(l0_ffn1_w_in, l0_ffn1_w_out), l0_ln1_g, l0_ln1_b)
    qkv = _project(h, l0_mix_w_in[:, :A_QKV_WIDTH].astype(BF16), F32)
    uv = _project(h, l0_mix_w_in[:, A_QKV_WIDTH:].astype(BF16), F32)
    a_out = _dilated_attention(qkv, batch, seq)
    b_out = _gmlp(uv, l0_gmlp_ln_g, l0_gmlp_ln_b, l0_gmlp_w_s, l0_gmlp_b_s)
    a_width = a_out.shape[1]
    w_mix_out = [l0_mix_w_out[:a_width].astype(BF16), l0_mix_w_out[a_width:].astype(BF16)]
    h = _outproj_deepnorm([a_out, b_out], w_mix_out, h, l0_ln2_g, l0_ln2_b)
    h = _mem_block(h, mem2, l0_mem_w_q, l0_mem_w_kv, l0_mem_w_o, l0_ln3_g, l0_ln3_b, batch, seq, mem_len)
    h = _ffn_deepnorm(h, *_prep_ffn(l0_ffn2_w_in, l0_ffn2_w_out), l0_ln4_g, l0_ln4_b)

    h = _ffn_deepnorm(h, *_prep_ffn(l1_ffn1_w_in, l1_ffn1_w_out), l1_ln1_g, l1_ln1_b)
    qkv = _project(h, l1_mix_w_in.astype(BF16), BF16)
    o = _moba(qkv, batch, seq)
    h = _outproj_deepnorm([o], [l1_mix_w_out.astype(BF16)], h, l1_ln2_g, l1_ln2_b)
    h = _mem_block(h, mem2, l1_mem_w_q, l1_mem_w_kv, l1_mem_w_o, l1_ln3_g, l1_ln3_b, batch, seq, mem_len)
    h = _ffn_deepnorm(h, *_prep_ffn(l1_ffn2_w_in, l1_ffn2_w_out), l1_ln4_g, l1_ln4_b)
    return h.reshape(batch, seq, d)
```

```python
import functools

import jax
import jax.numpy as jnp
from jax import lax
from jax.experimental import pallas as pl
from jax.experimental.pallas import tpu as pltpu

D_MODEL = 2048
DEPTH = 2
HEAD_DIM = 64
HEAD_PAIR = 2 * HEAD_DIM
DIL_GROUPS = ((128, 1), (512, 4), (2048, 16))
A_HEADS_PER_GROUP = 8
A_HEADS = A_HEADS_PER_GROUP * len(DIL_GROUPS)
BAND_BLOCK = 128
CHUNK = 128
B_GROUPS = 8
B_WIDTH = 1024
B_GROUP_DIM = B_WIDTH // B_GROUPS
C_HEADS = D_MODEL // HEAD_DIM
MOBA_BLOCK = 256
MOBA_TOPK = 3
X_HEADS = 4
X_HEAD_DIM = D_MODEL // X_HEADS
D_FF = 5504
DEEPNORM_ALPHA = (2 * DEPTH) ** 0.25
LN_EPS = 1e-5
A_QKV_WIDTH = 3 * A_HEADS * HEAD_DIM

LANES = 128
FF_TILE = 512
ROW_TILE = 512
VMEM_LIMIT = 56 * 1024 * 1024
NEG_BIG = -1e30
M_INIT = -1e29

BF16 = jnp.bfloat16
F32 = jnp.float32


def _params(*sem):
    return pltpu.CompilerParams(dimension_semantics=sem, vmem_limit_bytes=VMEM_LIMIT)


def _dot(a, b):
    return jnp.dot(a, b, preferred_element_type=F32)


def _dot_nt(a, b):
    return lax.dot_general(a, b, (((1,), (1,)), ((), ())), preferred_element_type=F32)


def _layer_norm(y, g, b):
    mu = jnp.mean(y, axis=-1, keepdims=True)
    yc = y - mu
    var = jnp.mean(yc * yc, axis=-1, keepdims=True)
    return yc * lax.rsqrt(var + LN_EPS) * g + b


FF_FULL = D_FF // FF_TILE
FF_TAIL = D_FF - FF_FULL * FF_TILE
assert FF_TAIL % LANES == 0 and FF_TAIL > 0


def _ff_tile_start(f, base=0):
    assert base % LANES == 0 and FF_TILE % LANES == 0
    return (jnp.minimum(f, FF_FULL - 1) * (FF_TILE // LANES) + base // LANES) * LANES


def _ffn_kernel(x_ref, wg_ref, wu_ref, wo_ref, wg_tail, wu_tail, wo_tail, g_ref, b_ref, o_ref, xb_ref, acc_ref):
    f = pl.program_id(1)
    last = f == pl.num_programs(1) - 1

    def contribution(wg, wu, wo):
        xb = xb_ref[...]
        gate = _dot(xb, wg[...])
        up = _dot(xb, wu[...])
        act = (gate * jax.nn.sigmoid(gate)) * up
        return _dot(act.astype(BF16), wo[...])

    @pl.when(f == 0)
    def _():
        xb_ref[...] = x_ref[...].astype(BF16)
        acc_ref[...] = jnp.zeros_like(acc_ref)

    @pl.when(jnp.logical_not(last))
    def _():
        acc_ref[...] += contribution(wg_ref, wu_ref, wo_ref)

    @pl.when(last)
    def _():
        ffn = acc_ref[...] + contribution(wg_tail, wu_tail, wo_tail)
        y = DEEPNORM_ALPHA * x_ref[...] + 0.5 * ffn
        o_ref[...] = _layer_norm(y, g_ref[...], b_ref[...])


def _ffn_deepnorm(x, w_in, w_out, g, b):
    n, d = x.shape
    tail_start = FF_FULL * FF_TILE
    full = lambda rows, cols: (pl.Element(rows), pl.Element(cols))
    return pl.pallas_call(
        _ffn_kernel,
        grid=(n // ROW_TILE, FF_FULL + 1),
        in_specs=[
            pl.BlockSpec((ROW_TILE, d), lambda i, f: (i, 0)),
            pl.BlockSpec(full(d, FF_TILE), lambda i, f: (0, _ff_tile_start(f))),
            pl.BlockSpec(full(d, FF_TILE), lambda i, f: (0, _ff_tile_start(f, D_FF))),
            pl.BlockSpec(full(FF_TILE, d), lambda i, f: (_ff_tile_start(f), 0)),
            pl.BlockSpec(full(d, FF_TAIL), lambda i, f: (0, tail_start)),
            pl.BlockSpec(full(d, FF_TAIL), lambda i, f: (0, D_FF + tail_start)),
            pl.BlockSpec(full(FF_TAIL, d), lambda i, f: (tail_start, 0)),
            pl.BlockSpec((1, d), lambda i, f: (0, 0)),
            pl.BlockSpec((1, d), lambda i, f: (0, 0)),
        ],
        out_specs=pl.BlockSpec((ROW_TILE, d), lambda i, f: (i, 0)),
        out_shape=jax.ShapeDtypeStruct((n, d), F32),
        scratch_shapes=[pltpu.VMEM((ROW_TILE, d), BF16), pltpu.VMEM((ROW_TILE, d), F32)],
        compiler_params=_params("parallel", "arbitrary"),
        name="ffn_deepnorm",
    )(x, w_in, w_in, w_out, w_in, w_in, w_out, g.reshape(1, d), b.reshape(1, d))


def _proj_kernel(x_ref, w_ref, o_ref):
    o_ref[...] = _dot(x_ref[...].astype(BF16), w_ref[...]).astype(o_ref.dtype)


PROJ_MAX_COLS = 2048


def _project(x, w, out_dtype, row_tile=1024):
    n, k = x.shape
    m = w.shape[1]
    row_tile = min(row_tile, n)
    col_tile = max(c for c in range(LANES, PROJ_MAX_COLS + 1, LANES) if m % c == 0)
    return pl.pallas_call(
        _proj_kernel,
        grid=(n // row_tile, m // col_tile),
        in_specs=[
            pl.BlockSpec((row_tile, k), lambda i, j: (i, 0)),
            pl.BlockSpec((k, col_tile), lambda i, j: (0, j)),
        ],
        out_specs=pl.BlockSpec((row_tile, col_tile), lambda i, j: (i, j)),
        out_shape=jax.ShapeDtypeStruct((n, m), out_dtype),
        compiler_params=_params("parallel", "arbitrary"),
        name="project",
    )(x, w)


OUTPROJ_CHUNK = 256


def _outproj_kernel(*refs, n_in):
    a_refs, w_refs = refs[:n_in], refs[n_in:2 * n_in]
    x_ref, g_ref, b_ref, o_ref = refs[2 * n_in:]
    for c in range(ROW_TILE // OUTPROJ_CHUNK):
        rows = slice(c * OUTPROJ_CHUNK, (c + 1) * OUTPROJ_CHUNK)
        fx = _dot(a_refs[0][rows, :], w_refs[0][...])
        for a_ref, w_ref in zip(a_refs[1:], w_refs[1:]):
            fx = fx + _dot(a_ref[rows, :], w_ref[...])
        y = DEEPNORM_ALPHA * x_ref[rows, :] + fx
        o_ref[rows, :] = _layer_norm(y, g_ref[...], b_ref[...])


def _outproj_deepnorm(a_list, w_list, x, g, b):
    n, d = x.shape
    n_in = len(a_list)
    in_specs = [pl.BlockSpec((ROW_TILE, a.shape[1]), lambda i: (i, 0)) for a in a_list]
    in_specs += [pl.BlockSpec(w.shape, lambda i: (0, 0)) for w in w_list]
    in_specs += [
        pl.BlockSpec((ROW_TILE, d), lambda i: (i, 0)),
        pl.BlockSpec((1, d), lambda i: (0, 0)),
        pl.BlockSpec((1, d), lambda i: (0, 0)),
    ]
    return pl.pallas_call(
        functools.partial(_outproj_kernel, n_in=n_in),
        grid=(n // ROW_TILE,),
        in_specs=in_specs,
        out_specs=pl.BlockSpec((ROW_TILE, d), lambda i: (i, 0)),
        out_shape=jax.ShapeDtypeStruct((n, d), F32),
        compiler_params=_params("parallel"),
        name="outproj_deepnorm",
    )(*a_list, *w_list, x, g.reshape(1, d), b.reshape(1, d))


def _dilated_kernel(*refs, seq):
    n_g = len(DIL_GROUPS)
    qkv_refs = refs[:3 * n_g]
    o_ref = refs[3 * n_g]
    o_sc, lse_sc, bias_sc = refs[3 * n_g + 1:]

    bb = BAND_BLOCK
    qi = lax.broadcasted_iota(jnp.int32, (bb, 2 * bb), 0)
    ki = lax.broadcasted_iota(jnp.int32, (bb, 2 * bb), 1)
    for first, delta in ((0, bb), (1, 0)):
        off = qi + delta - ki
        bias_sc[first] = jnp.where((off >= 0) & (off <= bb), 0.0, -jnp.inf).astype(F32)

    lane = lax.broadcasted_iota(jnp.int32, (bb, HEAD_PAIR), 1)
    head0 = lane < HEAD_DIM
    scale = HEAD_DIM ** -0.5

    for g, (window, dil) in enumerate(DIL_GROUPS):
        assert window // dil == bb
        q_ref, k_ref, v_ref = qkv_refs[3 * g:3 * g + 3]
        n_blk = seq // (dil * bb)

        def body(it, carry, q_ref=q_ref, k_ref=k_ref, v_ref=v_ref, dil=dil, g=g):
            blk = it // dil
            phase = it - blk * dil
            is_first = jnp.where(blk == 0, 1, 0)
            q_start = blk * (bb * dil) + phase
            k_start = jnp.maximum(blk - 1, 0) * (bb * dil) + phase
            if dil == 1:
                q_rows, k_rows = pl.ds(q_start, bb), pl.ds(k_start, 2 * bb)
            else:
                q_rows = pl.ds(q_start, bb, stride=dil)
                k_rows = pl.ds(k_start, 2 * bb, stride=dil)
            q = q_ref[q_rows, :] * scale
            k = k_ref[k_rows, :].astype(BF16)
            v = v_ref[k_rows, :].astype(BF16)
            bias = bias_sc[is_first]
            outs, lses = [], []
            for h0 in (True, False):
                qh = jnp.where(head0 if h0 else ~head0, q, 0.0).astype(BF16)
                s = _dot_nt(qh, k) + bias
                m = jnp.max(s, axis=-1, keepdims=True)
                p = jnp.exp(s - m)
                l = jnp.sum(p, axis=-1, keepdims=True)
                outs.append(_dot(p.astype(BF16), v) / l)
                lses.append(m + jnp.log(l))
            o_sc[g, q_rows, :] = jnp.where(head0, outs[0], outs[1])
            lse_sc[g, q_rows, :] = jnp.where(head0, lses[0], lses[1])
            return carry

        lax.fori_loop(0, n_blk * dil, body, 0, unroll=8)

    lse = [lse_sc[g] for g in range(n_g)]
    top = functools.reduce(jnp.maximum, lse)
    w = [jnp.exp(x - top) for x in lse]
    den = functools.reduce(jnp.add, w)
    acc = functools.reduce(jnp.add, [w[g] * o_sc[g] for g in range(n_g)])
    o_ref[...] = (acc / den).astype(o_ref.dtype)


def _dilated_attention(qkv, batch, seq):
    n_g = len(DIL_GROUPS)
    pairs = A_HEADS_PER_GROUP // 2
    blocks_per_part = A_HEADS * HEAD_DIM // HEAD_PAIR
    in_specs = []
    for g in range(n_g):
        for part in range(3):
            base = part * blocks_per_part + g * pairs
            in_specs.append(pl.BlockSpec((seq, HEAD_PAIR), lambda b, j, base=base: (b, base + j)))
    return pl.pallas_call(
        functools.partial(_dilated_kernel, seq=seq),
        grid=(batch, pairs),
        in_specs=in_specs,
        out_specs=pl.BlockSpec((seq, HEAD_PAIR), lambda b, j: (b, j)),
        out_shape=jax.ShapeDtypeStruct((batch * seq, pairs * HEAD_PAIR), BF16),
        scratch_shapes=[
            pltpu.VMEM((n_g, seq, HEAD_PAIR), F32),
            pltpu.VMEM((n_g, seq, HEAD_PAIR), F32),
            pltpu.VMEM((2, BAND_BLOCK, 2 * BAND_BLOCK), F32),
        ],
        compiler_params=_params("parallel", "parallel"),
        name="dilated_attention",
    )(*([qkv] * (3 * n_g)))


def _gmlp_kernel(u_ref, v_ref, g_ref, b_ref, ws_ref, bias_ref, o_ref, *, chunks):
    u = jax.nn.gelu(u_ref[...])
    v = _layer_norm(jax.nn.gelu(v_ref[...]), g_ref[...], b_ref[...]).astype(BF16)
    row = lax.broadcasted_iota(jnp.int32, (CHUNK, CHUNK), 0)
    col = lax.broadcasted_iota(jnp.int32, (CHUNK, CHUNK), 1)
    tril = row >= col
    bias = bias_ref[...]
    for grp in range(B_GROUPS):
        w = jnp.where(tril, ws_ref[grp], 0.0).astype(BF16)
        cols = slice(grp * B_GROUP_DIM, (grp + 1) * B_GROUP_DIM)
        for c in range(chunks):
            rows = slice(c * CHUNK, (c + 1) * CHUNK)
            mixed = _dot(w, v[rows, cols]) + bias[:, cols]
            o_ref[rows, cols] = (u[rows, cols] * mixed).astype(o_ref.dtype)


def _gmlp(uv, ln_g, ln_b, w_s, b_s, chunks=4):
    n = uv.shape[0]
    rows = chunks * CHUNK
    bias_full = jnp.repeat(b_s.T, B_GROUP_DIM, axis=1)
    return pl.pallas_call(
        functools.partial(_gmlp_kernel, chunks=chunks),
        grid=(n // rows,),
        in_specs=[
            pl.BlockSpec((rows, B_WIDTH), lambda i: (i, 0)),
            pl.BlockSpec((rows, B_WIDTH), lambda i: (i, 1)),
            pl.BlockSpec((1, B_WIDTH), lambda i: (0, 0)),
            pl.BlockSpec((1, B_WIDTH), lambda i: (0, 0)),
            pl.BlockSpec((B_GROUPS, CHUNK, CHUNK), lambda i: (0, 0, 0)),
            pl.BlockSpec((CHUNK, B_WIDTH), lambda i: (0, 0)),
        ],
        out_specs=pl.BlockSpec((rows, B_WIDTH), lambda i: (i, 0)),
        out_shape=jax.ShapeDtypeStruct((n, B_WIDTH), BF16),
        compiler_params=_params("parallel"),
        name="gmlp",
    )(uv, uv, ln_g.reshape(1, B_WIDTH), ln_b.reshape(1, B_WIDTH), w_s, bias_full)


VT_ROWS = HEAD_DIM + 16


def _moba_seq_kernel(q_ref, k_ref, v_ref, o_ref, kmean_sc, qt_sc, vt_sc, selb_sc, st_sc, *, n_blocks):
    blk = MOBA_BLOCK
    n_rows = kmean_sc.shape[0]
    scale = HEAD_DIM ** -0.5

    kmean_sc[...] = jnp.zeros_like(kmean_sc)
    ones_rows = (lax.broadcasted_iota(jnp.int32, (VT_ROWS - HEAD_DIM, blk), 0) == 0).astype(BF16)

    def prep(m, carry):
        rows = pl.ds(pl.multiple_of(m * blk, blk), blk)
        kmean_sc[pl.ds(m, 1), :] = jnp.mean(k_ref[rows, :].astype(F32), axis=0, keepdims=True)
        qt_sc[m] = (jnp.transpose(q_ref[rows, :].astype(F32)) * scale).astype(BF16)
        vt = jnp.transpose(v_ref[rows, :].astype(F32)).astype(BF16)
        for h in range(2):
            vt_sc[m, h] = jnp.concatenate([vt[h * HEAD_DIM:(h + 1) * HEAD_DIM], ones_rows], axis=0)
        return carry

    lax.fori_loop(0, n_blocks, prep, 0, unroll=2)

    chan = lax.broadcasted_iota(jnp.int32, (HEAD_PAIR, blk), 0)
    head_rows = [chan < HEAD_DIM, chan >= HEAD_DIM]
    kmean = kmean_sc[...].astype(BF16)
    cand = lax.broadcasted_iota(jnp.int32, (n_rows, blk), 0)

    def select(n, carry):
        qt = qt_sc[n]
        for h in range(2):
            gate = _dot(kmean, jnp.where(head_rows[h], qt, 0))
            left = jnp.where(cand < n, gate, -jnp.inf)
            chosen = cand == n
            for _ in range(MOBA_TOPK):
                best = jnp.max(left, axis=0, keepdims=True)
                first = jnp.min(jnp.where(left == best, cand, n_rows), axis=0, keepdims=True)
                take = (cand == first) & (best > -jnp.inf)
                chosen = chosen | (take & (best < jnp.inf))
                left = jnp.where(take, -jnp.inf, left)
            selb_sc[n, h] = jnp.where(chosen, 0.0, NEG_BIG).astype(BF16)
        return carry

    lax.fori_loop(0, n_blocks, select, 0, unroll=2)

    key_i = lax.broadcasted_iota(jnp.int32, (blk, blk), 0)
    qry_i = lax.broadcasted_iota(jnp.int32, (blk, blk), 1)
    causal = key_i <= qry_i
    key_lane = lax.broadcasted_iota(jnp.int32, (blk, HEAD_PAIR), 1)
    head_lanes = [key_lane < HEAD_DIM, key_lane >= HEAD_DIM]
    bias_base = [HEAD_DIM, 0]
    pad_rows = jnp.zeros((HEAD_DIM - n_rows, blk), BF16)

    def softmax_stage(m_run, st):
        m_new = jnp.maximum(m_run, jnp.max(st, axis=0, keepdims=True))
        return m_new, jnp.exp(m_run - m_new), jnp.exp(st - m_new).astype(BF16)

    def query_block(n, carry):
        qt = qt_sc[n]
        w_aug = [jnp.concatenate([qt[:HEAD_DIM], selb_sc[n, 0], pad_rows], axis=0),
                 jnp.concatenate([selb_sc[n, 1], pad_rows, qt[HEAD_DIM:]], axis=0)]

        def scores(m):
            k_m = k_ref[pl.ds(pl.multiple_of(m * blk, blk), blk), :]
            out = []
            for h in range(2):
                k_aug = jnp.where(head_lanes[h], k_m, (key_lane == bias_base[h] + m).astype(BF16))
                out.append(_dot(k_aug, w_aug[h]))
            return out

        def body(i, state):
            m_run, acc = state
            st_cur = [st_sc[i % 2, h] for h in range(2)]
            st_next = scores(i + 1)
            out_m, out_acc = [], []
            for h in range(2):
                st_sc[(i + 1) % 2, h] = st_next[h]
                m_new, alpha, p = softmax_stage(m_run[h], st_cur[h])
                out_m.append(m_new)
                out_acc.append(alpha * acc[h] + _dot(vt_sc[i, h], p))
            return out_m, out_acc

        for h, st in enumerate(scores(0)):
            st_sc[0, h] = st
        init = ([jnp.full((1, blk), M_INIT, F32)] * 2, [jnp.zeros((VT_ROWS, blk), F32)] * 2)
        m_run, acc = lax.fori_loop(0, n, body, init)
        outs = []
        for h in range(2):
            _, alpha, p = softmax_stage(m_run[h], jnp.where(causal, st_sc[n % 2, h], NEG_BIG))
            acc_h = alpha * acc[h] + _dot(vt_sc[n, h], p)
            outs.append(acc_h[:HEAD_DIM] / acc_h[HEAD_DIM:HEAD_DIM + 1])
        rows = pl.ds(pl.multiple_of(n * blk, blk), blk)
        o_ref[rows, :] = jnp.transpose(jnp.concatenate(outs, axis=0)).astype(o_ref.dtype)
        return carry

    lax.fori_loop(0, n_blocks, query_block, 0)


def _moba(qkv, batch, seq):
    assert seq % MOBA_BLOCK == 0
    n_blocks = seq // MOBA_BLOCK
    n_rows = -(-n_blocks // 16) * 16
    assert n_rows <= HEAD_DIM
    pairs = C_HEADS // 2
    return pl.pallas_call(
        functools.partial(_moba_seq_kernel, n_blocks=n_blocks),
        grid=(batch, pairs),
        in_specs=[
            pl.BlockSpec((seq, HEAD_PAIR), lambda b, j: (b, j)),
            pl.BlockSpec((seq, HEAD_PAIR), lambda b, j: (b, pairs + j)),
            pl.BlockSpec((seq, HEAD_PAIR), lambda b, j: (b, 2 * pairs + j)),
        ],
        out_specs=pl.BlockSpec((seq, HEAD_PAIR), lambda b, j: (b, j)),
        out_shape=jax.ShapeDtypeStruct((batch * seq, pairs * HEAD_PAIR), BF16),
        scratch_shapes=[
            pltpu.VMEM((n_rows, HEAD_PAIR), F32),
            pltpu.VMEM((n_blocks, HEAD_PAIR, MOBA_BLOCK), BF16),
            pltpu.VMEM((n_blocks, 2, VT_ROWS, MOBA_BLOCK), BF16),
            pltpu.VMEM((n_blocks, 2, n_rows, MOBA_BLOCK), BF16),
            pltpu.VMEM((2, 2, MOBA_BLOCK, MOBA_BLOCK), F32),
        ],
        compiler_params=_params("parallel", "parallel"),
        name="moba",
    )(qkv, qkv, qkv)


def _memattn_kernel(q_ref, k_ref, v_ref, o_ref):
    s = _dot_nt(q_ref[...], k_ref[...]) * (X_HEAD_DIM ** -0.5)
    m = jnp.max(s, axis=-1, keepdims=True)
    p = jnp.exp(s - m)
    l = jnp.sum(p, axis=-1, keepdims=True)
    o_ref[...] = (_dot(p.astype(BF16), v_ref[...]) / l).astype(o_ref.dtype)


def _mem_attention(q, kv, batch, seq, mem_len, q_tile=1024):
    q_tile = min(q_tile, seq)
    tiles = seq // q_tile
    return pl.pallas_call(
        _memattn_kernel,
        grid=(batch, tiles, X_HEADS),
        in_specs=[
            pl.BlockSpec((q_tile, X_HEAD_DIM), lambda b, i, h: (b * tiles + i, h)),
            pl.BlockSpec((mem_len, X_HEAD_DIM), lambda b, i, h: (b, h)),
            pl.BlockSpec((mem_len, X_HEAD_DIM), lambda b, i, h: (b, X_HEADS + h)),
        ],
        out_specs=pl.BlockSpec((q_tile, X_HEAD_DIM), lambda b, i, h: (b * tiles + i, h)),
        out_shape=jax.ShapeDtypeStruct((batch * seq, D_MODEL), BF16),
        compiler_params=_params("parallel", "parallel", "arbitrary"),
        name="mem_attention",
    )(q, kv, kv)


def _prep_ffn(w_in, w_out):
    return w_in.astype(BF16), w_out.astype(BF16)


def _mem_block(x, mem2, w_q, w_kv, w_o, g, b, batch, seq, mem_len):
    q = _project(x, w_q.astype(BF16), BF16)
    kv = _project(mem2, w_kv.astype(BF16), BF16)
    o = _mem_attention(q, kv, batch, seq, mem_len)
    return _outproj_deepnorm([o], [w_o.astype(BF16)], x, g, b)


def kernel(x, mem, l0_ffn1_w_in, l0_ffn1_w_out, l0_ln1_g, l0_ln1_b, l0_mix_w_in, l0_gmlp_ln_g, l0_gmlp_ln_b, l0_gmlp_w_s, l0_gmlp_b_s, l0_mix_w_out, l0_ln2_g, l0_ln2_b, l0_mem_w_q, l0_mem_w_kv, l0_mem_w_o, l0_ln3_g, l0_ln3_b, l0_ffn2_w_in, l0_ffn2_w_out, l0_ln4_g, l0_ln4_b, l1_ffn1_w_in, l1_ffn1_w_out, l1_ln1_g, l1_ln1_b, l1_mix_w_in, l1_mix_w_out, l1_ln2_g, l1_ln2_b, l1_mem_w_q, l1_mem_w_kv, l1_mem_w_o, l1_ln3_g, l1_ln3_b, l1_ffn2_w_in, l1_ffn2_w_out, l1_ln4_g, l1_ln4_b):
    batch, seq, d = x.shape
    mem_len = mem.shape[1]
    h = x.reshape(batch * seq, d)
    mem2 = mem.reshape(batch * mem_len, d)

    h = _ffn_deepnorm(h, *_prep_ffn(l0_ffn1_w_in, l0_ffn1_w_out), l0_ln1_g, l0_ln1_b)
    qkv = _project(h, l0_mix_w_in[:, :A_QKV_WIDTH].astype(BF16), F32)
    uv = _project(h, l0_mix_w_in[:, A_QKV_WIDTH:].astype(BF16), F32)
    a_out = _dilated_attention(qkv, batch, seq)
    b_out = _gmlp(uv, l0_gmlp_ln_g, l0_gmlp_ln_b, l0_gmlp_w_s, l0_gmlp_b_s)
    a_width = a_out.shape[1]
    w_mix_out = [l0_mix_w_out[:a_width].astype(BF16), l0_mix_w_out[a_width:].astype(BF16)]
    h = _outproj_deepnorm([a_out, b_out], w_mix_out, h, l0_ln2_g, l0_ln2_b)
    h = _mem_block(h, mem2, l0_mem_w_q, l0_mem_w_kv, l0_mem_w_o, l0_ln3_g, l0_ln3_b, batch, seq, mem_len)
    h = _ffn_deepnorm(h, *_prep_ffn(l0_ffn2_w_in, l0_ffn2_w_out), l0_ln4_g, l0_ln4_b)

    h = _ffn_deepnorm(h, *_prep_ffn(l1_ffn1_w_in, l1_ffn1_w_out), l1_ln1_g, l1_ln1_b)
    qkv = _project(h, l1_mix_w_in.astype(BF16), BF16)
    o = _moba(qkv, batch, seq)
    h = _outproj_deepnorm([o], [l1_mix_w_out.astype(BF16)], h, l1_ln2_g, l1_ln2_b)
    h = _mem_block(h, mem2, l1_mem_w_q, l1_mem_w_kv, l1_mem_w_o, l1_ln3_g, l1_ln3_b, batch, seq, mem_len)
    h = _ffn_deepnorm(h, *_prep_ffn(l1_ffn2_w_in, l1_ffn2_w_out), l1_ln4_g, l1_ln4_b)
    return h.reshape(batch, seq, d)
```

```python
import functools

import jax
import jax.numpy as jnp
from jax import lax
from jax.experimental import pallas as pl
from jax.experimental.pallas import tpu as pltpu

D_MODEL = 2048
DEPTH = 2
HEAD_DIM = 64
HEAD_PAIR = 2 * HEAD_DIM
DIL_GROUPS = ((128, 1), (512, 4), (2048, 16))
A_HEADS_PER_GROUP = 8
A_HEADS = A_HEADS_PER_GROUP * len(DIL_GROUPS)
BAND_BLOCK = 128
CHUNK = 128
B_GROUPS = 8
B_WIDTH = 1024
B_GROUP_DIM = B_WIDTH // B_GROUPS
C_HEADS = D_MODEL // HEAD_DIM
MOBA_BLOCK = 256
MOBA_TOPK = 3
X_HEADS = 4
X_HEAD_DIM = D_MODEL // X_HEADS
D_FF = 5504
DEEPNORM_ALPHA = (2 * DEPTH) ** 0.25
LN_EPS = 1e-5
A_QKV_WIDTH = 3 * A_HEADS * HEAD_DIM

LANES = 128
FF_TILE = 512
ROW_TILE = 512
VMEM_LIMIT = 56 * 1024 * 1024
NEG_BIG = -1e30
M_INIT = -1e29

BF16 = jnp.bfloat16
F32 = jnp.float32


def _params(*sem):
    return pltpu.CompilerParams(dimension_semantics=sem, vmem_limit_bytes=VMEM_LIMIT)


def _dot(a, b):
    return jnp.dot(a, b, preferred_element_type=F32)


def _dot_nt(a, b):
    return lax.dot_general(a, b, (((1,), (1,)), ((), ())), preferred_element_type=F32)


def _layer_norm(y, g, b):
    mu = jnp.mean(y, axis=-1, keepdims=True)
    yc = y - mu
    var = jnp.mean(yc * yc, axis=-1, keepdims=True)
    return yc * lax.rsqrt(var + LN_EPS) * g + b


FF_FULL = D_FF // FF_TILE
FF_TAIL = D_FF - FF_FULL * FF_TILE
assert FF_TAIL % LANES == 0 and FF_TAIL > 0


def _ff_tile_start(f, base=0):
    assert base % LANES == 0 and FF_TILE % LANES == 0
    return (jnp.minimum(f, FF_FULL - 1) * (FF_TILE // LANES) + base // LANES) * LANES


def _ffn_kernel(x_ref, wg_ref, wu_ref, wo_ref, wg_tail, wu_tail, wo_tail, g_ref, b_ref, o_ref, xb_ref, acc_ref):
    f = pl.program_id(1)
    last = f == pl.num_programs(1) - 1

    def contribution(wg, wu, wo):
        xb = xb_ref[...]
        gate = _dot(xb, wg[...])
        up = _dot(xb, wu[...])
        act = (gate * jax.nn.sigmoid(gate)) * up
        return _dot(act.astype(BF16), wo[...])

    @pl.when(f == 0)
    def _():
        xb_ref[...] = x_ref[...].astype(BF16)
        acc_ref[...] = jnp.zeros_like(acc_ref)

    @pl.when(jnp.logical_not(last))
    def _():
        acc_ref[...] += contribution(wg_ref, wu_ref, wo_ref)

    @pl.when(last)
    def _():
        ffn = acc_ref[...] + contribution(wg_tail, wu_tail, wo_tail)
        y = DEEPNORM_ALPHA * x_ref[...] + 0.5 * ffn
        o_ref[...] = _layer_norm(y, g_ref[...], b_ref[...])


def _ffn_deepnorm(x, w_in, w_out, g, b):
    n, d = x.shape
    tail_start = FF_FULL * FF_TILE
    full = lambda rows, cols: (pl.Element(rows), pl.Element(cols))
    return pl.pallas_call(
        _ffn_kernel,
        grid=(n // ROW_TILE, FF_FULL + 1),
        in_specs=[
            pl.BlockSpec((ROW_TILE, d), lambda i, f: (i, 0)),
            pl.BlockSpec(full(d, FF_TILE), lambda i, f: (0, _ff_tile_start(f))),
            pl.BlockSpec(full(d, FF_TILE), lambda i, f: (0, _ff_tile_start(f, D_FF))),
            pl.BlockSpec(full(FF_TILE, d), lambda i, f: (_ff_tile_start(f), 0)),
            pl.BlockSpec(full(d, FF_TAIL), lambda i, f: (0, tail_start)),
            pl.BlockSpec(full(d, FF_TAIL), lambda i, f: (0, D_FF + tail_start)),
            pl.BlockSpec(full(FF_TAIL, d), lambda i, f: (tail_start, 0)),
            pl.BlockSpec((1, d), lambda i, f: (0, 0)),
            pl.BlockSpec((1, d), lambda i, f: (0, 0)),
        ],
        out_specs=pl.BlockSpec((ROW_TILE, d), lambda i, f: (i, 0)),
        out_shape=jax.ShapeDtypeStruct((n, d), F32),
        scratch_shapes=[pltpu.VMEM((ROW_TILE, d), BF16), pltpu.VMEM((ROW_TILE, d), F32)],
        compiler_params=_params("parallel", "arbitrary"),
        name="ffn_deepnorm",
    )(x, w_in, w_in, w_out, w_in, w_in, w_out, g.reshape(1, d), b.reshape(1, d))


def _proj_kernel(x_ref, w_ref, o_ref):
    o_ref[...] = _dot(x_ref[...].astype(BF16), w_ref[...]).astype(o_ref.dtype)


PROJ_MAX_COLS = 2048


def _project(x, w, out_dtype, row_tile=1024):
    n, k = x.shape
    m = w.shape[1]
    row_tile = min(row_tile, n)
    col_tile = max(c for c in range(LANES, PROJ_MAX_COLS + 1, LANES) if m % c == 0)
    return pl.pallas_call(
        _proj_kernel,
        grid=(n // row_tile, m // col_tile),
        in_specs=[
            pl.BlockSpec((row_tile, k), lambda i, j: (i, 0)),
            pl.BlockSpec((k, col_tile), lambda i, j: (0, j)),
        ],
        out_specs=pl.BlockSpec((row_tile, col_tile), lambda i, j: (i, j)),
        out_shape=jax.ShapeDtypeStruct((n, m), out_dtype),
        compiler_params=_params("parallel", "arbitrary"),
        name="project",
    )(x, w)


OUTPROJ_CHUNK = 256


def _outproj_kernel(*refs, n_in):
    a_refs, w_refs = refs[:n_in], refs[n_in:2 * n_in]
    x_ref, g_ref, b_ref, o_ref = refs[2 * n_in:]
    for c in range(ROW_TILE // OUTPROJ_CHUNK):
        rows = slice(c * OUTPROJ_CHUNK, (c + 1) * OUTPROJ_CHUNK)
        fx = _dot(a_refs[0][rows, :], w_refs[0][...])
        for a_ref, w_ref in zip(a_refs[1:], w_refs[1:]):
            fx = fx + _dot(a_ref[rows, :], w_ref[...])
        y = DEEPNORM_ALPHA * x_ref[rows, :] + fx
        o_ref[rows, :] = _layer_norm(y, g_ref[...], b_ref[...])


def _outproj_deepnorm(a_list, w_list, x, g, b):
    n, d = x.shape
    n_in = len(a_list)
    in_specs = [pl.BlockSpec((ROW_TILE, a.shape[1]), lambda i: (i, 0)) for a in a_list]
    in_specs += [pl.BlockSpec(w.shape, lambda i: (0, 0)) for w in w_list]
    in_specs += [
        pl.BlockSpec((ROW_TILE, d), lambda i: (i, 0)),
        pl.BlockSpec((1, d), lambda i: (0, 0)),
        pl.BlockSpec((1, d), lambda i: (0, 0)),
    ]
    return pl.pallas_call(
        functools.partial(_outproj_kernel, n_in=n_in),
        grid=(n // ROW_TILE,),
        in_specs=in_specs,
        out_specs=pl.BlockSpec((ROW_TILE, d), lambda i: (i, 0)),
        out_shape=jax.ShapeDtypeStruct((n, d), F32),
        compiler_params=_params("parallel"),
        name="outproj_deepnorm",
    )(*a_list, *w_list, x, g.reshape(1, d), b.reshape(1, d))


def _dilated_kernel(*refs, seq):
    n_g = len(DIL_GROUPS)
    qkv_refs = refs[:3 * n_g]
    o_ref = refs[3 * n_g]
    o_sc, lse_sc, bias_sc = refs[3 * n_g + 1:]

    bb = BAND_BLOCK
    qi = lax.broadcasted_iota(jnp.int32, (bb, 2 * bb), 0)
    ki = lax.broadcasted_iota(jnp.int32, (bb, 2 * bb), 1)
    for first, delta in ((0, bb), (1, 0)):
        off = qi + delta - ki
        bias_sc[first] = jnp.where((off >= 0) & (off <= bb), 0.0, -jnp.inf).astype(F32)

    lane = lax.broadcasted_iota(jnp.int32, (bb, HEAD_PAIR), 1)
    head0 = lane < HEAD_DIM
    scale = HEAD_DIM ** -0.5

    for g, (window, dil) in enumerate(DIL_GROUPS):
        assert window // dil == bb
        q_ref, k_ref, v_ref = qkv_refs[3 * g:3 * g + 3]
        n_blk = seq // (dil * bb)

        def body(it, carry, q_ref=q_ref, k_ref=k_ref, v_ref=v_ref, dil=dil, g=g):
            blk = it // dil
            phase = it - blk * dil
            is_first = jnp.where(blk == 0, 1, 0)
            q_start = blk * (bb * dil) + phase
            k_start = jnp.maximum(blk - 1, 0) * (bb * dil) + phase
            if dil == 1:
                q_rows, k_rows = pl.ds(q_start, bb), pl.ds(k_start, 2 * bb)
            else:
                q_rows = pl.ds(q_start, bb, stride=dil)
                k_rows = pl.ds(k_start, 2 * bb, stride=dil)
            q = q_ref[q_rows, :] * scale
            k = k_ref[k_rows, :].astype(BF16)
            v = v_ref[k_rows, :].astype(BF16)
            bias = bias_sc[is_first]
            outs, lses = [], []
            for h0 in (True, False):
                qh = jnp.where(head0 if h0 else ~head0, q, 0.0).astype(BF16)
                s = _dot_nt(qh, k) + bias
                m = jnp.max(s, axis=-1, keepdims=True)
                p = jnp.exp(s - m)
                l = jnp.sum(p, axis=-1, keepdims=True)
                outs.append(_dot(p.astype(BF16), v) / l)
                lses.append(m + jnp.log(l))
            o_sc[g, q_rows, :] = jnp.where(head0, outs[0], outs[1])
            lse_sc[g, q_rows, :] = jnp.where(head0, lses[0], lses[1])
            return carry

        lax.fori_loop(0, n_blk * dil, body, 0, unroll=8)

    lse = [lse_sc[g] for g in range(n_g)]
    top = functools.reduce(jnp.maximum, lse)
    w = [jnp.exp(x - top) for x in lse]
    den = functools.reduce(jnp.add, w)
    acc = functools.reduce(jnp.add, [w[g] * o_sc[g] for g in range(n_g)])
    o_ref[...] = (acc / den).astype(o_ref.dtype)


def _dilated_attention(qkv, batch, seq):
    n_g = len(DIL_GROUPS)
    pairs = A_HEADS_PER_GROUP // 2
    blocks_per_part = A_HEADS * HEAD_DIM // HEAD_PAIR
    in_specs = []
    for g in range(n_g):
        for part in range(3):
            base = part * blocks_per_part + g * pairs
            in_specs.append(pl.BlockSpec((seq, HEAD_PAIR), lambda b, j, base=base: (b, base + j)))
    return pl.pallas_call(
        functools.partial(_dilated_kernel, seq=seq),
        grid=(batch, pairs),
        in_specs=in_specs,
        out_specs=pl.BlockSpec((seq, HEAD_PAIR), lambda b, j: (b, j)),
        out_shape=jax.ShapeDtypeStruct((batch * seq, pairs * HEAD_PAIR), BF16),
        scratch_shapes=[
            pltpu.VMEM((n_g, seq, HEAD_PAIR), F32),
            pltpu.VMEM((n_g, seq, HEAD_PAIR), F32),
            pltpu.VMEM((2, BAND_BLOCK, 2 * BAND_BLOCK), F32),
        ],
        compiler_params=_params("parallel", "parallel"),
        name="dilated_attention",
    )(*([qkv] * (3 * n_g)))


def _gmlp_kernel(u_ref, v_ref, g_ref, b_ref, ws_ref, bias_ref, o_ref, *, chunks):
    u = jax.nn.gelu(u_ref[...])
    v = _layer_norm(jax.nn.gelu(v_ref[...]), g_ref[...], b_ref[...]).astype(BF16)
    row = lax.broadcasted_iota(jnp.int32, (CHUNK, CHUNK), 0)
    col = lax.broadcasted_iota(jnp.int32, (CHUNK, CHUNK), 1)
    tril = row >= col
    bias = bias_ref[...]
    for grp in range(B_GROUPS):
        w = jnp.where(tril, ws_ref[grp], 0.0).astype(BF16)
        cols = slice(grp * B_GROUP_DIM, (grp + 1) * B_GROUP_DIM)
        for c in range(chunks):
            rows = slice(c * CHUNK, (c + 1) * CHUNK)
            mixed = _dot(w, v[rows, cols]) + bias[:, cols]
            o_ref[rows, cols] = (u[rows, cols] * mixed).astype(o_ref.dtype)


def _gmlp(uv, ln_g, ln_b, w_s, b_s, chunks=4):
    n = uv.shape[0]
    rows = chunks * CHUNK
    bias_full = jnp.repeat(b_s.T, B_GROUP_DIM, axis=1)
    return pl.pallas_call(
        functools.partial(_gmlp_kernel, chunks=chunks),
        grid=(n // rows,),
        in_specs=[
            pl.BlockSpec((rows, B_WIDTH), lambda i: (i, 0)),
            pl.BlockSpec((rows, B_WIDTH), lambda i: (i, 1)),
            pl.BlockSpec((1, B_WIDTH), lambda i: (0, 0)),
            pl.BlockSpec((1, B_WIDTH), lambda i: (0, 0)),
            pl.BlockSpec((B_GROUPS, CHUNK, CHUNK), lambda i: (0, 0, 0)),
            pl.BlockSpec((CHUNK, B_WIDTH), lambda i: (0, 0)),
        ],
        out_specs=pl.BlockSpec((rows, B_WIDTH), lambda i: (i, 0)),
        out_shape=jax.ShapeDtypeStruct((n, B_WIDTH), BF16),
        compiler_params=_params("parallel"),
        name="gmlp",
    )(uv, uv, ln_g.reshape(1, B_WIDTH), ln_b.reshape(1, B_WIDTH), w_s, bias_full)


VT_ROWS = HEAD_DIM + 16


def _moba_seq_kernel(q_ref, k_ref, v_ref, o_ref, kmean_sc, qt_sc, vt_sc, selb_sc, st_sc, *, n_blocks):
    blk = MOBA_BLOCK
    n_rows = kmean_sc.shape[0]
    scale = HEAD_DIM ** -0.5

    kmean_sc[...] = jnp.zeros_like(kmean_sc)
    ones_rows = (lax.broadcasted_iota(jnp.int32, (VT_ROWS - HEAD_DIM, blk), 0) == 0).astype(BF16)

    def prep(m, carry):
        rows = pl.ds(pl.multiple_of(m * blk, blk), blk)
        kmean_sc[pl.ds(m, 1), :] = jnp.mean(k_ref[rows, :].astype(F32), axis=0, keepdims=True)
        qt_sc[m] = (jnp.transpose(q_ref[rows, :].astype(F32)) * scale).astype(BF16)
        vt = jnp.transpose(v_ref[rows, :].astype(F32)).astype(BF16)
        for h in range(2):
            vt_sc[m, h] = jnp.concatenate([vt[h * HEAD_DIM:(h + 1) * HEAD_DIM], ones_rows], axis=0)
        return carry

    lax.fori_loop(0, n_blocks, prep, 0, unroll=2)

    chan = lax.broadcasted_iota(jnp.int32, (HEAD_PAIR, blk), 0)
    head_rows = [chan < HEAD_DIM, chan >= HEAD_DIM]
    kmean = kmean_sc[...].astype(BF16)
    cand = lax.broadcasted_iota(jnp.int32, (n_rows, blk), 0)

    def select(n, carry):
        qt = qt_sc[n]
        for h in range(2):
            gate = _dot(kmean, jnp.where(head_rows[h], qt, 0))
            left = jnp.where(cand < n, gate, -jnp.inf)
            chosen = cand == n
            for _ in range(MOBA_TOPK):
                best = jnp.max(left, axis=0, keepdims=True)
                first = jnp.min(jnp.where(left == best, cand, n_rows), axis=0, keepdims=True)
                take = (cand == first) & (best > -jnp.inf)
                chosen = chosen | (take & (best < jnp.inf))
                left = jnp.where(take, -jnp.inf, left)
            selb_sc[n, h] = jnp.where(chosen, 0.0, NEG_BIG).astype(BF16)
        return carry

    lax.fori_loop(0, n_blocks, select, 0, unroll=2)

    key_i = lax.broadcasted_iota(jnp.int32, (blk, blk), 0)
    qry_i = lax.broadcasted_iota(jnp.int32, (blk, blk), 1)
    causal = key_i <= qry_i
    key_lane = lax.broadcasted_iota(jnp.int32, (blk, HEAD_PAIR), 1)
    head_lanes = [key_lane < HEAD_DIM, key_lane >= HEAD_DIM]
    bias_base = [HEAD_DIM, 0]
    pad_rows = jnp.zeros((HEAD_DIM - n_rows, blk), BF16)

    def softmax_stage(m_run, st):
        m_new = jnp.maximum(m_run, jnp.max(st, axis=0, keepdims=True))
        return m_new, jnp.exp(m_run - m_new), jnp.exp(st - m_new).astype(BF16)

    def query_block(n, carry):
        qt = qt_sc[n]
        w_aug = [jnp.concatenate([qt[:HEAD_DIM], selb_sc[n, 0], pad_rows], axis=0),
                 jnp.concatenate([selb_sc[n, 1], pad_rows, qt[HEAD_DIM:]], axis=0)]

        def scores(m):
            k_m = k_ref[pl.ds(pl.multiple_of(m * blk, blk), blk), :]
            out = []
            for h in range(2):
                k_aug = jnp.where(head_lanes[h], k_m, (key_lane == bias_base[h] + m).astype(BF16))
                out.append(_dot(k_aug, w_aug[h]))
            return out

        def body(i, state):
            m_run, acc = state
            st_cur = [st_sc[i % 2, h] for h in range(2)]
            st_next = scores(i + 1)
            out_m, out_acc = [], []
            for h in range(2):
                st_sc[(i + 1) % 2, h] = st_next[h]
                m_new, alpha, p = softmax_stage(m_run[h], st_cur[h])
                out_m.append(m_new)
                out_acc.append(alpha * acc[h] + _dot(vt_sc[i, h], p))
            return out_m, out_acc

        for h, st in enumerate(scores(0)):
            st_sc[0, h] = st
        init = ([jnp.full((1, blk), M_INIT, F32)] * 2, [jnp.zeros((VT_ROWS, blk), F32)] * 2)
        m_run, acc = lax.fori_loop(0, n, body, init)
        outs = []
        for h in range(2):
            _, alpha, p = softmax_stage(m_run[h], jnp.where(causal, st_sc[n % 2, h], NEG_BIG))
            acc_h = alpha * acc[h] + _dot(vt_sc[n, h], p)
            outs.append(acc_h[:HEAD_DIM] / acc_h[HEAD_DIM:HEAD_DIM + 1])
        rows = pl.ds(pl.multiple_of(n * blk, blk), blk)
        o_ref[rows, :] = jnp.transpose(jnp.concatenate(outs, axis=0)).astype(o_ref.dtype)
        return carry

    lax.fori_loop(0, n_blocks, query_block, 0, unroll=8)


def _moba(qkv, batch, seq):
    assert seq % MOBA_BLOCK == 0
    n_blocks = seq // MOBA_BLOCK
    n_rows = -(-n_blocks // 16) * 16
    assert n_rows <= HEAD_DIM
    pairs = C_HEADS // 2
    return pl.pallas_call(
        functools.partial(_moba_seq_kernel, n_blocks=n_blocks),
        grid=(batch, pairs),
        in_specs=[
            pl.BlockSpec((seq, HEAD_PAIR), lambda b, j: (b, j)),
            pl.BlockSpec((seq, HEAD_PAIR), lambda b, j: (b, pairs + j)),
            pl.BlockSpec((seq, HEAD_PAIR), lambda b, j: (b, 2 * pairs + j)),
        ],
        out_specs=pl.BlockSpec((seq, HEAD_PAIR), lambda b, j: (b, j)),
        out_shape=jax.ShapeDtypeStruct((batch * seq, pairs * HEAD_PAIR), BF16),
        scratch_shapes=[
            pltpu.VMEM((n_rows, HEAD_PAIR), F32),
            pltpu.VMEM((n_blocks, HEAD_PAIR, MOBA_BLOCK), BF16),
            pltpu.VMEM((n_blocks, 2, VT_ROWS, MOBA_BLOCK), BF16),
            pltpu.VMEM((n_blocks, 2, n_rows, MOBA_BLOCK), BF16),
            pltpu.VMEM((2, 2, MOBA_BLOCK, MOBA_BLOCK), F32),
        ],
        compiler_params=_params("parallel", "parallel"),
        name="moba",
    )(qkv, qkv, qkv)


def _memattn_kernel(q_ref, k_ref, v_ref, o_ref):
    s = _dot_nt(q_ref[...], k_ref[...]) * (X_HEAD_DIM ** -0.5)
    m = jnp.max(s, axis=-1, keepdims=True)
    p = jnp.exp(s - m)
    l = jnp.sum(p, axis=-1, keepdims=True)
    o_ref[...] = (_dot(p.astype(BF16), v_ref[...]) / l).astype(o_ref.dtype)


def _mem_attention(q, kv, batch, seq, mem_len, q_tile=1024):
    q_tile = min(q_tile, seq)
    tiles = seq // q_tile
    return pl.pallas_call(
        _memattn_kernel,
        grid=(batch, tiles, X_HEADS),
        in_specs=[
            pl.BlockSpec((q_tile, X_HEAD_DIM), lambda b, i, h: (b * tiles + i, h)),
            pl.BlockSpec((mem_len, X_HEAD_DIM), lambda b, i, h: (b, h)),
            pl.BlockSpec((mem_len, X_HEAD_DIM), lambda b, i, h: (b, X_HEADS + h)),
        ],
        out_specs=pl.BlockSpec((q_tile, X_HEAD_DIM), lambda b, i, h: (b * tiles + i, h)),
        out_shape=jax.ShapeDtypeStruct((batch * seq, D_MODEL), BF16),
        compiler_params=_params("parallel", "parallel", "arbitrary"),
        name="mem_attention",
    )(q, kv, kv)


def _prep_ffn(w_in, w_out):
    return w_in.astype(BF16), w_out.astype(BF16)


def _mem_block(x, mem2, w_q, w_kv, w_o, g, b, batch, seq, mem_len):
    q = _project(x, w_q.astype(BF16), BF16)
    kv = _project(mem2, w_kv.astype(BF16), BF16)
    o = _mem_attention(q, kv, batch, seq, mem_len)
    return _outproj_deepnorm([o], [w_o.astype(BF16)], x, g, b)


def kernel(x, mem, l0_ffn1_w_in, l0_ffn1_w_out, l0_ln1_g, l0_ln1_b, l0_mix_w_in, l0_gmlp_ln_g, l0_gmlp_ln_b, l0_gmlp_w_s, l0_gmlp_b_s, l0_mix_w_out, l0_ln2_g, l0_ln2_b, l0_mem_w_q, l0_mem_w_kv, l0_mem_w_o, l0_ln3_g, l0_ln3_b, l0_ffn2_w_in, l0_ffn2_w_out, l0_ln4_g, l0_ln4_b, l1_ffn1_w_in, l1_ffn1_w_out, l1_ln1_g, l1_ln1_b, l1_mix_w_in, l1_mix_w_out, l1_ln2_g, l1_ln2_b, l1_mem_w_q, l1_mem_w_kv, l1_mem_w_o, l1_ln3_g, l1_ln3_b, l1_ffn2_w_in, l1_ffn2_w_out, l1_ln4_g, l1_ln4_b):
    batch, seq, d = x.shape
    mem_len = mem.shape[1]
    h = x.reshape(batch * seq, d)
    mem2 = mem.reshape(batch * mem_len, d)

    h = _ffn_deepnorm(h, *_prep_ffn(l0_ffn1_w_in, l0_ffn1_w_out), l0_ln1_g, l0_ln1_b)
    qkv = _project(h, l0_mix_w_in[:, :A_QKV_WIDTH].astype(BF16), F32)
    uv = _project(h, l0_mix_w_in[:, A_QKV_WIDTH:].astype(BF16), F32)
    a_out = _dilated_attention(qkv, batch, seq)
    b_out = _gmlp(uv, l0_gmlp_ln_g, l0_gmlp_ln_b, l0_gmlp_w_s, l0_gmlp_b_s)
    a_width = a_out.shape[1]
    w_mix_out = [l0_mix_w_out[:a_width].astype(BF16), l0_mix_w_out[a_width:].astype(BF16)]
    h = _outproj_deepnorm([a_out, b_out], w_mix_out, h, l0_ln2_g, l0_ln2_b)
    h = _mem_block(h, mem2, l0_mem_w_q, l0_mem_w_kv, l0_mem_w_o, l0_ln3_g, l0_ln3_b, batch, seq, mem_len)
    h = _ffn_deepnorm(h, *_prep_ffn(l0_ffn2_w_in, l0_ffn2_w_out), l0_ln4_g, l0_ln4_b)

    h = _ffn_deepnorm(h, *_prep_ffn(l1_ffn1_w_in, l1_ffn1_w_out), l1_ln1_g, l1_ln1_b)
    qkv = _project(h, l1_mix_w_in.astype(BF16), BF16)
    o = _moba(qkv, batch, seq)
    h = _outproj_deepnorm([o], [l1_mix_w_out.astype(BF16)], h, l1_ln2_g, l1_ln2_b)
    h = _mem_block(h, mem2, l1_mem_w_q, l1_mem_w_kv, l1_mem_w_o, l1_ln3_g, l1_ln3_b, batch, seq, mem_len)
    h = _ffn_deepnorm(h, *_prep_ffn(l1_ffn2_w_in, l1_ffn2_w_out), l1_ln4_g, l1_ln4_b)
    return h.reshape(batch, seq, d)
```

```python
import functools

import jax
import jax.numpy as jnp
from jax import lax
from jax.experimental import pallas as pl
from jax.experimental.pallas import tpu as pltpu

D_MODEL = 2048
DEPTH = 2
HEAD_DIM = 64
HEAD_PAIR = 2 * HEAD_DIM
DIL_GROUPS = ((128, 1), (512, 4), (2048, 16))
A_HEADS_PER_GROUP = 8
A_HEADS = A_HEADS_PER_GROUP * len(DIL_GROUPS)
BAND_BLOCK = 128
CHUNK = 128
B_GROUPS = 8
B_WIDTH = 1024
B_GROUP_DIM = B_WIDTH // B_GROUPS
C_HEADS = D_MODEL // HEAD_DIM
MOBA_BLOCK = 256
MOBA_TOPK = 3
X_HEADS = 4
X_HEAD_DIM = D_MODEL // X_HEADS
D_FF = 5504
DEEPNORM_ALPHA = (2 * DEPTH) ** 0.25
LN_EPS = 1e-5
A_QKV_WIDTH = 3 * A_HEADS * HEAD_DIM

LANES = 128
FF_TILE = 512
ROW_TILE = 512
VMEM_LIMIT = 56 * 1024 * 1024
NEG_BIG = -1e30
M_INIT = -1e29

BF16 = jnp.bfloat16
F32 = jnp.float32


def _params(*sem):
    return pltpu.CompilerParams(dimension_semantics=sem, vmem_limit_bytes=VMEM_LIMIT)


def _dot(a, b):
    return jnp.dot(a, b, preferred_element_type=F32)


def _dot_nt(a, b):
    return lax.dot_general(a, b, (((1,), (1,)), ((), ())), preferred_element_type=F32)


def _layer_norm(y, g, b):
    mu = jnp.mean(y, axis=-1, keepdims=True)
    yc = y - mu
    var = jnp.mean(yc * yc, axis=-1, keepdims=True)
    return yc * lax.rsqrt(var + LN_EPS) * g + b


FF_FULL = D_FF // FF_TILE
FF_TAIL = D_FF - FF_FULL * FF_TILE
assert FF_TAIL % LANES == 0 and FF_TAIL > 0


def _ff_tile_start(f, base=0):
    assert base % LANES == 0 and FF_TILE % LANES == 0
    return (jnp.minimum(f, FF_FULL - 1) * (FF_TILE // LANES) + base // LANES) * LANES


def _ffn_kernel(x_ref, wg_ref, wu_ref, wo_ref, wg_tail, wu_tail, wo_tail, g_ref, b_ref, o_ref,
                xb_ref, acc_ref, y_ref):
    i = pl.program_id(0)
    f = pl.program_id(1)
    last = f == pl.num_programs(1) - 1
    final_row = i == pl.num_programs(0) - 1

    def contribution(wg, wu, wo):
        xb = xb_ref[...]
        gate = _dot(xb, wg[...])
        up = _dot(xb, wu[...])
        act = (gate * jax.nn.sigmoid(gate)) * up
        return _dot(act.astype(BF16), wo[...])

    def pre_norm():
        return DEEPNORM_ALPHA * x_ref[...] + 0.5 * (acc_ref[...] + contribution(wg_tail, wu_tail, wo_tail))

    @pl.when(jnp.logical_and(i == 0, f == 0))
    def _():
        y_ref[...] = jnp.zeros_like(y_ref)

    @pl.when(f == 0)
    def _():
        xb_ref[...] = x_ref[...].astype(BF16)
        o_ref[...] = _layer_norm(y_ref[...], g_ref[...], b_ref[...])
        acc_ref[...] = contribution(wg_ref, wu_ref, wo_ref)

    @pl.when(jnp.logical_and(f > 0, jnp.logical_not(last)))
    def _():
        acc_ref[...] += contribution(wg_ref, wu_ref, wo_ref)

    @pl.when(jnp.logical_and(last, jnp.logical_not(final_row)))
    def _():
        y_ref[...] = pre_norm()

    @pl.when(jnp.logical_and(last, final_row))
    def _():
        o_ref[...] = _layer_norm(pre_norm(), g_ref[...], b_ref[...])


def _ffn_deepnorm(x, w_in, w_out, g, b):
    n, d = x.shape
    tail_start = FF_FULL * FF_TILE
    n_tiles = n // ROW_TILE
    full = lambda rows, cols: (pl.Element(rows), pl.Element(cols))

    def out_index(i, f):
        own_step = jnp.logical_and(i == n_tiles - 1, f == FF_FULL)
        return jnp.where(own_step, i, jnp.maximum(i - 1, 0)), 0

    return pl.pallas_call(
        _ffn_kernel,
        grid=(n_tiles, FF_FULL + 1),
        in_specs=[
            pl.BlockSpec((ROW_TILE, d), lambda i, f: (i, 0)),
            pl.BlockSpec(full(d, FF_TILE), lambda i, f: (0, _ff_tile_start(f))),
            pl.BlockSpec(full(d, FF_TILE), lambda i, f: (0, _ff_tile_start(f, D_FF))),
            pl.BlockSpec(full(FF_TILE, d), lambda i, f: (_ff_tile_start(f), 0)),
            pl.BlockSpec(full(d, FF_TAIL), lambda i, f: (0, tail_start)),
            pl.BlockSpec(full(d, FF_TAIL), lambda i, f: (0, D_FF + tail_start)),
            pl.BlockSpec(full(FF_TAIL, d), lambda i, f: (tail_start, 0)),
            pl.BlockSpec((1, d), lambda i, f: (0, 0)),
            pl.BlockSpec((1, d), lambda i, f: (0, 0)),
        ],
        out_specs=pl.BlockSpec((ROW_TILE, d), out_index),
        out_shape=jax.ShapeDtypeStruct((n, d), F32),
        scratch_shapes=[
            pltpu.VMEM((ROW_TILE, d), BF16),
            pltpu.VMEM((ROW_TILE, d), F32),
            pltpu.VMEM((ROW_TILE, d), F32),
        ],
        compiler_params=_params("arbitrary", "arbitrary"),
        name="ffn_deepnorm",
    )(x, w_in, w_in, w_out, w_in, w_in, w_out, g.reshape(1, d), b.reshape(1, d))


def _proj_kernel(x_ref, w_ref, o_ref):
    o_ref[...] = _dot(x_ref[...].astype(BF16), w_ref[...]).astype(o_ref.dtype)


PROJ_MAX_COLS = 2048


def _project(x, w, out_dtype, row_tile=1024):
    n, k = x.shape
    m = w.shape[1]
    row_tile = min(row_tile, n)
    col_tile = max(c for c in range(LANES, PROJ_MAX_COLS + 1, LANES) if m % c == 0)
    return pl.pallas_call(
        _proj_kernel,
        grid=(n // row_tile, m // col_tile),
        in_specs=[
            pl.BlockSpec((row_tile, k), lambda i, j: (i, 0)),
            pl.BlockSpec((k, col_tile), lambda i, j: (0, j)),
        ],
        out_specs=pl.BlockSpec((row_tile, col_tile), lambda i, j: (i, j)),
        out_shape=jax.ShapeDtypeStruct((n, m), out_dtype),
        compiler_params=_params("parallel", "arbitrary"),
        name="project",
    )(x, w)


OUTPROJ_CHUNK = 256


def _outproj_kernel(*refs, n_in):
    a_refs, w_refs = refs[:n_in], refs[n_in:2 * n_in]
    x_ref, g_ref, b_ref, o_ref = refs[2 * n_in:]
    for c in range(ROW_TILE // OUTPROJ_CHUNK):
        rows = slice(c * OUTPROJ_CHUNK, (c + 1) * OUTPROJ_CHUNK)
        fx = _dot(a_refs[0][rows, :], w_refs[0][...])
        for a_ref, w_ref in zip(a_refs[1:], w_refs[1:]):
            fx = fx + _dot(a_ref[rows, :], w_ref[...])
        y = DEEPNORM_ALPHA * x_ref[rows, :] + fx
        o_ref[rows, :] = _layer_norm(y, g_ref[...], b_ref[...])


def _outproj_deepnorm(a_list, w_list, x, g, b):
    n, d = x.shape
    n_in = len(a_list)
    in_specs = [pl.BlockSpec((ROW_TILE, a.shape[1]), lambda i: (i, 0)) for a in a_list]
    in_specs += [pl.BlockSpec(w.shape, lambda i: (0, 0)) for w in w_list]
    in_specs += [
        pl.BlockSpec((ROW_TILE, d), lambda i: (i, 0)),
        pl.BlockSpec((1, d), lambda i: (0, 0)),
        pl.BlockSpec((1, d), lambda i: (0, 0)),
    ]
    return pl.pallas_call(
        functools.partial(_outproj_kernel, n_in=n_in),
        grid=(n // ROW_TILE,),
        in_specs=in_specs,
        out_specs=pl.BlockSpec((ROW_TILE, d), lambda i: (i, 0)),
        out_shape=jax.ShapeDtypeStruct((n, d), F32),
        compiler_params=_params("parallel"),
        name="outproj_deepnorm",
    )(*a_list, *w_list, x, g.reshape(1, d), b.reshape(1, d))


def _dilated_kernel(*refs, seq):
    n_g = len(DIL_GROUPS)
    qkv_refs = refs[:3 * n_g]
    o_ref = refs[3 * n_g]
    o_sc, lse_sc, bias_sc = refs[3 * n_g + 1:]

    bb = BAND_BLOCK
    qi = lax.broadcasted_iota(jnp.int32, (bb, 2 * bb), 0)
    ki = lax.broadcasted_iota(jnp.int32, (bb, 2 * bb), 1)
    for first, delta in ((0, bb), (1, 0)):
        off = qi + delta - ki
        bias_sc[first] = jnp.where((off >= 0) & (off <= bb), 0.0, -jnp.inf).astype(F32)

    lane = lax.broadcasted_iota(jnp.int32, (bb, HEAD_PAIR), 1)
    head0 = lane < HEAD_DIM
    scale = HEAD_DIM ** -0.5

    for g, (window, dil) in enumerate(DIL_GROUPS):
        assert window // dil == bb
        q_ref, k_ref, v_ref = qkv_refs[3 * g:3 * g + 3]
        n_blk = seq // (dil * bb)

        def body(it, carry, q_ref=q_ref, k_ref=k_ref, v_ref=v_ref, dil=dil, g=g):
            blk = it // dil
            phase = it - blk * dil
            is_first = jnp.where(blk == 0, 1, 0)
            q_start = blk * (bb * dil) + phase
            k_start = jnp.maximum(blk - 1, 0) * (bb * dil) + phase
            if dil == 1:
                q_rows, k_rows = pl.ds(q_start, bb), pl.ds(k_start, 2 * bb)
            else:
                q_rows = pl.ds(q_start, bb, stride=dil)
                k_rows = pl.ds(k_start, 2 * bb, stride=dil)
            q = q_ref[q_rows, :] * scale
            k = k_ref[k_rows, :].astype(BF16)
            v = v_ref[k_rows, :].astype(BF16)
            bias = bias_sc[is_first]
            outs, lses = [], []
            for h0 in (True, False):
                qh = jnp.where(head0 if h0 else ~head0, q, 0.0).astype(BF16)
                s = _dot_nt(qh, k) + bias
                m = jnp.max(s, axis=-1, keepdims=True)
                p = jnp.exp(s - m)
                l = jnp.sum(p, axis=-1, keepdims=True)
                outs.append(_dot(p.astype(BF16), v) / l)
                lses.append(m + jnp.log(l))
            o_sc[g, q_rows, :] = jnp.where(head0, outs[0], outs[1])
            lse_sc[g, q_rows, :] = jnp.where(head0, lses[0], lses[1])
            return carry

        lax.fori_loop(0, n_blk * dil, body, 0, unroll=8)

    lse = [lse_sc[g] for g in range(n_g)]
    top = functools.reduce(jnp.maximum, lse)
    w = [jnp.exp(x - top) for x in lse]
    den = functools.reduce(jnp.add, w)
    acc = functools.reduce(jnp.add, [w[g] * o_sc[g] for g in range(n_g)])
    o_ref[...] = (acc / den).astype(o_ref.dtype)


def _dilated_attention(qkv, batch, seq):
    n_g = len(DIL_GROUPS)
    pairs = A_HEADS_PER_GROUP // 2
    blocks_per_part = A_HEADS * HEAD_DIM // HEAD_PAIR
    in_specs = []
    for g in range(n_g):
        for part in range(3):
            base = part * blocks_per_part + g * pairs
            in_specs.append(pl.BlockSpec((seq, HEAD_PAIR), lambda b, j, base=base: (b, base + j)))
    return pl.pallas_call(
        functools.partial(_dilated_kernel, seq=seq),
        grid=(batch, pairs),
        in_specs=in_specs,
        out_specs=pl.BlockSpec((seq, HEAD_PAIR), lambda b, j: (b, j)),
        out_shape=jax.ShapeDtypeStruct((batch * seq, pairs * HEAD_PAIR), BF16),
        scratch_shapes=[
            pltpu.VMEM((n_g, seq, HEAD_PAIR), F32),
            pltpu.VMEM((n_g, seq, HEAD_PAIR), F32),
            pltpu.VMEM((2, BAND_BLOCK, 2 * BAND_BLOCK), F32),
        ],
        compiler_params=_params("parallel", "parallel"),
        name="dilated_attention",
    )(*([qkv] * (3 * n_g)))


def _gmlp_kernel(u_ref, v_ref, g_ref, b_ref, ws_ref, bias_ref, o_ref, *, chunks):
    u = jax.nn.gelu(u_ref[...])
    v = _layer_norm(jax.nn.gelu(v_ref[...]), g_ref[...], b_ref[...]).astype(BF16)
    row = lax.broadcasted_iota(jnp.int32, (CHUNK, CHUNK), 0)
    col = lax.broadcasted_iota(jnp.int32, (CHUNK, CHUNK), 1)
    tril = row >= col
    bias = bias_ref[...]
    for grp in range(B_GROUPS):
        w = jnp.where(tril, ws_ref[grp], 0.0).astype(BF16)
        cols = slice(grp * B_GROUP_DIM, (grp + 1) * B_GROUP_DIM)
        for c in range(chunks):
            rows = slice(c * CHUNK, (c + 1) * CHUNK)
            mixed = _dot(w, v[rows, cols]) + bias[:, cols]
            o_ref[rows, cols] = (u[rows, cols] * mixed).astype(o_ref.dtype)


def _gmlp(uv, ln_g, ln_b, w_s, b_s, chunks=4):
    n = uv.shape[0]
    rows = chunks * CHUNK
    bias_full = jnp.repeat(b_s.T, B_GROUP_DIM, axis=1)
    return pl.pallas_call(
        functools.partial(_gmlp_kernel, chunks=chunks),
        grid=(n // rows,),
        in_specs=[
            pl.BlockSpec((rows, B_WIDTH), lambda i: (i, 0)),
            pl.BlockSpec((rows, B_WIDTH), lambda i: (i, 1)),
            pl.BlockSpec((1, B_WIDTH), lambda i: (0, 0)),
            pl.BlockSpec((1, B_WIDTH), lambda i: (0, 0)),
            pl.BlockSpec((B_GROUPS, CHUNK, CHUNK), lambda i: (0, 0, 0)),
            pl.BlockSpec((CHUNK, B_WIDTH), lambda i: (0, 0)),
        ],
        out_specs=pl.BlockSpec((rows, B_WIDTH), lambda i: (i, 0)),
        out_shape=jax.ShapeDtypeStruct((n, B_WIDTH), BF16),
        compiler_params=_params("parallel"),
        name="gmlp",
    )(uv, uv, ln_g.reshape(1, B_WIDTH), ln_b.reshape(1, B_WIDTH), w_s, bias_full)


VT_ROWS = HEAD_DIM + 16


def _moba_seq_kernel(q_ref, k_ref, v_ref, o_ref, kmean_sc, qt_sc, vt_sc, selb_sc, st_sc, *, n_blocks):
    blk = MOBA_BLOCK
    n_rows = kmean_sc.shape[0]
    scale = HEAD_DIM ** -0.5

    kmean_sc[...] = jnp.zeros_like(kmean_sc)
    ones_rows = (lax.broadcasted_iota(jnp.int32, (VT_ROWS - HEAD_DIM, blk), 0) == 0).astype(BF16)

    def prep(m, carry):
        rows = pl.ds(pl.multiple_of(m * blk, blk), blk)
        kmean_sc[pl.ds(m, 1), :] = jnp.mean(k_ref[rows, :].astype(F32), axis=0, keepdims=True)
        qt_sc[m] = (jnp.transpose(q_ref[rows, :].astype(F32)) * scale).astype(BF16)
        vt = jnp.transpose(v_ref[rows, :].astype(F32)).astype(BF16)
        for h in range(2):
            vt_sc[m, h] = jnp.concatenate([vt[h * HEAD_DIM:(h + 1) * HEAD_DIM], ones_rows], axis=0)
        return carry

    lax.fori_loop(0, n_blocks, prep, 0, unroll=2)

    chan = lax.broadcasted_iota(jnp.int32, (HEAD_PAIR, blk), 0)
    head_rows = [chan < HEAD_DIM, chan >= HEAD_DIM]
    kmean = kmean_sc[...].astype(BF16)
    cand = lax.broadcasted_iota(jnp.int32, (n_rows, blk), 0)

    def select(n, carry):
        qt = qt_sc[n]
        for h in range(2):
            gate = _dot(kmean, jnp.where(head_rows[h], qt, 0))
            left = jnp.where(cand < n, gate, -jnp.inf)
            chosen = cand == n
            for _ in range(MOBA_TOPK):
                best = jnp.max(left, axis=0, keepdims=True)
                first = jnp.min(jnp.where(left == best, cand, n_rows), axis=0, keepdims=True)
                take = (cand == first) & (best > -jnp.inf)
                chosen = chosen | (take & (best < jnp.inf))
                left = jnp.where(take, -jnp.inf, left)
            selb_sc[n, h] = jnp.where(chosen, 0.0, NEG_BIG).astype(BF16)
        return carry

    lax.fori_loop(0, n_blocks, select, 0, unroll=2)

    key_i = lax.broadcasted_iota(jnp.int32, (blk, blk), 0)
    qry_i = lax.broadcasted_iota(jnp.int32, (blk, blk), 1)
    causal = key_i <= qry_i
    key_lane = lax.broadcasted_iota(jnp.int32, (blk, HEAD_PAIR), 1)
    head_lanes = [key_lane < HEAD_DIM, key_lane >= HEAD_DIM]
    bias_base = [HEAD_DIM, 0]
    pad_rows = jnp.zeros((HEAD_DIM - n_rows, blk), BF16)

    def softmax_stage(m_run, st):
        m_new = jnp.maximum(m_run, jnp.max(st, axis=0, keepdims=True))
        return m_new, jnp.exp(m_run - m_new), jnp.exp(st - m_new).astype(BF16)

    def query_block(n, carry):
        qt = qt_sc[n]
        w_aug = [jnp.concatenate([qt[:HEAD_DIM], selb_sc[n, 0], pad_rows], axis=0),
                 jnp.concatenate([selb_sc[n, 1], pad_rows, qt[HEAD_DIM:]], axis=0)]

        def scores(m):
            k_m = k_ref[pl.ds(pl.multiple_of(m * blk, blk), blk), :]
            out = []
            for h in range(2):
                k_aug = jnp.where(head_lanes[h], k_m, (key_lane == bias_base[h] + m).astype(BF16))
                out.append(_dot(k_aug, w_aug[h]))
            return out

        def body(i, state):
            m_run, acc = state
            st_cur = [st_sc[i % 2, h] for h in range(2)]
            st_next = scores(i + 1)
            out_m, out_acc = [], []
            for h in range(2):
                st_sc[(i + 1) % 2, h] = st_next[h]
                m_new, alpha, p = softmax_stage(m_run[h], st_cur[h])
                out_m.append(m_new)
                out_acc.append(alpha * acc[h] + _dot(vt_sc[i, h], p))
            return out_m, out_acc

        for h, st in enumerate(scores(0)):
            st_sc[0, h] = st
        init = ([jnp.full((1, blk), M_INIT, F32)] * 2, [jnp.zeros((VT_ROWS, blk), F32)] * 2)
        m_run, acc = lax.fori_loop(0, n, body, init)
        outs = []
        for h in range(2):
            _, alpha, p = softmax_stage(m_run[h], jnp.where(causal, st_sc[n % 2, h], NEG_BIG))
            acc_h = alpha * acc[h] + _dot(vt_sc[n, h], p)
            outs.append(acc_h[:HEAD_DIM] / acc_h[HEAD_DIM:HEAD_DIM + 1])
        rows = pl.ds(pl.multiple_of(n * blk, blk), blk)
        o_ref[rows, :] = jnp.transpose(jnp.concatenate(outs, axis=0)).astype(o_ref.dtype)
        return carry

    lax.fori_loop(0, n_blocks, query_block, 0, unroll=8)


def _moba(qkv, batch, seq):
    assert seq % MOBA_BLOCK == 0
    n_blocks = seq // MOBA_BLOCK
    n_rows = -(-n_blocks // 16) * 16
    assert n_rows <= HEAD_DIM
    pairs = C_HEADS // 2
    return pl.pallas_call(
        functools.partial(_moba_seq_kernel, n_blocks=n_blocks),
        grid=(batch, pairs),
        in_specs=[
            pl.BlockSpec((seq, HEAD_PAIR), lambda b, j: (b, j)),
            pl.BlockSpec((seq, HEAD_PAIR), lambda b, j: (b, pairs + j)),
            pl.BlockSpec((seq, HEAD_PAIR), lambda b, j: (b, 2 * pairs + j)),
        ],
        out_specs=pl.BlockSpec((seq, HEAD_PAIR), lambda b, j: (b, j)),
        out_shape=jax.ShapeDtypeStruct((batch * seq, pairs * HEAD_PAIR), BF16),
        scratch_shapes=[
            pltpu.VMEM((n_rows, HEAD_PAIR), F32),
            pltpu.VMEM((n_blocks, HEAD_PAIR, MOBA_BLOCK), BF16),
            pltpu.VMEM((n_blocks, 2, VT_ROWS, MOBA_BLOCK), BF16),
            pltpu.VMEM((n_blocks, 2, n_rows, MOBA_BLOCK), BF16),
            pltpu.VMEM((2, 2, MOBA_BLOCK, MOBA_BLOCK), F32),
        ],
        compiler_params=_params("parallel", "parallel"),
        name="moba",
    )(qkv, qkv, qkv)


def _memattn_kernel(q_ref, k_ref, v_ref, o_ref):
    s = _dot_nt(q_ref[...], k_ref[...]) * (X_HEAD_DIM ** -0.5)
    m = jnp.max(s, axis=-1, keepdims=True)
    p = jnp.exp(s - m)
    l = jnp.sum(p, axis=-1, keepdims=True)
    o_ref[...] = (_dot(p.astype(BF16), v_ref[...]) / l).astype(o_ref.dtype)


def _mem_attention(q, kv, batch, seq, mem_len, q_tile=1024):
    q_tile = min(q_tile, seq)
    tiles = seq // q_tile
    return pl.pallas_call(
        _memattn_kernel,
        grid=(batch, tiles, X_HEADS),
        in_specs=[
            pl.BlockSpec((q_tile, X_HEAD_DIM), lambda b, i, h: (b * tiles + i, h)),
            pl.BlockSpec((mem_len, X_HEAD_DIM), lambda b, i, h: (b, h)),
            pl.BlockSpec((mem_len, X_HEAD_DIM), lambda b, i, h: (b, X_HEADS + h)),
        ],
        out_specs=pl.BlockSpec((q_tile, X_HEAD_DIM), lambda b, i, h: (b * tiles + i, h)),
        out_shape=jax.ShapeDtypeStruct((batch * seq, D_MODEL), BF16),
        compiler_params=_params("parallel", "parallel", "arbitrary"),
        name="mem_attention",
    )(q, kv, kv)


def _prep_ffn(w_in, w_out):
    return w_in.astype(BF16), w_out.astype(BF16)


def _mem_block(x, mem2, w_q, w_kv, w_o, g, b, batch, seq, mem_len):
    q = _project(x, w_q.astype(BF16), BF16)
    kv = _project(mem2, w_kv.astype(BF16), BF16)
    o = _mem_attention(q, kv, batch, seq, mem_len)
    return _outproj_deepnorm([o], [w_o.astype(BF16)], x, g, b)


def kernel(x, mem, l0_ffn1_w_in, l0_ffn1_w_out, l0_ln1_g, l0_ln1_b, l0_mix_w_in, l0_gmlp_ln_g, l0_gmlp_ln_b, l0_gmlp_w_s, l0_gmlp_b_s, l0_mix_w_out, l0_ln2_g, l0_ln2_b, l0_mem_w_q, l0_mem_w_kv, l0_mem_w_o, l0_ln3_g, l0_ln3_b, l0_ffn2_w_in, l0_ffn2_w_out, l0_ln4_g, l0_ln4_b, l1_ffn1_w_in, l1_ffn1_w_out, l1_ln1_g, l1_ln1_b, l1_mix_w_in, l1_mix_w_out, l1_ln2_g, l1_ln2_b, l1_mem_w_q, l1_mem_w_kv, l1_mem_w_o, l1_ln3_g, l1_ln3_b, l1_ffn2_w_in, l1_ffn2_w_out, l1_ln4_g, l1_ln4_b):
    batch, seq, d = x.shape
    mem_len = mem.shape[1]
    h = x.reshape(batch * seq, d)
    mem2 = mem.reshape(batch * mem_len, d)

    h = _ffn_deepnorm(h, *_prep_ffn(l0_ffn1_w_in, l0_ffn1_w_out), l0_ln1_g, l0_ln1_b)
    qkv = _project(h, l0_mix_w_in[:, :A_QKV_WIDTH].astype(BF16), F32)
    uv = _project(h, l0_mix_w_in[:, A_QKV_WIDTH:].astype(BF16), F32)
    a_out = _dilated_attention(qkv, batch, seq)
    b_out = _gmlp(uv, l0_gmlp_ln_g, l0_gmlp_ln_b, l0_gmlp_w_s, l0_gmlp_b_s)
    a_width = a_out.shape[1]
    w_mix_out = [l0_mix_w_out[:a_width].astype(BF16), l0_mix_w_out[a_width:].astype(BF16)]
    h = _outproj_deepnorm([a_out, b_out], w_mix_out, h, l0_ln2_g, l0_ln2_b)
    h = _mem_block(h, mem2, l0_mem_w_q, l0_mem_w_kv, l0_mem_w_o, l0_ln3_g, l0_ln3_b, batch, seq, mem_len)
    h = _ffn_deepnorm(h, *_prep_ffn(l0_ffn2_w_in, l0_ffn2_w_out), l0_ln4_g, l0_ln4_b)

    h = _ffn_deepnorm(h, *_prep_ffn(l1_ffn1_w_in, l1_ffn1_w_out), l1_ln1_g, l1_ln1_b)
    qkv = _project(h, l1_mix_w_in.astype(BF16), BF16)
    o = _moba(qkv, batch, seq)
    h = _outproj_deepnorm([o], [l1_mix_w_out.astype(BF16)], h, l1_ln2_g, l1_ln2_b)
    h = _mem_block(h, mem2, l1_mem_w_q, l1_mem_w_kv, l1_mem_w_o, l1_ln3_g, l1_ln3_b, batch, seq, mem_len)
    h = _ffn_deepnorm(h, *_prep_ffn(l1_ffn2_w_in, l1_ffn2_w_out), l1_ln4_g, l1_ln4_b)
    return h.reshape(batch, seq, d)
```

```python
import functools

import jax
import jax.numpy as jnp
from jax import lax
from jax.experimental import pallas as pl
from jax.experimental.pallas import tpu as pltpu

D_MODEL = 2048
DEPTH = 2
HEAD_DIM = 64
HEAD_PAIR = 2 * HEAD_DIM
DIL_GROUPS = ((128, 1), (512, 4), (2048, 16))
A_HEADS_PER_GROUP = 8
A_HEADS = A_HEADS_PER_GROUP * len(DIL_GROUPS)
BAND_BLOCK = 128
CHUNK = 128
B_GROUPS = 8
B_WIDTH = 1024
B_GROUP_DIM = B_WIDTH // B_GROUPS
C_HEADS = D_MODEL // HEAD_DIM
MOBA_BLOCK = 256
MOBA_TOPK = 3
X_HEADS = 4
X_HEAD_DIM = D_MODEL // X_HEADS
D_FF = 5504
DEEPNORM_ALPHA = (2 * DEPTH) ** 0.25
LN_EPS = 1e-5
A_QKV_WIDTH = 3 * A_HEADS * HEAD_DIM

LANES = 128
FF_TILE = 768
ROW_TILE = 512
VMEM_LIMIT = 56 * 1024 * 1024
NEG_BIG = -1e30
M_INIT = -1e29

BF16 = jnp.bfloat16
F32 = jnp.float32


def _params(*sem):
    return pltpu.CompilerParams(dimension_semantics=sem, vmem_limit_bytes=VMEM_LIMIT)


def _dot(a, b):
    return jnp.dot(a, b, preferred_element_type=F32)


def _dot_nt(a, b):
    return lax.dot_general(a, b, (((1,), (1,)), ((), ())), preferred_element_type=F32)


def _layer_norm(y, g, b):
    mu = jnp.mean(y, axis=-1, keepdims=True)
    yc = y - mu
    var = jnp.mean(yc * yc, axis=-1, keepdims=True)
    return yc * lax.rsqrt(var + LN_EPS) * g + b


FF_FULL = D_FF // FF_TILE
FF_TAIL = D_FF - FF_FULL * FF_TILE
assert FF_TAIL % LANES == 0 and FF_TAIL > 0


def _ff_tile_start(f, base=0):
    assert base % LANES == 0 and FF_TILE % LANES == 0
    return (jnp.minimum(f, FF_FULL - 1) * (FF_TILE // LANES) + base // LANES) * LANES


def _ffn_kernel(x_ref, wg_ref, wu_ref, wo_ref, wg_tail, wu_tail, wo_tail, g_ref, b_ref, o_ref,
                xb_ref, acc_ref, y_ref):
    i = pl.program_id(0)
    f = pl.program_id(1)
    last = f == pl.num_programs(1) - 1
    final_row = i == pl.num_programs(0) - 1

    def contribution(wg, wu, wo):
        xb = xb_ref[...]
        gate = _dot(xb, wg[...])
        up = _dot(xb, wu[...])
        act = (gate * jax.nn.sigmoid(gate)) * up
        return _dot(act.astype(BF16), wo[...])

    def pre_norm():
        return DEEPNORM_ALPHA * x_ref[...] + 0.5 * (acc_ref[...] + contribution(wg_tail, wu_tail, wo_tail))

    @pl.when(jnp.logical_and(i == 0, f == 0))
    def _():
        y_ref[...] = jnp.zeros_like(y_ref)

    @pl.when(f == 0)
    def _():
        xb_ref[...] = x_ref[...].astype(BF16)
        o_ref[...] = _layer_norm(y_ref[...], g_ref[...], b_ref[...])
        acc_ref[...] = contribution(wg_ref, wu_ref, wo_ref)

    @pl.when(jnp.logical_and(f > 0, jnp.logical_not(last)))
    def _():
        acc_ref[...] += contribution(wg_ref, wu_ref, wo_ref)

    @pl.when(jnp.logical_and(last, jnp.logical_not(final_row)))
    def _():
        y_ref[...] = pre_norm()

    @pl.when(jnp.logical_and(last, final_row))
    def _():
        o_ref[...] = _layer_norm(pre_norm(), g_ref[...], b_ref[...])


def _ffn_deepnorm(x, w_in, w_out, g, b):
    n, d = x.shape
    tail_start = FF_FULL * FF_TILE
    n_tiles = n // ROW_TILE
    full = lambda rows, cols: (pl.Element(rows), pl.Element(cols))

    def out_index(i, f):
        own_step = jnp.logical_and(i == n_tiles - 1, f == FF_FULL)
        return jnp.where(own_step, i, jnp.maximum(i - 1, 0)), 0

    return pl.pallas_call(
        _ffn_kernel,
        grid=(n_tiles, FF_FULL + 1),
        in_specs=[
            pl.BlockSpec((ROW_TILE, d), lambda i, f: (i, 0)),
            pl.BlockSpec(full(d, FF_TILE), lambda i, f: (0, _ff_tile_start(f))),
            pl.BlockSpec(full(d, FF_TILE), lambda i, f: (0, _ff_tile_start(f, D_FF))),
            pl.BlockSpec(full(FF_TILE, d), lambda i, f: (_ff_tile_start(f), 0)),
            pl.BlockSpec(full(d, FF_TAIL), lambda i, f: (0, tail_start)),
            pl.BlockSpec(full(d, FF_TAIL), lambda i, f: (0, D_FF + tail_start)),
            pl.BlockSpec(full(FF_TAIL, d), lambda i, f: (tail_start, 0)),
            pl.BlockSpec((1, d), lambda i, f: (0, 0)),
            pl.BlockSpec((1, d), lambda i, f: (0, 0)),
        ],
        out_specs=pl.BlockSpec((ROW_TILE, d), out_index),
        out_shape=jax.ShapeDtypeStruct((n, d), F32),
        scratch_shapes=[
            pltpu.VMEM((ROW_TILE, d), BF16),
            pltpu.VMEM((ROW_TILE, d), F32),
            pltpu.VMEM((ROW_TILE, d), F32),
        ],
        compiler_params=_params("arbitrary", "arbitrary"),
        name="ffn_deepnorm",
    )(x, w_in, w_in, w_out, w_in, w_in, w_out, g.reshape(1, d), b.reshape(1, d))


def _proj_kernel(x_ref, w_ref, o_ref):
    o_ref[...] = _dot(x_ref[...].astype(BF16), w_ref[...]).astype(o_ref.dtype)


PROJ_MAX_COLS = 2048


def _project(x, w, out_dtype, row_tile=1024):
    n, k = x.shape
    m = w.shape[1]
    row_tile = min(row_tile, n)
    col_tile = max(c for c in range(LANES, PROJ_MAX_COLS + 1, LANES) if m % c == 0)
    return pl.pallas_call(
        _proj_kernel,
        grid=(n // row_tile, m // col_tile),
        in_specs=[
            pl.BlockSpec((row_tile, k), lambda i, j: (i, 0)),
            pl.BlockSpec((k, col_tile), lambda i, j: (0, j)),
        ],
        out_specs=pl.BlockSpec((row_tile, col_tile), lambda i, j: (i, j)),
        out_shape=jax.ShapeDtypeStruct((n, m), out_dtype),
        compiler_params=_params("parallel", "arbitrary"),
        name="project",
    )(x, w)


OUTPROJ_CHUNK = 256


def _outproj_kernel(*refs, n_in):
    a_refs, w_refs = refs[:n_in], refs[n_in:2 * n_in]
    x_ref, g_ref, b_ref, o_ref = refs[2 * n_in:]
    for c in range(ROW_TILE // OUTPROJ_CHUNK):
        rows = slice(c * OUTPROJ_CHUNK, (c + 1) * OUTPROJ_CHUNK)
        fx = _dot(a_refs[0][rows, :], w_refs[0][...])
        for a_ref, w_ref in zip(a_refs[1:], w_refs[1:]):
            fx = fx + _dot(a_ref[rows, :], w_ref[...])
        y = DEEPNORM_ALPHA * x_ref[rows, :] + fx
        o_ref[rows, :] = _layer_norm(y, g_ref[...], b_ref[...])


def _outproj_deepnorm(a_list, w_list, x, g, b):
    n, d = x.shape
    n_in = len(a_list)
    in_specs = [pl.BlockSpec((ROW_TILE, a.shape[1]), lambda i: (i, 0)) for a in a_list]
    in_specs += [pl.BlockSpec(w.shape, lambda i: (0, 0)) for w in w_list]
    in_specs += [
        pl.BlockSpec((ROW_TILE, d), lambda i: (i, 0)),
        pl.BlockSpec((1, d), lambda i: (0, 0)),
        pl.BlockSpec((1, d), lambda i: (0, 0)),
    ]
    return pl.pallas_call(
        functools.partial(_outproj_kernel, n_in=n_in),
        grid=(n // ROW_TILE,),
        in_specs=in_specs,
        out_specs=pl.BlockSpec((ROW_TILE, d), lambda i: (i, 0)),
        out_shape=jax.ShapeDtypeStruct((n, d), F32),
        compiler_params=_params("parallel"),
        name="outproj_deepnorm",
    )(*a_list, *w_list, x, g.reshape(1, d), b.reshape(1, d))


def _dilated_kernel(*refs, seq):
    n_g = len(DIL_GROUPS)
    qkv_refs = refs[:3 * n_g]
    o_ref = refs[3 * n_g]
    o_sc, lse_sc, bias_sc = refs[3 * n_g + 1:]

    bb = BAND_BLOCK
    qi = lax.broadcasted_iota(jnp.int32, (bb, 2 * bb), 0)
    ki = lax.broadcasted_iota(jnp.int32, (bb, 2 * bb), 1)
    for first, delta in ((0, bb), (1, 0)):
        off = qi + delta - ki
        bias_sc[first] = jnp.where((off >= 0) & (off <= bb), 0.0, -jnp.inf).astype(F32)

    lane = lax.broadcasted_iota(jnp.int32, (bb, HEAD_PAIR), 1)
    head0 = lane < HEAD_DIM
    scale = HEAD_DIM ** -0.5

    for g, (window, dil) in enumerate(DIL_GROUPS):
        assert window // dil == bb
        q_ref, k_ref, v_ref = qkv_refs[3 * g:3 * g + 3]
        n_blk = seq // (dil * bb)

        def body(it, carry, q_ref=q_ref, k_ref=k_ref, v_ref=v_ref, dil=dil, g=g):
            blk = it // dil
            phase = it - blk * dil
            is_first = jnp.where(blk == 0, 1, 0)
            q_start = blk * (bb * dil) + phase
            k_start = jnp.maximum(blk - 1, 0) * (bb * dil) + phase
            if dil == 1:
                q_rows, k_rows = pl.ds(q_start, bb), pl.ds(k_start, 2 * bb)
            else:
                q_rows = pl.ds(q_start, bb, stride=dil)
                k_rows = pl.ds(k_start, 2 * bb, stride=dil)
            q = q_ref[q_rows, :] * scale
            k = k_ref[k_rows, :].astype(BF16)
            v = v_ref[k_rows, :].astype(BF16)
            bias = bias_sc[is_first]
            outs, lses = [], []
            for h0 in (True, False):
                qh = jnp.where(head0 if h0 else ~head0, q, 0.0).astype(BF16)
                s = _dot_nt(qh, k) + bias
                m = jnp.max(s, axis=-1, keepdims=True)
                p = jnp.exp(s - m)
                l = jnp.sum(p, axis=-1, keepdims=True)
                outs.append(_dot(p.astype(BF16), v) / l)
                lses.append(m + jnp.log(l))
            o_sc[g, q_rows, :] = jnp.where(head0, outs[0], outs[1])
            lse_sc[g, q_rows, :] = jnp.where(head0, lses[0], lses[1])
            return carry

        lax.fori_loop(0, n_blk * dil, body, 0, unroll=8)

    lse = [lse_sc[g] for g in range(n_g)]
    top = functools.reduce(jnp.maximum, lse)
    w = [jnp.exp(x - top) for x in lse]
    den = functools.reduce(jnp.add, w)
    acc = functools.reduce(jnp.add, [w[g] * o_sc[g] for g in range(n_g)])
    o_ref[...] = (acc / den).astype(o_ref.dtype)


def _dilated_attention(qkv, batch, seq):
    n_g = len(DIL_GROUPS)
    pairs = A_HEADS_PER_GROUP // 2
    blocks_per_part = A_HEADS * HEAD_DIM // HEAD_PAIR
    in_specs = []
    for g in range(n_g):
        for part in range(3):
            base = part * blocks_per_part + g * pairs
            in_specs.append(pl.BlockSpec((seq, HEAD_PAIR), lambda b, j, base=base: (b, base + j)))
    return pl.pallas_call(
        functools.partial(_dilated_kernel, seq=seq),
        grid=(batch, pairs),
        in_specs=in_specs,
        out_specs=pl.BlockSpec((seq, HEAD_PAIR), lambda b, j: (b, j)),
        out_shape=jax.ShapeDtypeStruct((batch * seq, pairs * HEAD_PAIR), BF16),
        scratch_shapes=[
            pltpu.VMEM((n_g, seq, HEAD_PAIR), F32),
            pltpu.VMEM((n_g, seq, HEAD_PAIR), F32),
            pltpu.VMEM((2, BAND_BLOCK, 2 * BAND_BLOCK), F32),
        ],
        compiler_params=_params("parallel", "parallel"),
        name="dilated_attention",
    )(*([qkv] * (3 * n_g)))


def _gmlp_kernel(u_ref, v_ref, g_ref, b_ref, ws_ref, bias_ref, o_ref, *, chunks):
    u = jax.nn.gelu(u_ref[...])
    v = _layer_norm(jax.nn.gelu(v_ref[...]), g_ref[...], b_ref[...]).astype(BF16)
    row = lax.broadcasted_iota(jnp.int32, (CHUNK, CHUNK), 0)
    col = lax.broadcasted_iota(jnp.int32, (CHUNK, CHUNK), 1)
    tril = row >= col
    bias = bias_ref[...]
    for grp in range(B_GROUPS):
        w = jnp.where(tril, ws_ref[grp], 0.0).astype(BF16)
        cols = slice(grp * B_GROUP_DIM, (grp + 1) * B_GROUP_DIM)
        for c in range(chunks):
            rows = slice(c * CHUNK, (c + 1) * CHUNK)
            mixed = _dot(w, v[rows, cols]) + bias[:, cols]
            o_ref[rows, cols] = (u[rows, cols] * mixed).astype(o_ref.dtype)


def _gmlp(uv, ln_g, ln_b, w_s, b_s, chunks=4):
    n = uv.shape[0]
    rows = chunks * CHUNK
    bias_full = jnp.repeat(b_s.T, B_GROUP_DIM, axis=1)
    return pl.pallas_call(
        functools.partial(_gmlp_kernel, chunks=chunks),
        grid=(n // rows,),
        in_specs=[
            pl.BlockSpec((rows, B_WIDTH), lambda i: (i, 0)),
            pl.BlockSpec((rows, B_WIDTH), lambda i: (i, 1)),
            pl.BlockSpec((1, B_WIDTH), lambda i: (0, 0)),
            pl.BlockSpec((1, B_WIDTH), lambda i: (0, 0)),
            pl.BlockSpec((B_GROUPS, CHUNK, CHUNK), lambda i: (0, 0, 0)),
            pl.BlockSpec((CHUNK, B_WIDTH), lambda i: (0, 0)),
        ],
        out_specs=pl.BlockSpec((rows, B_WIDTH), lambda i: (i, 0)),
        out_shape=jax.ShapeDtypeStruct((n, B_WIDTH), BF16),
        compiler_params=_params("parallel"),
        name="gmlp",
    )(uv, uv, ln_g.reshape(1, B_WIDTH), ln_b.reshape(1, B_WIDTH), w_s, bias_full)


VT_ROWS = HEAD_DIM + 16


def _moba_seq_kernel(q_ref, k_ref, v_ref, o_ref, kmean_sc, qt_sc, vt_sc, selb_sc, st_sc, *, n_blocks):
    blk = MOBA_BLOCK
    n_rows = kmean_sc.shape[0]
    scale = HEAD_DIM ** -0.5

    kmean_sc[...] = jnp.zeros_like(kmean_sc)
    ones_rows = (lax.broadcasted_iota(jnp.int32, (VT_ROWS - HEAD_DIM, blk), 0) == 0).astype(BF16)

    def prep(m, carry):
        rows = pl.ds(pl.multiple_of(m * blk, blk), blk)
        kmean_sc[pl.ds(m, 1), :] = jnp.mean(k_ref[rows, :].astype(F32), axis=0, keepdims=True)
        qt_sc[m] = (jnp.transpose(q_ref[rows, :].astype(F32)) * scale).astype(BF16)
        vt = jnp.transpose(v_ref[rows, :].astype(F32)).astype(BF16)
        for h in range(2):
            vt_sc[m, h] = jnp.concatenate([vt[h * HEAD_DIM:(h + 1) * HEAD_DIM], ones_rows], axis=0)
        return carry

    lax.fori_loop(0, n_blocks, prep, 0, unroll=2)

    chan = lax.broadcasted_iota(jnp.int32, (HEAD_PAIR, blk), 0)
    head_rows = [chan < HEAD_DIM, chan >= HEAD_DIM]
    kmean = kmean_sc[...].astype(BF16)
    cand = lax.broadcasted_iota(jnp.int32, (n_rows, blk), 0)

    def select(n, carry):
        qt = qt_sc[n]
        for h in range(2):
            gate = _dot(kmean, jnp.where(head_rows[h], qt, 0))
            left = jnp.where(cand < n, gate, -jnp.inf)
            chosen = cand == n
            for _ in range(MOBA_TOPK):
                best = jnp.max(left, axis=0, keepdims=True)
                first = jnp.min(jnp.where(left == best, cand, n_rows), axis=0, keepdims=True)
                take = (cand == first) & (best > -jnp.inf)
                chosen = chosen | (take & (best < jnp.inf))
                left = jnp.where(take, -jnp.inf, left)
            selb_sc[n, h] = jnp.where(chosen, 0.0, NEG_BIG).astype(BF16)
        return carry

    lax.fori_loop(0, n_blocks, select, 0, unroll=2)

    key_i = lax.broadcasted_iota(jnp.int32, (blk, blk), 0)
    qry_i = lax.broadcasted_iota(jnp.int32, (blk, blk), 1)
    causal = key_i <= qry_i
    key_lane = lax.broadcasted_iota(jnp.int32, (blk, HEAD_PAIR), 1)
    head_lanes = [key_lane < HEAD_DIM, key_lane >= HEAD_DIM]
    bias_base = [HEAD_DIM, 0]
    pad_rows = jnp.zeros((HEAD_DIM - n_rows, blk), BF16)

    def softmax_stage(m_run, st):
        m_new = jnp.maximum(m_run, jnp.max(st, axis=0, keepdims=True))
        return m_new, jnp.exp(m_run - m_new), jnp.exp(st - m_new).astype(BF16)

    def query_block(n, carry):
        qt = qt_sc[n]
        w_aug = [jnp.concatenate([qt[:HEAD_DIM], selb_sc[n, 0], pad_rows], axis=0),
                 jnp.concatenate([selb_sc[n, 1], pad_rows, qt[HEAD_DIM:]], axis=0)]

        def scores(m):
            k_m = k_ref[pl.ds(pl.multiple_of(m * blk, blk), blk), :]
            out = []
            for h in range(2):
                k_aug = jnp.where(head_lanes[h], k_m, (key_lane == bias_base[h] + m).astype(BF16))
                out.append(_dot(k_aug, w_aug[h]))
            return out

        def body(i, state):
            m_run, acc = state
            st_cur = [st_sc[i % 2, h] for h in range(2)]
            st_next = scores(i + 1)
            out_m, out_acc = [], []
            for h in range(2):
                st_sc[(i + 1) % 2, h] = st_next[h]
                m_new, alpha, p = softmax_stage(m_run[h], st_cur[h])
                out_m.append(m_new)
                out_acc.append(alpha * acc[h] + _dot(vt_sc[i, h], p))
            return out_m, out_acc

        for h, st in enumerate(scores(0)):
            st_sc[0, h] = st
        init = ([jnp.full((1, blk), M_INIT, F32)] * 2, [jnp.zeros((VT_ROWS, blk), F32)] * 2)
        m_run, acc = lax.fori_loop(0, n, body, init)
        outs = []
        for h in range(2):
            _, alpha, p = softmax_stage(m_run[h], jnp.where(causal, st_sc[n % 2, h], NEG_BIG))
            acc_h = alpha * acc[h] + _dot(vt_sc[n, h], p)
            outs.append(acc_h[:HEAD_DIM] / acc_h[HEAD_DIM:HEAD_DIM + 1])
        rows = pl.ds(pl.multiple_of(n * blk, blk), blk)
        o_ref[rows, :] = jnp.transpose(jnp.concatenate(outs, axis=0)).astype(o_ref.dtype)
        return carry

    lax.fori_loop(0, n_blocks, query_block, 0, unroll=8)


def _moba(qkv, batch, seq):
    assert seq % MOBA_BLOCK == 0
    n_blocks = seq // MOBA_BLOCK
    n_rows = -(-n_blocks // 16) * 16
    assert n_rows <= HEAD_DIM
    pairs = C_HEADS // 2
    return pl.pallas_call(
        functools.partial(_moba_seq_kernel, n_blocks=n_blocks),
        grid=(batch, pairs),
        in_specs=[
            pl.BlockSpec((seq, HEAD_PAIR), lambda b, j: (b, j)),
            pl.BlockSpec((seq, HEAD_PAIR), lambda b, j: (b, pairs + j)),
            pl.BlockSpec((seq, HEAD_PAIR), lambda b, j: (b, 2 * pairs + j)),
        ],
        out_specs=pl.BlockSpec((seq, HEAD_PAIR), lambda b, j: (b, j)),
        out_shape=jax.ShapeDtypeStruct((batch * seq, pairs * HEAD_PAIR), BF16),
        scratch_shapes=[
            pltpu.VMEM((n_rows, HEAD_PAIR), F32),
            pltpu.VMEM((n_blocks, HEAD_PAIR, MOBA_BLOCK), BF16),
            pltpu.VMEM((n_blocks, 2, VT_ROWS, MOBA_BLOCK), BF16),
            pltpu.VMEM((n_blocks, 2, n_rows, MOBA_BLOCK), BF16),
            pltpu.VMEM((2, 2, MOBA_BLOCK, MOBA_BLOCK), F32),
        ],
        compiler_params=_params("parallel", "parallel"),
        name="moba",
    )(qkv, qkv, qkv)


def _memattn_kernel(q_ref, k_ref, v_ref, o_ref):
    s = _dot_nt(q_ref[...], k_ref[...]) * (X_HEAD_DIM ** -0.5)
    m = jnp.max(s, axis=-1, keepdims=True)
    p = jnp.exp(s - m)
    l = jnp.sum(p, axis=-1, keepdims=True)
    o_ref[...] = (_dot(p.astype(BF16), v_ref[...]) / l).astype(o_ref.dtype)


def _mem_attention(q, kv, batch, seq, mem_len, q_tile=1024):
    q_tile = min(q_tile, seq)
    tiles = seq // q_tile
    return pl.pallas_call(
        _memattn_kernel,
        grid=(batch, tiles, X_HEADS),
        in_specs=[
            pl.BlockSpec((q_tile, X_HEAD_DIM), lambda b, i, h: (b * tiles + i, h)),
            pl.BlockSpec((mem_len, X_HEAD_DIM), lambda b, i, h: (b, h)),
            pl.BlockSpec((mem_len, X_HEAD_DIM), lambda b, i, h: (b, X_HEADS + h)),
        ],
        out_specs=pl.BlockSpec((q_tile, X_HEAD_DIM), lambda b, i, h: (b * tiles + i, h)),
        out_shape=jax.ShapeDtypeStruct((batch * seq, D_MODEL), BF16),
        compiler_params=_params("parallel", "parallel", "arbitrary"),
        name="mem_attention",
    )(q, kv, kv)


def _prep_ffn(w_in, w_out):
    return w_in.astype(BF16), w_out.astype(BF16)


def _mem_block(x, mem2, w_q, w_kv, w_o, g, b, batch, seq, mem_len):
    q = _project(x, w_q.astype(BF16), BF16)
    kv = _project(mem2, w_kv.astype(BF16), BF16)
    o = _mem_attention(q, kv, batch, seq, mem_len)
    return _outproj_deepnorm([o], [w_o.astype(BF16)], x, g, b)


def kernel(x, mem, l0_ffn1_w_in, l0_ffn1_w_out, l0_ln1_g, l0_ln1_b, l0_mix_w_in, l0_gmlp_ln_g, l0_gmlp_ln_b, l0_gmlp_w_s, l0_gmlp_b_s, l0_mix_w_out, l0_ln2_g, l0_ln2_b, l0_mem_w_q, l0_mem_w_kv, l0_mem_w_o, l0_ln3_g, l0_ln3_b, l0_ffn2_w_in, l0_ffn2_w_out, l0_ln4_g, l0_ln4_b, l1_ffn1_w_in, l1_ffn1_w_out, l1_ln1_g, l1_ln1_b, l1_mix_w_in, l1_mix_w_out, l1_ln2_g, l1_ln2_b, l1_mem_w_q, l1_mem_w_kv, l1_mem_w_o, l1_ln3_g, l1_ln3_b, l1_ffn2_w_in, l1_ffn2_w_out, l1_ln4_g, l1_ln4_b):
    batch, seq, d = x.shape
    mem_len = mem.shape[1]
    h = x.reshape(batch * seq, d)
    mem2 = mem.reshape(batch * mem_len, d)

    h = _ffn_deepnorm(h, *_prep_ffn(l0_ffn1_w_in, l0_ffn1_w_out), l0_ln1_g, l0_ln1_b)
    qkv = _project(h, l0_mix_w_in[:, :A_QKV_WIDTH].astype(BF16), F32)
    uv = _project(h, l0_mix_w_in[:, A_QKV_WIDTH:].astype(BF16), F32)
    a_out = _dilated_attention(qkv, batch, seq)
    b_out = _gmlp(uv, l0_gmlp_ln_g, l0_gmlp_ln_b, l0_gmlp_w_s, l0_gmlp_b_s)
    a_width = a_out.shape[1]
    w_mix_out = [l0_mix_w_out[:a_width].astype(BF16), l0_mix_w_out[a_width:].astype(BF16)]
    h = _outproj_deepnorm([a_out, b_out], w_mix_out, h, l0_ln2_g, l0_ln2_b)
    h = _mem_block(h, mem2, l0_mem_w_q, l0_mem_w_kv, l0_mem_w_o, l0_ln3_g, l0_ln3_b, batch, seq, mem_len)
    h = _ffn_deepnorm(h, *_prep_ffn(l0_ffn2_w_in, l0_ffn2_w_out), l0_ln4_g, l0_ln4_b)

    h = _ffn_deepnorm(h, *_prep_ffn(l1_ffn1_w_in, l1_ffn1_w_out), l1_ln1_g, l1_ln1_b)
    qkv = _project(h, l1_mix_w_in.astype(BF16), BF16)
    o = _moba(qkv, batch, seq)
    h = _outproj_deepnorm([o], [l1_mix_w_out.astype(BF16)], h, l1_ln2_g, l1_ln2_b)
    h = _mem_block(h, mem2, l1_mem_w_q, l1_mem_w_kv, l1_mem_w_o, l1_ln3_g, l1_ln3_b, batch, seq, mem_len)
    h = _ffn_deepnorm(h, *_prep_ffn(l1_ffn2_w_in, l1_ffn2_w_out), l1_ln4_g, l1_ln4_b)
    return h.reshape(batch, seq, d)
```

```python
import functools

import jax
import jax.numpy as jnp
from jax import lax
from jax.experimental import pallas as pl
from jax.experimental.pallas import tpu as pltpu

D_MODEL = 2048
DEPTH = 2
HEAD_DIM = 64
HEAD_PAIR = 2 * HEAD_DIM
DIL_GROUPS = ((128, 1), (512, 4), (2048, 16))
A_HEADS_PER_GROUP = 8
A_HEADS = A_HEADS_PER_GROUP * len(DIL_GROUPS)
BAND_BLOCK = 128
CHUNK = 128
B_GROUPS = 8
B_WIDTH = 1024
B_GROUP_DIM = B_WIDTH // B_GROUPS
C_HEADS = D_MODEL // HEAD_DIM
MOBA_BLOCK = 256
MOBA_TOPK = 3
X_HEADS = 4
X_HEAD_DIM = D_MODEL // X_HEADS
D_FF = 5504
DEEPNORM_ALPHA = (2 * DEPTH) ** 0.25
LN_EPS = 1e-5
A_QKV_WIDTH = 3 * A_HEADS * HEAD_DIM

LANES = 128
FF_TILE = 512
ROW_TILE = 512
VMEM_LIMIT = 56 * 1024 * 1024
NEG_BIG = -1e30
M_INIT = -1e29

BF16 = jnp.bfloat16
F32 = jnp.float32


def _params(*sem):
    return pltpu.CompilerParams(dimension_semantics=sem, vmem_limit_bytes=VMEM_LIMIT)


def _dot(a, b):
    return jnp.dot(a, b, preferred_element_type=F32)


def _dot_nt(a, b):
    return lax.dot_general(a, b, (((1,), (1,)), ((), ())), preferred_element_type=F32)


def _layer_norm(y, g, b):
    mu = jnp.mean(y, axis=-1, keepdims=True)
    yc = y - mu
    var = jnp.mean(yc * yc, axis=-1, keepdims=True)
    return yc * lax.rsqrt(var + LN_EPS) * g + b


FF_FULL = D_FF // FF_TILE
FF_TAIL = D_FF - FF_FULL * FF_TILE
assert FF_TAIL % LANES == 0 and FF_TAIL > 0


def _ff_tile_start(f, base=0):
    assert base % LANES == 0 and FF_TILE % LANES == 0
    return (jnp.minimum(f, FF_FULL - 1) * (FF_TILE // LANES) + base // LANES) * LANES


def _ffn_kernel(x_ref, wg_ref, wu_ref, wo_ref, wg_tail, wu_tail, wo_tail, g_ref, b_ref, o_ref,
                xb_ref, acc_ref, y_ref):
    i = pl.program_id(0)
    f = pl.program_id(1)
    last = f == pl.num_programs(1) - 1
    final_row = i == pl.num_programs(0) - 1

    def contribution(wg, wu, wo):
        xb = xb_ref[...]
        gate = _dot(xb, wg[...])
        up = _dot(xb, wu[...])
        act = (gate * jax.nn.sigmoid(gate)) * up
        return _dot(act.astype(BF16), wo[...])

    def pre_norm():
        return DEEPNORM_ALPHA * x_ref[...] + 0.5 * (acc_ref[...] + contribution(wg_tail, wu_tail, wo_tail))

    @pl.when(jnp.logical_and(i == 0, f == 0))
    def _():
        y_ref[...] = jnp.zeros_like(y_ref)

    @pl.when(f == 0)
    def _():
        xb_ref[...] = x_ref[...].astype(BF16)
        o_ref[...] = _layer_norm(y_ref[...], g_ref[...], b_ref[...])
        acc_ref[...] = contribution(wg_ref, wu_ref, wo_ref)

    @pl.when(jnp.logical_and(f > 0, jnp.logical_not(last)))
    def _():
        acc_ref[...] += contribution(wg_ref, wu_ref, wo_ref)

    @pl.when(jnp.logical_and(last, jnp.logical_not(final_row)))
    def _():
        y_ref[...] = pre_norm()

    @pl.when(jnp.logical_and(last, final_row))
    def _():
        o_ref[...] = _layer_norm(pre_norm(), g_ref[...], b_ref[...])


def _ffn_deepnorm(x, w_in, w_out, g, b):
    n, d = x.shape
    tail_start = FF_FULL * FF_TILE
    n_tiles = n // ROW_TILE
    full = lambda rows, cols: (pl.Element(rows), pl.Element(cols))

    def out_index(i, f):
        own_step = jnp.logical_and(i == n_tiles - 1, f == FF_FULL)
        return jnp.where(own_step, i, jnp.maximum(i - 1, 0)), 0

    return pl.pallas_call(
        _ffn_kernel,
        grid=(n_tiles, FF_FULL + 1),
        in_specs=[
            pl.BlockSpec((ROW_TILE, d), lambda i, f: (i, 0)),
            pl.BlockSpec(full(d, FF_TILE), lambda i, f: (0, _ff_tile_start(f))),
            pl.BlockSpec(full(d, FF_TILE), lambda i, f: (0, _ff_tile_start(f, D_FF))),
            pl.BlockSpec(full(FF_TILE, d), lambda i, f: (_ff_tile_start(f), 0)),
            pl.BlockSpec(full(d, FF_TAIL), lambda i, f: (0, tail_start)),
            pl.BlockSpec(full(d, FF_TAIL), lambda i, f: (0, D_FF + tail_start)),
            pl.BlockSpec(full(FF_TAIL, d), lambda i, f: (tail_start, 0)),
            pl.BlockSpec((1, d), lambda i, f: (0, 0)),
            pl.BlockSpec((1, d), lambda i, f: (0, 0)),
        ],
        out_specs=pl.BlockSpec((ROW_TILE, d), out_index),
        out_shape=jax.ShapeDtypeStruct((n, d), F32),
        scratch_shapes=[
            pltpu.VMEM((ROW_TILE, d), BF16),
            pltpu.VMEM((ROW_TILE, d), F32),
            pltpu.VMEM((ROW_TILE, d), F32),
        ],
        compiler_params=_params("arbitrary", "arbitrary"),
        name="ffn_deepnorm",
    )(x, w_in, w_in, w_out, w_in, w_in, w_out, g.reshape(1, d), b.reshape(1, d))


def _proj_kernel(x_ref, w_ref, o_ref):
    o_ref[...] = _dot(x_ref[...].astype(BF16), w_ref[...]).astype(o_ref.dtype)


PROJ_MAX_COLS = 2048


def _project(x, w, out_dtype, row_tile=1024):
    n, k = x.shape
    m = w.shape[1]
    row_tile = min(row_tile, n)
    col_tile = max(c for c in range(LANES, PROJ_MAX_COLS + 1, LANES) if m % c == 0)
    return pl.pallas_call(
        _proj_kernel,
        grid=(n // row_tile, m // col_tile),
        in_specs=[
            pl.BlockSpec((row_tile, k), lambda i, j: (i, 0)),
            pl.BlockSpec((k, col_tile), lambda i, j: (0, j)),
        ],
        out_specs=pl.BlockSpec((row_tile, col_tile), lambda i, j: (i, j)),
        out_shape=jax.ShapeDtypeStruct((n, m), out_dtype),
        compiler_params=_params("parallel", "arbitrary"),
        name="project",
    )(x, w)


OUTPROJ_CHUNK = 256


def _outproj_kernel(*refs, n_in):
    a_refs, w_refs = refs[:n_in], refs[n_in:2 * n_in]
    x_ref, g_ref, b_ref, o_ref = refs[2 * n_in:]
    for c in range(ROW_TILE // OUTPROJ_CHUNK):
        rows = slice(c * OUTPROJ_CHUNK, (c + 1) * OUTPROJ_CHUNK)
        fx = _dot(a_refs[0][rows, :], w_refs[0][...])
        for a_ref, w_ref in zip(a_refs[1:], w_refs[1:]):
            fx = fx + _dot(a_ref[rows, :], w_ref[...])
        y = DEEPNORM_ALPHA * x_ref[rows, :] + fx
        o_ref[rows, :] = _layer_norm(y, g_ref[...], b_ref[...])


def _outproj_deepnorm(a_list, w_list, x, g, b):
    n, d = x.shape
    n_in = len(a_list)
    in_specs = [pl.BlockSpec((ROW_TILE, a.shape[1]), lambda i: (i, 0)) for a in a_list]
    in_specs += [pl.BlockSpec(w.shape, lambda i: (0, 0)) for w in w_list]
    in_specs += [
        pl.BlockSpec((ROW_TILE, d), lambda i: (i, 0)),
        pl.BlockSpec((1, d), lambda i: (0, 0)),
        pl.BlockSpec((1, d), lambda i: (0, 0)),
    ]
    return pl.pallas_call(
        functools.partial(_outproj_kernel, n_in=n_in),
        grid=(n // ROW_TILE,),
        in_specs=in_specs,
        out_specs=pl.BlockSpec((ROW_TILE, d), lambda i: (i, 0)),
        out_shape=jax.ShapeDtypeStruct((n, d), F32),
        compiler_params=_params("parallel"),
        name="outproj_deepnorm",
    )(*a_list, *w_list, x, g.reshape(1, d), b.reshape(1, d))


def _dilated_kernel(*refs, seq):
    n_g = len(DIL_GROUPS)
    qkv_refs = refs[:3 * n_g]
    o_ref = refs[3 * n_g]
    o_sc, lse_sc, bias_sc = refs[3 * n_g + 1:]

    bb = BAND_BLOCK
    qi = lax.broadcasted_iota(jnp.int32, (bb, 2 * bb), 0)
    ki = lax.broadcasted_iota(jnp.int32, (bb, 2 * bb), 1)
    for first, delta in ((0, bb), (1, 0)):
        off = qi + delta - ki
        bias_sc[first] = jnp.where((off >= 0) & (off <= bb), 0.0, -jnp.inf).astype(F32)

    lane = lax.broadcasted_iota(jnp.int32, (bb, HEAD_PAIR), 1)
    head0 = lane < HEAD_DIM
    scale = HEAD_DIM ** -0.5

    for g, (window, dil) in enumerate(DIL_GROUPS):
        assert window // dil == bb
        q_ref, k_ref, v_ref = qkv_refs[3 * g:3 * g + 3]
        n_blk = seq // (dil * bb)

        def body(it, carry, q_ref=q_ref, k_ref=k_ref, v_ref=v_ref, dil=dil, g=g):
            blk = it // dil
            phase = it - blk * dil
            is_first = jnp.where(blk == 0, 1, 0)
            q_start = blk * (bb * dil) + phase
            k_start = jnp.maximum(blk - 1, 0) * (bb * dil) + phase
            if dil == 1:
                q_rows, k_rows = pl.ds(q_start, bb), pl.ds(k_start, 2 * bb)
            else:
                q_rows = pl.ds(q_start, bb, stride=dil)
                k_rows = pl.ds(k_start, 2 * bb, stride=dil)
            q = q_ref[q_rows, :] * scale
            k = k_ref[k_rows, :].astype(BF16)
            v = v_ref[k_rows, :].astype(BF16)
            bias = bias_sc[is_first]
            outs, lses = [], []
            for h0 in (True, False):
                qh = jnp.where(head0 if h0 else ~head0, q, 0.0).astype(BF16)
                s = _dot_nt(qh, k) + bias
                m = jnp.max(s, axis=-1, keepdims=True)
                p = jnp.exp(s - m)
                l = jnp.sum(p, axis=-1, keepdims=True)
                outs.append(_dot(p.astype(BF16), v) / l)
                lses.append(m + jnp.log(l))
            o_sc[g, q_rows, :] = jnp.where(head0, outs[0], outs[1])
            lse_sc[g, q_rows, :] = jnp.where(head0, lses[0], lses[1])
            return carry

        lax.fori_loop(0, n_blk * dil, body, 0, unroll=8)

    lse = [lse_sc[g] for g in range(n_g)]
    top = functools.reduce(jnp.maximum, lse)
    w = [jnp.exp(x - top) for x in lse]
    den = functools.reduce(jnp.add, w)
    acc = functools.reduce(jnp.add, [w[g] * o_sc[g] for g in range(n_g)])
    o_ref[...] = (acc / den).astype(o_ref.dtype)


def _dilated_attention(qkv, batch, seq):
    n_g = len(DIL_GROUPS)
    pairs = A_HEADS_PER_GROUP // 2
    blocks_per_part = A_HEADS * HEAD_DIM // HEAD_PAIR
    in_specs = []
    for g in range(n_g):
        for part in range(3):
            base = part * blocks_per_part + g * pairs
            in_specs.append(pl.BlockSpec((seq, HEAD_PAIR), lambda b, j, base=base: (b, base + j)))
    return pl.pallas_call(
        functools.partial(_dilated_kernel, seq=seq),
        grid=(batch, pairs),
        in_specs=in_specs,
        out_specs=pl.BlockSpec((seq, HEAD_PAIR), lambda b, j: (b, j)),
        out_shape=jax.ShapeDtypeStruct((batch * seq, pairs * HEAD_PAIR), BF16),
        scratch_shapes=[
            pltpu.VMEM((n_g, seq, HEAD_PAIR), F32),
            pltpu.VMEM((n_g, seq, HEAD_PAIR), F32),
            pltpu.VMEM((2, BAND_BLOCK, 2 * BAND_BLOCK), F32),
        ],
        compiler_params=_params("parallel", "parallel"),
        name="dilated_attention",
    )(*([qkv] * (3 * n_g)))


def _gmlp_kernel(u_ref, v_ref, g_ref, b_ref, ws_ref, bias_ref, o_ref, *, chunks):
    u = jax.nn.gelu(u_ref[...])
    v = _layer_norm(jax.nn.gelu(v_ref[...]), g_ref[...], b_ref[...]).astype(BF16)
    row = lax.broadcasted_iota(jnp.int32, (CHUNK, CHUNK), 0)
    col = lax.broadcasted_iota(jnp.int32, (CHUNK, CHUNK), 1)
    tril = row >= col
    bias = bias_ref[...]
    for grp in range(B_GROUPS):
        w = jnp.where(tril, ws_ref[grp], 0.0).astype(BF16)
        cols = slice(grp * B_GROUP_DIM, (grp + 1) * B_GROUP_DIM)
        for c in range(chunks):
            rows = slice(c * CHUNK, (c + 1) * CHUNK)
            mixed = _dot(w, v[rows, cols]) + bias[:, cols]
            o_ref[rows, cols] = (u[rows, cols] * mixed).astype(o_ref.dtype)


def _gmlp(uv, ln_g, ln_b, w_s, b_s, chunks=4):
    n = uv.shape[0]
    rows = chunks * CHUNK
    bias_full = jnp.repeat(b_s.T, B_GROUP_DIM, axis=1)
    return pl.pallas_call(
        functools.partial(_gmlp_kernel, chunks=chunks),
        grid=(n // rows,),
        in_specs=[
            pl.BlockSpec((rows, B_WIDTH), lambda i: (i, 0)),
            pl.BlockSpec((rows, B_WIDTH), lambda i: (i, 1)),
            pl.BlockSpec((1, B_WIDTH), lambda i: (0, 0)),
            pl.BlockSpec((1, B_WIDTH), lambda i: (0, 0)),
            pl.BlockSpec((B_GROUPS, CHUNK, CHUNK), lambda i: (0, 0, 0)),
            pl.BlockSpec((CHUNK, B_WIDTH), lambda i: (0, 0)),
        ],
        out_specs=pl.BlockSpec((rows, B_WIDTH), lambda i: (i, 0)),
        out_shape=jax.ShapeDtypeStruct((n, B_WIDTH), BF16),
        compiler_params=_params("parallel"),
        name="gmlp",
    )(uv, uv, ln_g.reshape(1, B_WIDTH), ln_b.reshape(1, B_WIDTH), w_s, bias_full)


KEY_LOOP_UNROLL = 16
VT_ROWS = HEAD_DIM + 16


def _moba_seq_kernel(q_ref, k_ref, v_ref, o_ref, kmean_sc, qt_sc, vt_sc, selb_sc, st_sc, *, n_blocks):
    blk = MOBA_BLOCK
    n_rows = kmean_sc.shape[0]
    scale = HEAD_DIM ** -0.5

    kmean_sc[...] = jnp.zeros_like(kmean_sc)
    ones_rows = (lax.broadcasted_iota(jnp.int32, (VT_ROWS - HEAD_DIM, blk), 0) == 0).astype(BF16)
    key_lane = lax.broadcasted_iota(jnp.int32, (blk, HEAD_PAIR), 1)
    head_lanes = [key_lane < HEAD_DIM, key_lane >= HEAD_DIM]
    bias_base = [HEAD_DIM, 0]

    def prep(m, carry):
        rows = pl.ds(pl.multiple_of(m * blk, blk), blk)
        kmean_sc[pl.ds(m, 1), :] = jnp.mean(k_ref[rows, :].astype(F32), axis=0, keepdims=True)
        qt_sc[m] = (jnp.transpose(q_ref[rows, :].astype(F32)) * scale).astype(BF16)
        vt = jnp.transpose(v_ref[rows, :].astype(F32)).astype(BF16)
        for h in range(2):
            vt_sc[m, h] = jnp.concatenate([vt[h * HEAD_DIM:(h + 1) * HEAD_DIM], ones_rows], axis=0)
        return carry

    lax.fori_loop(0, n_blocks, prep, 0, unroll=4)

    chan = lax.broadcasted_iota(jnp.int32, (HEAD_PAIR, blk), 0)
    head_rows = [chan < HEAD_DIM, chan >= HEAD_DIM]
    kmean = kmean_sc[...].astype(BF16)
    cand = lax.broadcasted_iota(jnp.int32, (n_rows, blk), 0)

    def select(n, carry):
        qt = qt_sc[n]
        for h in range(2):
            gate = _dot(kmean, jnp.where(head_rows[h], qt, 0))
            left = jnp.where(cand < n, gate, -jnp.inf)
            chosen = cand == n
            for _ in range(MOBA_TOPK):
                best = jnp.max(left, axis=0, keepdims=True)
                first = jnp.min(jnp.where(left == best, cand, n_rows), axis=0, keepdims=True)
                take = (cand == first) & (best > -jnp.inf)
                chosen = chosen | (take & (best < jnp.inf))
                left = jnp.where(take, -jnp.inf, left)
            selb_sc[n, h] = jnp.where(chosen, 0.0, NEG_BIG).astype(BF16)
        return carry

    lax.fori_loop(0, n_blocks, select, 0, unroll=8)

    key_i = lax.broadcasted_iota(jnp.int32, (blk, blk), 0)
    qry_i = lax.broadcasted_iota(jnp.int32, (blk, blk), 1)
    causal = key_i <= qry_i
    pad_rows = jnp.zeros((HEAD_DIM - n_rows, blk), BF16)

    def block_rows(m):
        start = m * blk
        return pl.ds(start if isinstance(m, int) else pl.multiple_of(start, blk), blk)

    def softmax_stage(m_run, st):
        m_new = jnp.maximum(m_run, jnp.max(st, axis=0, keepdims=True))
        return m_new, jnp.exp(m_run - m_new), jnp.exp(st - m_new).astype(BF16)

    def query_block(n, carry):
        qt = qt_sc[n]
        w_aug = [jnp.concatenate([qt[:HEAD_DIM], selb_sc[n, 0], pad_rows], axis=0),
                 jnp.concatenate([selb_sc[n, 1], pad_rows, qt[HEAD_DIM:]], axis=0)]

        def scores(m):
            k_m = k_ref[block_rows(m), :]
            out = []
            for h in range(2):
                k_aug = jnp.where(head_lanes[h], k_m, (key_lane == bias_base[h] + m).astype(BF16))
                out.append(_dot(k_aug, w_aug[h]))
            return out

        def body(i, state):
            m_run, acc = state
            st_cur = [st_sc[i % 2, h] for h in range(2)]
            st_next = scores(i + 1)
            out_m, out_acc = [], []
            for h in range(2):
                st_sc[(i + 1) % 2, h] = st_next[h]
                m_new, alpha, p = softmax_stage(m_run[h], st_cur[h])
                out_m.append(m_new)
                out_acc.append(alpha * acc[h] + _dot(vt_sc[i, h], p))
            return out_m, out_acc

        for h, st in enumerate(scores(0)):
            st_sc[0, h] = st
        init = ([jnp.full((1, blk), M_INIT, F32)] * 2, [jnp.zeros((VT_ROWS, blk), F32)] * 2)
        m_run, acc = lax.fori_loop(0, n, body, init, unroll=max(1, min(KEY_LOOP_UNROLL, n)))
        outs = []
        for h in range(2):
            _, alpha, p = softmax_stage(m_run[h], jnp.where(causal, st_sc[n % 2, h], NEG_BIG))
            acc_h = alpha * acc[h] + _dot(vt_sc[n, h], p)
            outs.append(acc_h[:HEAD_DIM] / acc_h[HEAD_DIM:HEAD_DIM + 1])
        o_ref[block_rows(n), :] = jnp.transpose(jnp.concatenate(outs, axis=0)).astype(o_ref.dtype)
        return carry

    for n in range(n_blocks):
        query_block(n, 0)


def _moba(qkv, batch, seq):
    assert seq % MOBA_BLOCK == 0
    n_blocks = seq // MOBA_BLOCK
    n_rows = -(-n_blocks // 16) * 16
    assert n_rows <= HEAD_DIM
    pairs = C_HEADS // 2
    return pl.pallas_call(
        functools.partial(_moba_seq_kernel, n_blocks=n_blocks),
        grid=(batch, pairs),
        in_specs=[
            pl.BlockSpec((seq, HEAD_PAIR), lambda b, j: (b, j)),
            pl.BlockSpec((seq, HEAD_PAIR), lambda b, j: (b, pairs + j)),
            pl.BlockSpec((seq, HEAD_PAIR), lambda b, j: (b, 2 * pairs + j)),
        ],
        out_specs=pl.BlockSpec((seq, HEAD_PAIR), lambda b, j: (b, j)),
        out_shape=jax.ShapeDtypeStruct((batch * seq, pairs * HEAD_PAIR), BF16),
        scratch_shapes=[
            pltpu.VMEM((n_rows, HEAD_PAIR), F32),
            pltpu.VMEM((n_blocks, HEAD_PAIR, MOBA_BLOCK), BF16),
            pltpu.VMEM((n_blocks, 2, VT_ROWS, MOBA_BLOCK), BF16),
            pltpu.VMEM((n_blocks, 2, n_rows, MOBA_BLOCK), BF16),
            pltpu.VMEM((2, 2, MOBA_BLOCK, MOBA_BLOCK), F32),
        ],
        compiler_params=_params("parallel", "parallel"),
        name="moba",
    )(qkv, qkv, qkv)


def _memattn_kernel(q_ref, k_ref, v_ref, o_ref):
    s = _dot_nt(q_ref[...], k_ref[...]) * (X_HEAD_DIM ** -0.5)
    m = jnp.max(s, axis=-1, keepdims=True)
    p = jnp.exp(s - m)
    l = jnp.sum(p, axis=-1, keepdims=True)
    o_ref[...] = (_dot(p.astype(BF16), v_ref[...]) / l).astype(o_ref.dtype)


def _mem_attention(q, kv, batch, seq, mem_len, q_tile=1024):
    q_tile = min(q_tile, seq)
    tiles = seq // q_tile
    return pl.pallas_call(
        _memattn_kernel,
        grid=(batch, tiles, X_HEADS),
        in_specs=[
            pl.BlockSpec((q_tile, X_HEAD_DIM), lambda b, i, h: (b * tiles + i, h)),
            pl.BlockSpec((mem_len, X_HEAD_DIM), lambda b, i, h: (b, h)),
            pl.BlockSpec((mem_len, X_HEAD_DIM), lambda b, i, h: (b, X_HEADS + h)),
        ],
        out_specs=pl.BlockSpec((q_tile, X_HEAD_DIM), lambda b, i, h: (b * tiles + i, h)),
        out_shape=jax.ShapeDtypeStruct((batch * seq, D_MODEL), BF16),
        compiler_params=_params("parallel", "parallel", "arbitrary"),
        name="mem_attention",
    )(q, kv, kv)


def _prep_ffn(w_in, w_out):
    return w_in.astype(BF16), w_out.astype(BF16)


def _mem_block(x, mem2, w_q, w_kv, w_o, g, b, batch, seq, mem_len):
    q = _project(x, w_q.astype(BF16), BF16)
    kv = _project(mem2, w_kv.astype(BF16), BF16)
    o = _mem_attention(q, kv, batch, seq, mem_len)
    return _outproj_deepnorm([o], [w_o.astype(BF16)], x, g, b)


def kernel(x, mem, l0_ffn1_w_in, l0_ffn1_w_out, l0_ln1_g, l0_ln1_b, l0_mix_w_in, l0_gmlp_ln_g, l0_gmlp_ln_b, l0_gmlp_w_s, l0_gmlp_b_s, l0_mix_w_out, l0_ln2_g, l0_ln2_b, l0_mem_w_q, l0_mem_w_kv, l0_mem_w_o, l0_ln3_g, l0_ln3_b, l0_ffn2_w_in, l0_ffn2_w_out, l0_ln4_g, l0_ln4_b, l1_ffn1_w_in, l1_ffn1_w_out, l1_ln1_g, l1_ln1_b, l1_mix_w_in, l1_mix_w_out, l1_ln2_g, l1_ln2_b, l1_mem_w_q, l1_mem_w_kv, l1_mem_w_o, l1_ln3_g, l1_ln3_b, l1_ffn2_w_in, l1_ffn2_w_out, l1_ln4_g, l1_ln4_b):
    batch, seq, d = x.shape
    mem_len = mem.shape[1]
    h = x.reshape(batch * seq, d)
    mem2 = mem.reshape(batch * mem_len, d)

    h = _ffn_deepnorm(h, *_prep_ffn(l0_ffn1_w_in, l0_ffn1_w_out), l0_ln1_g, l0_ln1_b)
    qkv = _project(h, l0_mix_w_in[:, :A_QKV_WIDTH].astype(BF16), F32)
    uv = _project(h, l0_mix_w_in[:, A_QKV_WIDTH:].astype(BF16), F32)
    a_out = _dilated_attention(qkv, batch, seq)
    b_out = _gmlp(uv, l0_gmlp_ln_g, l0_gmlp_ln_b, l0_gmlp_w_s, l0_gmlp_b_s)
    a_width = a_out.shape[1]
    w_mix_out = [l0_mix_w_out[:a_width].astype(BF16), l0_mix_w_out[a_width:].astype(BF16)]
    h = _outproj_deepnorm([a_out, b_out], w_mix_out, h, l0_ln2_g, l0_ln2_b)
    h = _mem_block(h, mem2, l0_mem_w_q, l0_mem_w_kv, l0_mem_w_o, l0_ln3_g, l0_ln3_b, batch, seq, mem_len)
    h = _ffn_deepnorm(h, *_prep_ffn(l0_ffn2_w_in, l0_ffn2_w_out), l0_ln4_g, l0_ln4_b)

    h = _ffn_deepnorm(h, *_prep_ffn(l1_ffn1_w_in, l1_ffn1_w_out), l1_ln1_g, l1_ln1_b)
    qkv = _project(h, l1_mix_w_in.astype(BF16), BF16)
    o = _moba(qkv, batch, seq)
    h = _outproj_deepnorm([o], [l1_mix_w_out.astype(BF16)], h, l1_ln2_g, l1_ln2_b)
    h = _mem_block(h, mem2, l1_mem_w_q, l1_mem_w_kv, l1_mem_w_o, l1_ln3_g, l1_ln3_b, batch, seq, mem_len)
    h = _ffn_deepnorm(h, *_prep_ffn(l1_ffn2_w_in, l1_ffn2_w_out), l1_ln4_g, l1_ln4_b)
    return h.reshape(batch, seq, d)
```

```python
import functools

import jax
import jax.numpy as jnp
from jax import lax
from jax.experimental import pallas as pl
from jax.experimental.pallas import tpu as pltpu

D_MODEL = 2048
DEPTH = 2
HEAD_DIM = 64
HEAD_PAIR = 2 * HEAD_DIM
DIL_GROUPS = ((128, 1), (512, 4), (2048, 16))
A_HEADS_PER_GROUP = 8
A_HEADS = A_HEADS_PER_GROUP * len(DIL_GROUPS)
BAND_BLOCK = 128
CHUNK = 128
B_GROUPS = 8
B_WIDTH = 1024
B_GROUP_DIM = B_WIDTH // B_GROUPS
C_HEADS = D_MODEL // HEAD_DIM
MOBA_BLOCK = 256
MOBA_TOPK = 3
X_HEADS = 4
X_HEAD_DIM = D_MODEL // X_HEADS
D_FF = 5504
DEEPNORM_ALPHA = (2 * DEPTH) ** 0.25
LN_EPS = 1e-5
A_QKV_WIDTH = 3 * A_HEADS * HEAD_DIM

LANES = 128
FF_TILE = 512
ROW_TILE = 512
VMEM_LIMIT = 56 * 1024 * 1024
NEG_BIG = -1e30
M_INIT = -1e29

BF16 = jnp.bfloat16
F32 = jnp.float32


def _params(*sem):
    return pltpu.CompilerParams(dimension_semantics=sem, vmem_limit_bytes=VMEM_LIMIT)


def _dot(a, b):
    return jnp.dot(a, b, preferred_element_type=F32)


def _dot_nt(a, b):
    return lax.dot_general(a, b, (((1,), (1,)), ((), ())), preferred_element_type=F32)


def _layer_norm(y, g, b):
    mu = jnp.mean(y, axis=-1, keepdims=True)
    yc = y - mu
    var = jnp.mean(yc * yc, axis=-1, keepdims=True)
    return yc * lax.rsqrt(var + LN_EPS) * g + b


FF_FULL = D_FF // FF_TILE
FF_TAIL = D_FF - FF_FULL * FF_TILE
assert FF_TAIL % LANES == 0 and FF_TAIL > 0


def _ff_tile_start(f, base=0):
    assert base % LANES == 0 and FF_TILE % LANES == 0
    return (jnp.minimum(f, FF_FULL - 1) * (FF_TILE // LANES) + base // LANES) * LANES


def _ffn_kernel(x_ref, wg_ref, wu_ref, wo_ref, wg_tail, wu_tail, wo_tail, g_ref, b_ref, o_ref,
                xb_ref, acc_ref, y_ref):
    i = pl.program_id(0)
    f = pl.program_id(1)
    last = f == pl.num_programs(1) - 1
    final_row = i == pl.num_programs(0) - 1

    def contribution(wg, wu, wo):
        xb = xb_ref[...]
        gate = _dot(xb, wg[...])
        up = _dot(xb, wu[...])
        act = (gate * jax.nn.sigmoid(gate)) * up
        return _dot(act.astype(BF16), wo[...])

    def pre_norm():
        return DEEPNORM_ALPHA * x_ref[...] + 0.5 * (acc_ref[...] + contribution(wg_tail, wu_tail, wo_tail))

    @pl.when(jnp.logical_and(i == 0, f == 0))
    def _():
        y_ref[...] = jnp.zeros_like(y_ref)

    @pl.when(f == 0)
    def _():
        xb_ref[...] = x_ref[...].astype(BF16)
        o_ref[...] = _layer_norm(y_ref[...], g_ref[...], b_ref[...])
        acc_ref[...] = contribution(wg_ref, wu_ref, wo_ref)

    @pl.when(jnp.logical_and(f > 0, jnp.logical_not(last)))
    def _():
        acc_ref[...] += contribution(wg_ref, wu_ref, wo_ref)

    @pl.when(jnp.logical_and(last, jnp.logical_not(final_row)))
    def _():
        y_ref[...] = pre_norm()

    @pl.when(jnp.logical_and(last, final_row))
    def _():
        o_ref[...] = _layer_norm(pre_norm(), g_ref[...], b_ref[...])


def _ffn_deepnorm(x, w_in, w_out, g, b):
    n, d = x.shape
    tail_start = FF_FULL * FF_TILE
    n_tiles = n // ROW_TILE
    full = lambda rows, cols: (pl.Element(rows), pl.Element(cols))

    def out_index(i, f):
        own_step = jnp.logical_and(i == n_tiles - 1, f == FF_FULL)
        return jnp.where(own_step, i, jnp.maximum(i - 1, 0)), 0

    return pl.pallas_call(
        _ffn_kernel,
        grid=(n_tiles, FF_FULL + 1),
        in_specs=[
            pl.BlockSpec((ROW_TILE, d), lambda i, f: (i, 0)),
            pl.BlockSpec(full(d, FF_TILE), lambda i, f: (0, _ff_tile_start(f))),
            pl.BlockSpec(full(d, FF_TILE), lambda i, f: (0, _ff_tile_start(f, D_FF))),
            pl.BlockSpec(full(FF_TILE, d), lambda i, f: (_ff_tile_start(f), 0)),
            pl.BlockSpec(full(d, FF_TAIL), lambda i, f: (0, tail_start)),
            pl.BlockSpec(full(d, FF_TAIL), lambda i, f: (0, D_FF + tail_start)),
            pl.BlockSpec(full(FF_TAIL, d), lambda i, f: (tail_start, 0)),
            pl.BlockSpec((1, d), lambda i, f: (0, 0)),
            pl.BlockSpec((1, d), lambda i, f: (0, 0)),
        ],
        out_specs=pl.BlockSpec((ROW_TILE, d), out_index),
        out_shape=jax.ShapeDtypeStruct((n, d), F32),
        scratch_shapes=[
            pltpu.VMEM((ROW_TILE, d), BF16),
            pltpu.VMEM((ROW_TILE, d), F32),
            pltpu.VMEM((ROW_TILE, d), F32),
        ],
        compiler_params=_params("arbitrary", "arbitrary"),
        name="ffn_deepnorm",
    )(x, w_in, w_in, w_out, w_in, w_in, w_out, g.reshape(1, d), b.reshape(1, d))


def _proj_kernel(x_ref, w_ref, o_ref):
    o_ref[...] = _dot(x_ref[...].astype(BF16), w_ref[...]).astype(o_ref.dtype)


PROJ_MAX_COLS = 2048


def _project(x, w, out_dtype, row_tile=1024):
    n, k = x.shape
    m = w.shape[1]
    row_tile = min(row_tile, n)
    col_tile = max(c for c in range(LANES, PROJ_MAX_COLS + 1, LANES) if m % c == 0)
    return pl.pallas_call(
        _proj_kernel,
        grid=(n // row_tile, m // col_tile),
        in_specs=[
            pl.BlockSpec((row_tile, k), lambda i, j: (i, 0)),
            pl.BlockSpec((k, col_tile), lambda i, j: (0, j)),
        ],
        out_specs=pl.BlockSpec((row_tile, col_tile), lambda i, j: (i, j)),
        out_shape=jax.ShapeDtypeStruct((n, m), out_dtype),
        compiler_params=_params("parallel", "arbitrary"),
        name="project",
    )(x, w)


OUTPROJ_CHUNK = 256


def _outproj_kernel(*refs, n_in):
    a_refs, w_refs = refs[:n_in], refs[n_in:2 * n_in]
    x_ref, g_ref, b_ref, o_ref = refs[2 * n_in:]
    for c in range(ROW_TILE // OUTPROJ_CHUNK):
        rows = slice(c * OUTPROJ_CHUNK, (c + 1) * OUTPROJ_CHUNK)
        fx = _dot(a_refs[0][rows, :], w_refs[0][...])
        for a_ref, w_ref in zip(a_refs[1:], w_refs[1:]):
            fx = fx + _dot(a_ref[rows, :], w_ref[...])
        y = DEEPNORM_ALPHA * x_ref[rows, :] + fx
        o_ref[rows, :] = _layer_norm(y, g_ref[...], b_ref[...])


def _outproj_deepnorm(a_list, w_list, x, g, b):
    n, d = x.shape
    n_in = len(a_list)
    in_specs = [pl.BlockSpec((ROW_TILE, a.shape[1]), lambda i: (i, 0)) for a in a_list]
    in_specs += [pl.BlockSpec(w.shape, lambda i: (0, 0)) for w in w_list]
    in_specs += [
        pl.BlockSpec((ROW_TILE, d), lambda i: (i, 0)),
        pl.BlockSpec((1, d), lambda i: (0, 0)),
        pl.BlockSpec((1, d), lambda i: (0, 0)),
    ]
    return pl.pallas_call(
        functools.partial(_outproj_kernel, n_in=n_in),
        grid=(n // ROW_TILE,),
        in_specs=in_specs,
        out_specs=pl.BlockSpec((ROW_TILE, d), lambda i: (i, 0)),
        out_shape=jax.ShapeDtypeStruct((n, d), F32),
        compiler_params=_params("parallel"),
        name="outproj_deepnorm",
    )(*a_list, *w_list, x, g.reshape(1, d), b.reshape(1, d))


def _dilated_kernel(*refs, seq):
    n_g = len(DIL_GROUPS)
    qkv_refs = refs[:3 * n_g]
    o_ref = refs[3 * n_g]
    o_sc, lse_sc, bias_sc = refs[3 * n_g + 1:]

    bb = BAND_BLOCK
    qi = lax.broadcasted_iota(jnp.int32, (bb, 2 * bb), 0)
    ki = lax.broadcasted_iota(jnp.int32, (bb, 2 * bb), 1)
    for first, delta in ((0, bb), (1, 0)):
        off = qi + delta - ki
        bias_sc[first] = jnp.where((off >= 0) & (off <= bb), 0.0, -jnp.inf).astype(F32)

    lane = lax.broadcasted_iota(jnp.int32, (bb, HEAD_PAIR), 1)
    head0 = lane < HEAD_DIM
    scale = HEAD_DIM ** -0.5

    for g, (window, dil) in enumerate(DIL_GROUPS):
        assert window // dil == bb
        q_ref, k_ref, v_ref = qkv_refs[3 * g:3 * g + 3]
        n_blk = seq // (dil * bb)

        def body(it, carry, q_ref=q_ref, k_ref=k_ref, v_ref=v_ref, dil=dil, g=g):
            blk = it // dil
            phase = it - blk * dil
            is_first = jnp.where(blk == 0, 1, 0)
            q_start = blk * (bb * dil) + phase
            k_start = jnp.maximum(blk - 1, 0) * (bb * dil) + phase
            if dil == 1:
                q_rows, k_rows = pl.ds(q_start, bb), pl.ds(k_start, 2 * bb)
            else:
                q_rows = pl.ds(q_start, bb, stride=dil)
                k_rows = pl.ds(k_start, 2 * bb, stride=dil)
            q = q_ref[q_rows, :] * scale
            k = k_ref[k_rows, :].astype(BF16)
            v = v_ref[k_rows, :].astype(BF16)
            bias = bias_sc[is_first]
            outs, lses = [], []
            for h0 in (True, False):
                qh = jnp.where(head0 if h0 else ~head0, q, 0.0).astype(BF16)
                s = _dot_nt(qh, k) + bias
                m = jnp.max(s, axis=-1, keepdims=True)
                p = jnp.exp(s - m)
                l = jnp.sum(p, axis=-1, keepdims=True)
                outs.append(_dot(p.astype(BF16), v) / l)
                lses.append(m + jnp.log(l))
            o_sc[g, q_rows, :] = jnp.where(head0, outs[0], outs[1])
            lse_sc[g, q_rows, :] = jnp.where(head0, lses[0], lses[1])
            return carry

        lax.fori_loop(0, n_blk * dil, body, 0, unroll=32)

    lse = [lse_sc[g] for g in range(n_g)]
    top = functools.reduce(jnp.maximum, lse)
    w = [jnp.exp(x - top) for x in lse]
    den = functools.reduce(jnp.add, w)
    acc = functools.reduce(jnp.add, [w[g] * o_sc[g] for g in range(n_g)])
    o_ref[...] = (acc / den).astype(o_ref.dtype)


def _dilated_attention(qkv, batch, seq):
    n_g = len(DIL_GROUPS)
    pairs = A_HEADS_PER_GROUP // 2
    blocks_per_part = A_HEADS * HEAD_DIM // HEAD_PAIR
    in_specs = []
    for g in range(n_g):
        for part in range(3):
            base = part * blocks_per_part + g * pairs
            in_specs.append(pl.BlockSpec((seq, HEAD_PAIR), lambda b, j, base=base: (b, base + j)))
    return pl.pallas_call(
        functools.partial(_dilated_kernel, seq=seq),
        grid=(batch, pairs),
        in_specs=in_specs,
        out_specs=pl.BlockSpec((seq, HEAD_PAIR), lambda b, j: (b, j)),
        out_shape=jax.ShapeDtypeStruct((batch * seq, pairs * HEAD_PAIR), BF16),
        scratch_shapes=[
            pltpu.VMEM((n_g, seq, HEAD_PAIR), F32),
            pltpu.VMEM((n_g, seq, HEAD_PAIR), F32),
            pltpu.VMEM((2, BAND_BLOCK, 2 * BAND_BLOCK), F32),
        ],
        compiler_params=_params("parallel", "parallel"),
        name="dilated_attention",
    )(*([qkv] * (3 * n_g)))


def _gmlp_kernel(u_ref, v_ref, g_ref, b_ref, ws_ref, bias_ref, o_ref, *, chunks):
    u = jax.nn.gelu(u_ref[...])
    v = _layer_norm(jax.nn.gelu(v_ref[...]), g_ref[...], b_ref[...]).astype(BF16)
    row = lax.broadcasted_iota(jnp.int32, (CHUNK, CHUNK), 0)
    col = lax.broadcasted_iota(jnp.int32, (CHUNK, CHUNK), 1)
    tril = row >= col
    bias = bias_ref[...]
    for grp in range(B_GROUPS):
        w = jnp.where(tril, ws_ref[grp], 0.0).astype(BF16)
        cols = slice(grp * B_GROUP_DIM, (grp + 1) * B_GROUP_DIM)
        for c in range(chunks):
            rows = slice(c * CHUNK, (c + 1) * CHUNK)
            mixed = _dot(w, v[rows, cols]) + bias[:, cols]
            o_ref[rows, cols] = (u[rows, cols] * mixed).astype(o_ref.dtype)


def _gmlp(uv, ln_g, ln_b, w_s, b_s, chunks=4):
    n = uv.shape[0]
    rows = chunks * CHUNK
    bias_full = jnp.repeat(b_s.T, B_GROUP_DIM, axis=1)
    return pl.pallas_call(
        functools.partial(_gmlp_kernel, chunks=chunks),
        grid=(n // rows,),
        in_specs=[
            pl.BlockSpec((rows, B_WIDTH), lambda i: (i, 0)),
            pl.BlockSpec((rows, B_WIDTH), lambda i: (i, 1)),
            pl.BlockSpec((1, B_WIDTH), lambda i: (0, 0)),
            pl.BlockSpec((1, B_WIDTH), lambda i: (0, 0)),
            pl.BlockSpec((B_GROUPS, CHUNK, CHUNK), lambda i: (0, 0, 0)),
            pl.BlockSpec((CHUNK, B_WIDTH), lambda i: (0, 0)),
        ],
        out_specs=pl.BlockSpec((rows, B_WIDTH), lambda i: (i, 0)),
        out_shape=jax.ShapeDtypeStruct((n, B_WIDTH), BF16),
        compiler_params=_params("parallel"),
        name="gmlp",
    )(uv, uv, ln_g.reshape(1, B_WIDTH), ln_b.reshape(1, B_WIDTH), w_s, bias_full)


KEY_LOOP_UNROLL = 16
VT_ROWS = HEAD_DIM + 16


def _moba_seq_kernel(q_ref, k_ref, v_ref, o_ref, kmean_sc, qt_sc, vt_sc, selb_sc, st_sc, *, n_blocks):
    blk = MOBA_BLOCK
    n_rows = kmean_sc.shape[0]
    scale = HEAD_DIM ** -0.5

    kmean_sc[...] = jnp.zeros_like(kmean_sc)
    ones_rows = (lax.broadcasted_iota(jnp.int32, (VT_ROWS - HEAD_DIM, blk), 0) == 0).astype(BF16)
    key_lane = lax.broadcasted_iota(jnp.int32, (blk, HEAD_PAIR), 1)
    head_lanes = [key_lane < HEAD_DIM, key_lane >= HEAD_DIM]
    bias_base = [HEAD_DIM, 0]

    def prep(m, carry):
        rows = pl.ds(m * blk, blk)
        kmean_sc[pl.ds(m, 1), :] = jnp.mean(k_ref[rows, :].astype(F32), axis=0, keepdims=True)
        qt_sc[m] = (jnp.transpose(q_ref[rows, :].astype(F32)) * scale).astype(BF16)
        vt = jnp.transpose(v_ref[rows, :].astype(F32)).astype(BF16)
        for h in range(2):
            vt_sc[m, h] = jnp.concatenate([vt[h * HEAD_DIM:(h + 1) * HEAD_DIM], ones_rows], axis=0)
        return carry

    for m in range(n_blocks):
        prep(m, 0)

    chan = lax.broadcasted_iota(jnp.int32, (HEAD_PAIR, blk), 0)
    head_rows = [chan < HEAD_DIM, chan >= HEAD_DIM]
    kmean = kmean_sc[...].astype(BF16)
    cand = lax.broadcasted_iota(jnp.int32, (n_rows, blk), 0)

    def select(n, carry):
        qt = qt_sc[n]
        for h in range(2):
            gate = _dot(kmean, jnp.where(head_rows[h], qt, 0))
            left = jnp.where(cand < n, gate, -jnp.inf)
            chosen = cand == n
            for _ in range(MOBA_TOPK):
                best = jnp.max(left, axis=0, keepdims=True)
                first = jnp.min(jnp.where(left == best, cand, n_rows), axis=0, keepdims=True)
                take = (cand == first) & (best > -jnp.inf)
                chosen = chosen | (take & (best < jnp.inf))
                left = jnp.where(take, -jnp.inf, left)
            selb_sc[n, h] = jnp.where(chosen, 0.0, NEG_BIG).astype(BF16)
        return carry

    for n in range(n_blocks):
        select(n, 0)

    key_i = lax.broadcasted_iota(jnp.int32, (blk, blk), 0)
    qry_i = lax.broadcasted_iota(jnp.int32, (blk, blk), 1)
    causal = key_i <= qry_i
    pad_rows = jnp.zeros((HEAD_DIM - n_rows, blk), BF16)

    def block_rows(m):
        start = m * blk
        return pl.ds(start if isinstance(m, int) else pl.multiple_of(start, blk), blk)

    def softmax_stage(m_run, st):
        m_new = jnp.maximum(m_run, jnp.max(st, axis=0, keepdims=True))
        return m_new, jnp.exp(m_run - m_new), jnp.exp(st - m_new).astype(BF16)

    def query_block(n, carry):
        qt = qt_sc[n]
        w_aug = [jnp.concatenate([qt[:HEAD_DIM], selb_sc[n, 0], pad_rows], axis=0),
                 jnp.concatenate([selb_sc[n, 1], pad_rows, qt[HEAD_DIM:]], axis=0)]

        def scores(m):
            k_m = k_ref[block_rows(m), :]
            out = []
            for h in range(2):
                k_aug = jnp.where(head_lanes[h], k_m, (key_lane == bias_base[h] + m).astype(BF16))
                out.append(_dot(k_aug, w_aug[h]))
            return out

        def body(i, state):
            m_run, acc = state
            st_cur = [st_sc[i % 2, h] for h in range(2)]
            st_next = scores(i + 1)
            out_m, out_acc = [], []
            for h in range(2):
                st_sc[(i + 1) % 2, h] = st_next[h]
                m_new, alpha, p = softmax_stage(m_run[h], st_cur[h])
                out_m.append(m_new)
                out_acc.append(alpha * acc[h] + _dot(vt_sc[i, h], p))
            return out_m, out_acc

        for h, st in enumerate(scores(0)):
            st_sc[0, h] = st
        init = ([jnp.full((1, blk), M_INIT, F32)] * 2, [jnp.zeros((VT_ROWS, blk), F32)] * 2)
        m_run, acc = lax.fori_loop(0, n, body, init, unroll=max(1, min(KEY_LOOP_UNROLL, n)))
        outs = []
        for h in range(2):
            _, alpha, p = softmax_stage(m_run[h], jnp.where(causal, st_sc[n % 2, h], NEG_BIG))
            acc_h = alpha * acc[h] + _dot(vt_sc[n, h], p)
            outs.append(acc_h[:HEAD_DIM] / acc_h[HEAD_DIM:HEAD_DIM + 1])
        o_ref[block_rows(n), :] = jnp.transpose(jnp.concatenate(outs, axis=0)).astype(o_ref.dtype)
        return carry

    for n in range(n_blocks):
        query_block(n, 0)


def _moba(qkv, batch, seq):
    assert seq % MOBA_BLOCK == 0
    n_blocks = seq // MOBA_BLOCK
    n_rows = -(-n_blocks // 16) * 16
    assert n_rows <= HEAD_DIM
    pairs = C_HEADS // 2
    return pl.pallas_call(
        functools.partial(_moba_seq_kernel, n_blocks=n_blocks),
        grid=(batch, pairs),
        in_specs=[
            pl.BlockSpec((seq, HEAD_PAIR), lambda b, j: (b, j)),
            pl.BlockSpec((seq, HEAD_PAIR), lambda b, j: (b, pairs + j)),
            pl.BlockSpec((seq, HEAD_PAIR), lambda b, j: (b, 2 * pairs + j)),
        ],
        out_specs=pl.BlockSpec((seq, HEAD_PAIR), lambda b, j: (b, j)),
        out_shape=jax.ShapeDtypeStruct((batch * seq, pairs * HEAD_PAIR), BF16),
        scratch_shapes=[
            pltpu.VMEM((n_rows, HEAD_PAIR), F32),
            pltpu.VMEM((n_blocks, HEAD_PAIR, MOBA_BLOCK), BF16),
            pltpu.VMEM((n_blocks, 2, VT_ROWS, MOBA_BLOCK), BF16),
            pltpu.VMEM((n_blocks, 2, n_rows, MOBA_BLOCK), BF16),
            pltpu.VMEM((2, 2, MOBA_BLOCK, MOBA_BLOCK), F32),
        ],
        compiler_params=_params("parallel", "parallel"),
        name="moba",
    )(qkv, qkv, qkv)


def _memattn_kernel(q_ref, k_ref, v_ref, o_ref):
    s = _dot_nt(q_ref[...], k_ref[...]) * (X_HEAD_DIM ** -0.5)
    m = jnp.max(s, axis=-1, keepdims=True)
    p = jnp.exp(s - m)
    l = jnp.sum(p, axis=-1, keepdims=True)
    o_ref[...] = (_dot(p.astype(BF16), v_ref[...]) / l).astype(o_ref.dtype)


def _mem_attention(q, kv, batch, seq, mem_len, q_tile=1024):
    q_tile = min(q_tile, seq)
    tiles = seq // q_tile
    return pl.pallas_call(
        _memattn_kernel,
        grid=(batch, tiles, X_HEADS),
        in_specs=[
            pl.BlockSpec((q_tile, X_HEAD_DIM), lambda b, i, h: (b * tiles + i, h)),
            pl.BlockSpec((mem_len, X_HEAD_DIM), lambda b, i, h: (b, h)),
            pl.BlockSpec((mem_len, X_HEAD_DIM), lambda b, i, h: (b, X_HEADS + h)),
        ],
        out_specs=pl.BlockSpec((q_tile, X_HEAD_DIM), lambda b, i, h: (b * tiles + i, h)),
        out_shape=jax.ShapeDtypeStruct((batch * seq, D_MODEL), BF16),
        compiler_params=_params("parallel", "parallel", "arbitrary"),
        name="mem_attention",
    )(q, kv, kv)


def _prep_ffn(w_in, w_out):
    return w_in.astype(BF16), w_out.astype(BF16)


def _mem_block(x, mem2, w_q, w_kv, w_o, g, b, batch, seq, mem_len):
    q = _project(x, w_q.astype(BF16), BF16)
    kv = _project(mem2, w_kv.astype(BF16), BF16)
    o = _mem_attention(q, kv, batch, seq, mem_len)
    return _outproj_deepnorm([o], [w_o.astype(BF16)], x, g, b)


def kernel(x, mem, l0_ffn1_w_in, l0_ffn1_w_out, l0_ln1_g, l0_ln1_b, l0_mix_w_in, l0_gmlp_ln_g, l0_gmlp_ln_b, l0_gmlp_w_s, l0_gmlp_b_s, l0_mix_w_out, l0_ln2_g, l0_ln2_b, l0_mem_w_q, l0_mem_w_kv, l0_mem_w_o, l0_ln3_g, l0_ln3_b, l0_ffn2_w_in, l0_ffn2_w_out, l0_ln4_g, l0_ln4_b, l1_ffn1_w_in, l1_ffn1_w_out, l1_ln1_g, l1_ln1_b, l1_mix_w_in, l1_mix_w_out, l1_ln2_g, l1_ln2_b, l1_mem_w_q, l1_mem_w_kv, l1_mem_w_o, l1_ln3_g, l1_ln3_b, l1_ffn2_w_in, l1_ffn2_w_out, l1_ln4_g, l1_ln4_b):
    batch, seq, d = x.shape
    mem_len = mem.shape[1]
    h = x.reshape(batch * seq, d)
    mem2 = mem.reshape(batch * mem_len, d)

    h = _ffn_deepnorm(h, *_prep_ffn(l0_ffn1_w_in, l0_ffn1_w_out), l0_ln1_g, l0_ln1_b)
    qkv = _project(h, l0_mix_w_in[:, :A_QKV_WIDTH].astype(BF16), F32)
    uv = _project(h, l0_mix_w_in[:, A_QKV_WIDTH:].astype(BF16), F32)
    a_out = _dilated_attention(qkv, batch, seq)
    b_out = _gmlp(uv, l0_gmlp_ln_g, l0_gmlp_ln_b, l0_gmlp_w_s, l0_gmlp_b_s)
    a_width = a_out.shape[1]
    w_mix_out = [l0_mix_w_out[:a_width].astype(BF16), l0_mix_w_out[a_width:].astype(BF16)]
    h = _outproj_deepnorm([a_out, b_out], w_mix_out, h, l0_ln2_g, l0_ln2_b)
    h = _mem_block(h, mem2, l0_mem_w_q, l0_mem_w_kv, l0_mem_w_o, l0_ln3_g, l0_ln3_b, batch, seq, mem_len)
    h = _ffn_deepnorm(h, *_prep_ffn(l0_ffn2_w_in, l0_ffn2_w_out), l0_ln4_g, l0_ln4_b)

    h = _ffn_deepnorm(h, *_prep_ffn(l1_ffn1_w_in, l1_ffn1_w_out), l1_ln1_g, l1_ln1_b)
    qkv = _project(h, l1_mix_w_in.astype(BF16), BF16)
    o = _moba(qkv, batch, seq)
    h = _outproj_deepnorm([o], [l1_mix_w_out.astype(BF16)], h, l1_ln2_g, l1_ln2_b)
    h = _mem_block(h, mem2, l1_mem_w_q, l1_mem_w_kv, l1_mem_w_o, l1_ln3_g, l1_ln3_b, batch, seq, mem_len)
    h = _ffn_deepnorm(h, *_prep_ffn(l1_ffn2_w_in, l1_ffn2_w_out), l1_ln4_g, l1_ln4_b)
    return h.reshape(batch, seq, d)
```

```python
import functools

import jax
import jax.numpy as jnp
from jax import lax
from jax.experimental import pallas as pl
from jax.experimental.pallas import tpu as pltpu

D_MODEL = 2048
DEPTH = 2
HEAD_DIM = 64
HEAD_PAIR = 2 * HEAD_DIM
DIL_GROUPS = ((128, 1), (512, 4), (2048, 16))
A_HEADS_PER_GROUP = 8
A_HEADS = A_HEADS_PER_GROUP * len(DIL_GROUPS)
BAND_BLOCK = 128
CHUNK = 128
B_GROUPS = 8
B_WIDTH = 1024
B_GROUP_DIM = B_WIDTH // B_GROUPS
C_HEADS = D_MODEL // HEAD_DIM
MOBA_BLOCK = 256
MOBA_TOPK = 3
X_HEADS = 4
X_HEAD_DIM = D_MODEL // X_HEADS
D_FF = 5504
DEEPNORM_ALPHA = (2 * DEPTH) ** 0.25
LN_EPS = 1e-5
A_QKV_WIDTH = 3 * A_HEADS * HEAD_DIM

LANES = 128
FF_TILE = 512
ROW_TILE = 512
VMEM_LIMIT = 56 * 1024 * 1024
NEG_BIG = -1e30
M_INIT = -1e29

BF16 = jnp.bfloat16
F32 = jnp.float32


def _params(*sem):
    return pltpu.CompilerParams(dimension_semantics=sem, vmem_limit_bytes=VMEM_LIMIT)


def _dot(a, b):
    return jnp.dot(a, b, preferred_element_type=F32)


def _dot_nt(a, b):
    return lax.dot_general(a, b, (((1,), (1,)), ((), ())), preferred_element_type=F32)


def _layer_norm(y, g, b):
    mu = jnp.mean(y, axis=-1, keepdims=True)
    yc = y - mu
    var = jnp.mean(yc * yc, axis=-1, keepdims=True)
    return yc * lax.rsqrt(var + LN_EPS) * g + b


FF_FULL = D_FF // FF_TILE
FF_TAIL = D_FF - FF_FULL * FF_TILE
assert FF_TAIL % LANES == 0 and FF_TAIL > 0
FFN_TILES_PER_REGION = 4


def _ff_tile_start(f, base=0):
    assert base % LANES == 0 and FF_TILE % LANES == 0
    return (jnp.minimum(f, FF_FULL - 1) * (FF_TILE // LANES) + base // LANES) * LANES


def _ffn_kernel(x_ref, wg_ref, wu_ref, wo_ref, wg_tail, wu_tail, wo_tail, g_ref, b_ref, o_ref,
                xb_ref, acc_ref, y_ref):
    i = pl.program_id(0)
    f = pl.program_id(1)
    last = f == pl.num_programs(1) - 1
    final_row = i == pl.num_programs(0) - 1

    def contribution(wg, wu, wo):
        xb = xb_ref[...]
        gate = _dot(xb, wg[...])
        up = _dot(xb, wu[...])
        act = (gate * jax.nn.sigmoid(gate)) * up
        return _dot(act.astype(BF16), wo[...])

    def pre_norm():
        return DEEPNORM_ALPHA * x_ref[...] + 0.5 * (acc_ref[...] + contribution(wg_tail, wu_tail, wo_tail))

    @pl.when(jnp.logical_and(i == 0, f == 0))
    def _():
        y_ref[...] = jnp.zeros_like(y_ref)

    @pl.when(f == 0)
    def _():
        xb_ref[...] = x_ref[...].astype(BF16)
        o_ref[...] = _layer_norm(y_ref[...], g_ref[...], b_ref[...])
        acc_ref[...] = contribution(wg_ref, wu_ref, wo_ref)

    @pl.when(jnp.logical_and(f > 0, jnp.logical_not(last)))
    def _():
        acc_ref[...] += contribution(wg_ref, wu_ref, wo_ref)

    @pl.when(jnp.logical_and(last, jnp.logical_not(final_row)))
    def _():
        y_ref[...] = pre_norm()

    @pl.when(jnp.logical_and(last, final_row))
    def _():
        o_ref[...] = _layer_norm(pre_norm(), g_ref[...], b_ref[...])


def _ffn_stream_kernel(x_ref, w_in_hbm, w_out_hbm, g_ref, b_ref, o_ref,
                       xb_ref, acc_ref, y_ref, wg0, wu0, wo0, wg_buf, wu_buf, wo_buf, sem0, sems):
    i = pl.program_id(0)
    n_tiles = pl.num_programs(0) - 1

    def tile_copies(f, slot, width=FF_TILE, resident=False):
        start = f * FF_TILE
        if not isinstance(f, int):
            start = pl.multiple_of(start, FF_TILE)
        cols = pl.ds(0, width)
        dst = (wg0, wu0, wo0) if resident else (wg_buf.at[slot], wu_buf.at[slot], wo_buf.at[slot])
        sem = sem0 if resident else sems.at[slot]
        return (
            pltpu.make_async_copy(w_in_hbm.at[:, pl.ds(start, width)], dst[0].at[:, cols], sem.at[0]),
            pltpu.make_async_copy(w_in_hbm.at[:, pl.ds(D_FF + start, width)], dst[1].at[:, cols], sem.at[1]),
            pltpu.make_async_copy(w_out_hbm.at[pl.ds(start, width), :], dst[2].at[cols, :], sem.at[2]),
        )

    def contribution(wg, wu, wo, width=FF_TILE):
        xb = xb_ref[...]
        gate = _dot(xb, wg[:, :width])
        up = _dot(xb, wu[:, :width])
        act = (gate * jax.nn.sigmoid(gate)) * up
        return _dot(act.astype(BF16), wo[:width, :])

    @pl.when(i == 0)
    def _():
        for c in tile_copies(0, 0, resident=True):
            c.start()
        y_ref[...] = jnp.zeros_like(y_ref)
        for c in tile_copies(0, 0, resident=True):
            c.wait()

    @pl.when(i < n_tiles)
    def _():
        for c in tile_copies(1, 1):
            c.start()
        xb_ref[...] = x_ref[...].astype(BF16)
        o_ref[...] = _layer_norm(y_ref[...], g_ref[...], b_ref[...])
        acc_ref[...] = contribution(wg0, wu0, wo0)

        def streamed_tile(f, next_width):
            slot = f % 2
            for c in tile_copies(f + 1, 1 - slot, next_width):
                c.start()
            for c in tile_copies(f, slot):
                c.wait()
            acc_ref[...] += contribution(wg_buf.at[slot], wu_buf.at[slot], wo_buf.at[slot])

        def full_tile(f, carry):
            streamed_tile(f, FF_TILE)
            return carry

        lax.fori_loop(1, FF_FULL - 1, full_tile, 0, unroll=FFN_TILES_PER_REGION)
        streamed_tile(FF_FULL - 1, FF_TAIL)
        tail_slot = FF_FULL % 2
        for c in tile_copies(FF_FULL, tail_slot, FF_TAIL):
            c.wait()
        ffn = acc_ref[...] + contribution(wg_buf.at[tail_slot], wu_buf.at[tail_slot], wo_buf.at[tail_slot], FF_TAIL)
        y_ref[...] = DEEPNORM_ALPHA * x_ref[...] + 0.5 * ffn

    @pl.when(i == n_tiles)
    def _():
        o_ref[...] = _layer_norm(y_ref[...], g_ref[...], b_ref[...])


def _ffn_deepnorm(x, w_in, w_out, g, b):
    n, d = x.shape
    n_tiles = n // ROW_TILE
    row_tile = lambda i: (jnp.minimum(i, n_tiles - 1), 0)
    return pl.pallas_call(
        _ffn_stream_kernel,
        grid=(n_tiles + 1,),
        in_specs=[
            pl.BlockSpec((ROW_TILE, d), row_tile),
            pl.BlockSpec(memory_space=pl.ANY),
            pl.BlockSpec(memory_space=pl.ANY),
            pl.BlockSpec((1, d), lambda i: (0, 0)),
            pl.BlockSpec((1, d), lambda i: (0, 0)),
        ],
        out_specs=pl.BlockSpec((ROW_TILE, d), lambda i: (jnp.maximum(i - 1, 0), 0)),
        out_shape=jax.ShapeDtypeStruct((n, d), F32),
        scratch_shapes=[
            pltpu.VMEM((ROW_TILE, d), BF16),
            pltpu.VMEM((ROW_TILE, d), F32),
            pltpu.VMEM((ROW_TILE, d), F32),
            pltpu.VMEM((d, FF_TILE), BF16),
            pltpu.VMEM((d, FF_TILE), BF16),
            pltpu.VMEM((FF_TILE, d), BF16),
            pltpu.VMEM((2, d, FF_TILE), BF16),
            pltpu.VMEM((2, d, FF_TILE), BF16),
            pltpu.VMEM((2, FF_TILE, d), BF16),
            pltpu.SemaphoreType.DMA((3,)),
            pltpu.SemaphoreType.DMA((2, 3)),
        ],
        compiler_params=_params("arbitrary"),
        name="ffn_deepnorm",
    )(x, w_in, w_out, g.reshape(1, d), b.reshape(1, d))


def _ffn_deepnorm_tiled(x, w_in, w_out, g, b):
    n, d = x.shape
    tail_start = FF_FULL * FF_TILE
    n_tiles = n // ROW_TILE
    full = lambda rows, cols: (pl.Element(rows), pl.Element(cols))

    def out_index(i, f):
        own_step = jnp.logical_and(i == n_tiles - 1, f == FF_FULL)
        return jnp.where(own_step, i, jnp.maximum(i - 1, 0)), 0

    return pl.pallas_call(
        _ffn_kernel,
        grid=(n_tiles, FF_FULL + 1),
        in_specs=[
            pl.BlockSpec((ROW_TILE, d), lambda i, f: (i, 0)),
            pl.BlockSpec(full(d, FF_TILE), lambda i, f: (0, _ff_tile_start(f))),
            pl.BlockSpec(full(d, FF_TILE), lambda i, f: (0, _ff_tile_start(f, D_FF))),
            pl.BlockSpec(full(FF_TILE, d), lambda i, f: (_ff_tile_start(f), 0)),
            pl.BlockSpec(full(d, FF_TAIL), lambda i, f: (0, tail_start)),
            pl.BlockSpec(full(d, FF_TAIL), lambda i, f: (0, D_FF + tail_start)),
            pl.BlockSpec(full(FF_TAIL, d), lambda i, f: (tail_start, 0)),
            pl.BlockSpec((1, d), lambda i, f: (0, 0)),
            pl.BlockSpec((1, d), lambda i, f: (0, 0)),
        ],
        out_specs=pl.BlockSpec((ROW_TILE, d), out_index),
        out_shape=jax.ShapeDtypeStruct((n, d), F32),
        scratch_shapes=[
            pltpu.VMEM((ROW_TILE, d), BF16),
            pltpu.VMEM((ROW_TILE, d), F32),
            pltpu.VMEM((ROW_TILE, d), F32),
        ],
        compiler_params=_params("arbitrary", "arbitrary"),
        name="ffn_deepnorm",
    )(x, w_in, w_in, w_out, w_in, w_in, w_out, g.reshape(1, d), b.reshape(1, d))


def _proj_kernel(x_ref, w_ref, o_ref):
    o_ref[...] = _dot(x_ref[...].astype(BF16), w_ref[...]).astype(o_ref.dtype)


PROJ_MAX_COLS = 2048


def _project(x, w, out_dtype, row_tile=1024):
    n, k = x.shape
    m = w.shape[1]
    row_tile = min(row_tile, n)
    col_tile = max(c for c in range(LANES, PROJ_MAX_COLS + 1, LANES) if m % c == 0)
    return pl.pallas_call(
        _proj_kernel,
        grid=(n // row_tile, m // col_tile),
        in_specs=[
            pl.BlockSpec((row_tile, k), lambda i, j: (i, 0)),
            pl.BlockSpec((k, col_tile), lambda i, j: (0, j)),
        ],
        out_specs=pl.BlockSpec((row_tile, col_tile), lambda i, j: (i, j)),
        out_shape=jax.ShapeDtypeStruct((n, m), out_dtype),
        compiler_params=_params("parallel", "arbitrary"),
        name="project",
    )(x, w)


OUTPROJ_CHUNK = 256


def _outproj_kernel(*refs, n_in):
    a_refs, w_refs = refs[:n_in], refs[n_in:2 * n_in]
    x_ref, g_ref, b_ref, o_ref = refs[2 * n_in:]
    for c in range(ROW_TILE // OUTPROJ_CHUNK):
        rows = slice(c * OUTPROJ_CHUNK, (c + 1) * OUTPROJ_CHUNK)
        fx = _dot(a_refs[0][rows, :], w_refs[0][...])
        for a_ref, w_ref in zip(a_refs[1:], w_refs[1:]):
            fx = fx + _dot(a_ref[rows, :], w_ref[...])
        y = DEEPNORM_ALPHA * x_ref[rows, :] + fx
        o_ref[rows, :] = _layer_norm(y, g_ref[...], b_ref[...])


def _outproj_deepnorm(a_list, w_list, x, g, b):
    n, d = x.shape
    n_in = len(a_list)
    in_specs = [pl.BlockSpec((ROW_TILE, a.shape[1]), lambda i: (i, 0)) for a in a_list]
    in_specs += [pl.BlockSpec(w.shape, lambda i: (0, 0)) for w in w_list]
    in_specs += [
        pl.BlockSpec((ROW_TILE, d), lambda i: (i, 0)),
        pl.BlockSpec((1, d), lambda i: (0, 0)),
        pl.BlockSpec((1, d), lambda i: (0, 0)),
    ]
    return pl.pallas_call(
        functools.partial(_outproj_kernel, n_in=n_in),
        grid=(n // ROW_TILE,),
        in_specs=in_specs,
        out_specs=pl.BlockSpec((ROW_TILE, d), lambda i: (i, 0)),
        out_shape=jax.ShapeDtypeStruct((n, d), F32),
        compiler_params=_params("parallel"),
        name="outproj_deepnorm",
    )(*a_list, *w_list, x, g.reshape(1, d), b.reshape(1, d))


def _dilated_kernel(*refs, seq):
    n_g = len(DIL_GROUPS)
    qkv_refs = refs[:3 * n_g]
    o_ref = refs[3 * n_g]
    o_sc, lse_sc, bias_sc = refs[3 * n_g + 1:]

    bb = BAND_BLOCK
    qi = lax.broadcasted_iota(jnp.int32, (bb, 2 * bb), 0)
    ki = lax.broadcasted_iota(jnp.int32, (bb, 2 * bb), 1)
    for first, delta in ((0, bb), (1, 0)):
        off = qi + delta - ki
        bias_sc[first] = jnp.where((off >= 0) & (off <= bb), 0.0, -jnp.inf).astype(F32)

    lane = lax.broadcasted_iota(jnp.int32, (bb, HEAD_PAIR), 1)
    head0 = lane < HEAD_DIM
    scale = HEAD_DIM ** -0.5

    for g, (window, dil) in enumerate(DIL_GROUPS):
        assert window // dil == bb
        q_ref, k_ref, v_ref = qkv_refs[3 * g:3 * g + 3]
        n_blk = seq // (dil * bb)

        def body(it, carry, q_ref=q_ref, k_ref=k_ref, v_ref=v_ref, dil=dil, g=g):
            blk = it // dil
            phase = it - blk * dil
            is_first = jnp.where(blk == 0, 1, 0)
            q_start = blk * (bb * dil) + phase
            k_start = jnp.maximum(blk - 1, 0) * (bb * dil) + phase
            if dil == 1:
                q_rows, k_rows = pl.ds(q_start, bb), pl.ds(k_start, 2 * bb)
            else:
                q_rows = pl.ds(q_start, bb, stride=dil)
                k_rows = pl.ds(k_start, 2 * bb, stride=dil)
            q = q_ref[q_rows, :] * scale
            k = k_ref[k_rows, :].astype(BF16)
            v = v_ref[k_rows, :].astype(BF16)
            bias = bias_sc[is_first]
            outs, lses = [], []
            for h0 in (True, False):
                qh = jnp.where(head0 if h0 else ~head0, q, 0.0).astype(BF16)
                s = _dot_nt(qh, k) + bias
                m = jnp.max(s, axis=-1, keepdims=True)
                p = jnp.exp(s - m)
                l = jnp.sum(p, axis=-1, keepdims=True)
                outs.append(_dot(p.astype(BF16), v) / l)
                lses.append(m + jnp.log(l))
            o_sc[g, q_rows, :] = jnp.where(head0, outs[0], outs[1])
            lse_sc[g, q_rows, :] = jnp.where(head0, lses[0], lses[1])
            return carry

        lax.fori_loop(0, n_blk * dil, body, 0, unroll=32)

    lse = [lse_sc[g] for g in range(n_g)]
    top = functools.reduce(jnp.maximum, lse)
    w = [jnp.exp(x - top) for x in lse]
    den = functools.reduce(jnp.add, w)
    acc = functools.reduce(jnp.add, [w[g] * o_sc[g] for g in range(n_g)])
    o_ref[...] = (acc / den).astype(o_ref.dtype)


def _dilated_attention(qkv, batch, seq):
    n_g = len(DIL_GROUPS)
    pairs = A_HEADS_PER_GROUP // 2
    blocks_per_part = A_HEADS * HEAD_DIM // HEAD_PAIR
    in_specs = []
    for g in range(n_g):
        for part in range(3):
            base = part * blocks_per_part + g * pairs
            in_specs.append(pl.BlockSpec((seq, HEAD_PAIR), lambda b, j, base=base: (b, base + j)))
    return pl.pallas_call(
        functools.partial(_dilated_kernel, seq=seq),
        grid=(batch, pairs),
        in_specs=in_specs,
        out_specs=pl.BlockSpec((seq, HEAD_PAIR), lambda b, j: (b, j)),
        out_shape=jax.ShapeDtypeStruct((batch * seq, pairs * HEAD_PAIR), BF16),
        scratch_shapes=[
            pltpu.VMEM((n_g, seq, HEAD_PAIR), F32),
            pltpu.VMEM((n_g, seq, HEAD_PAIR), F32),
            pltpu.VMEM((2, BAND_BLOCK, 2 * BAND_BLOCK), F32),
        ],
        compiler_params=_params("parallel", "parallel"),
        name="dilated_attention",
    )(*([qkv] * (3 * n_g)))


def _gmlp_kernel(u_ref, v_ref, g_ref, b_ref, ws_ref, bias_ref, o_ref, *, chunks):
    u = jax.nn.gelu(u_ref[...])
    v = _layer_norm(jax.nn.gelu(v_ref[...]), g_ref[...], b_ref[...]).astype(BF16)
    row = lax.broadcasted_iota(jnp.int32, (CHUNK, CHUNK), 0)
    col = lax.broadcasted_iota(jnp.int32, (CHUNK, CHUNK), 1)
    tril = row >= col
    bias = bias_ref[...]
    for grp in range(B_GROUPS):
        w = jnp.where(tril, ws_ref[grp], 0.0).astype(BF16)
        cols = slice(grp * B_GROUP_DIM, (grp + 1) * B_GROUP_DIM)
        for c in range(chunks):
            rows = slice(c * CHUNK, (c + 1) * CHUNK)
            mixed = _dot(w, v[rows, cols]) + bias[:, cols]
            o_ref[rows, cols] = (u[rows, cols] * mixed).astype(o_ref.dtype)


def _gmlp(uv, ln_g, ln_b, w_s, b_s, chunks=4):
    n = uv.shape[0]
    rows = chunks * CHUNK
    bias_full = jnp.repeat(b_s.T, B_GROUP_DIM, axis=1)
    return pl.pallas_call(
        functools.partial(_gmlp_kernel, chunks=chunks),
        grid=(n // rows,),
        in_specs=[
            pl.BlockSpec((rows, B_WIDTH), lambda i: (i, 0)),
            pl.BlockSpec((rows, B_WIDTH), lambda i: (i, 1)),
            pl.BlockSpec((1, B_WIDTH), lambda i: (0, 0)),
            pl.BlockSpec((1, B_WIDTH), lambda i: (0, 0)),
            pl.BlockSpec((B_GROUPS, CHUNK, CHUNK), lambda i: (0, 0, 0)),
            pl.BlockSpec((CHUNK, B_WIDTH), lambda i: (0, 0)),
        ],
        out_specs=pl.BlockSpec((rows, B_WIDTH), lambda i: (i, 0)),
        out_shape=jax.ShapeDtypeStruct((n, B_WIDTH), BF16),
        compiler_params=_params("parallel"),
        name="gmlp",
    )(uv, uv, ln_g.reshape(1, B_WIDTH), ln_b.reshape(1, B_WIDTH), w_s, bias_full)


KEY_LOOP_UNROLL = 16
VT_ROWS = HEAD_DIM + 16


def _moba_seq_kernel(q_ref, k_ref, v_ref, o_ref, kmean_sc, qt_sc, vt_sc, selb_sc, st_sc, *, n_blocks):
    blk = MOBA_BLOCK
    n_rows = kmean_sc.shape[0]
    scale = HEAD_DIM ** -0.5

    kmean_sc[...] = jnp.zeros_like(kmean_sc)
    ones_rows = (lax.broadcasted_iota(jnp.int32, (VT_ROWS - HEAD_DIM, blk), 0) == 0).astype(BF16)
    key_lane = lax.broadcasted_iota(jnp.int32, (blk, HEAD_PAIR), 1)
    head_lanes = [key_lane < HEAD_DIM, key_lane >= HEAD_DIM]
    bias_base = [HEAD_DIM, 0]

    def prep(m, carry):
        rows = pl.ds(m * blk, blk)
        kmean_sc[pl.ds(m, 1), :] = jnp.mean(k_ref[rows, :].astype(F32), axis=0, keepdims=True)
        qt_sc[m] = (jnp.transpose(q_ref[rows, :].astype(F32)) * scale).astype(BF16)
        vt = jnp.transpose(v_ref[rows, :].astype(F32)).astype(BF16)
        for h in range(2):
            vt_sc[m, h] = jnp.concatenate([vt[h * HEAD_DIM:(h + 1) * HEAD_DIM], ones_rows], axis=0)
        return carry

    for m in range(n_blocks):
        prep(m, 0)

    chan = lax.broadcasted_iota(jnp.int32, (HEAD_PAIR, blk), 0)
    head_rows = [chan < HEAD_DIM, chan >= HEAD_DIM]
    kmean = kmean_sc[...].astype(BF16)
    cand = lax.broadcasted_iota(jnp.int32, (n_rows, blk), 0)

    def select(n, carry):
        qt = qt_sc[n]
        for h in range(2):
            gate = _dot(kmean, jnp.where(head_rows[h], qt, 0))
            left = jnp.where(cand < n, gate, -jnp.inf)
            chosen = cand == n
            for _ in range(MOBA_TOPK):
                best = jnp.max(left, axis=0, keepdims=True)
                first = jnp.min(jnp.where(left == best, cand, n_rows), axis=0, keepdims=True)
                take = (cand == first) & (best > -jnp.inf)
                chosen = chosen | (take & (best < jnp.inf))
                left = jnp.where(take, -jnp.inf, left)
            selb_sc[n, h] = jnp.where(chosen, 0.0, NEG_BIG).astype(BF16)
        return carry

    for n in range(n_blocks):
        select(n, 0)

    key_i = lax.broadcasted_iota(jnp.int32, (blk, blk), 0)
    qry_i = lax.broadcasted_iota(jnp.int32, (blk, blk), 1)
    causal = key_i <= qry_i
    pad_rows = jnp.zeros((HEAD_DIM - n_rows, blk), BF16)

    def block_rows(m):
        start = m * blk
        return pl.ds(start if isinstance(m, int) else pl.multiple_of(start, blk), blk)

    def softmax_stage(m_run, st):
        m_new = jnp.maximum(m_run, jnp.max(st, axis=0, keepdims=True))
        return m_new, jnp.exp(m_run - m_new), jnp.exp(st - m_new).astype(BF16)

    def query_block(n, carry):
        qt = qt_sc[n]
        w_aug = [jnp.concatenate([qt[:HEAD_DIM], selb_sc[n, 0], pad_rows], axis=0),
                 jnp.concatenate([selb_sc[n, 1], pad_rows, qt[HEAD_DIM:]], axis=0)]

        def scores(m):
            k_m = k_ref[block_rows(m), :]
            out = []
            for h in range(2):
                k_aug = jnp.where(head_lanes[h], k_m, (key_lane == bias_base[h] + m).astype(BF16))
                out.append(_dot(k_aug, w_aug[h]))
            return out

        def body(i, state):
            m_run, acc = state
            st_cur = [st_sc[i % 2, h] for h in range(2)]
            st_next = scores(i + 1)
            out_m, out_acc = [], []
            for h in range(2):
                st_sc[(i + 1) % 2, h] = st_next[h]
                m_new, alpha, p = softmax_stage(m_run[h], st_cur[h])
                out_m.append(m_new)
                out_acc.append(alpha * acc[h] + _dot(vt_sc[i, h], p))
            return out_m, out_acc

        for h, st in enumerate(scores(0)):
            st_sc[0, h] = st
        init = ([jnp.full((1, blk), M_INIT, F32)] * 2, [jnp.zeros((VT_ROWS, blk), F32)] * 2)
        m_run, acc = lax.fori_loop(0, n, body, init, unroll=max(1, min(KEY_LOOP_UNROLL, n)))
        outs = []
        for h in range(2):
            _, alpha, p = softmax_stage(m_run[h], jnp.where(causal, st_sc[n % 2, h], NEG_BIG))
            acc_h = alpha * acc[h] + _dot(vt_sc[n, h], p)
            outs.append(acc_h[:HEAD_DIM] / acc_h[HEAD_DIM:HEAD_DIM + 1])
        o_ref[block_rows(n), :] = jnp.transpose(jnp.concatenate(outs, axis=0)).astype(o_ref.dtype)
        return carry

    for n in range(n_blocks):
        query_block(n, 0)


def _moba(qkv, batch, seq):
    assert seq % MOBA_BLOCK == 0
    n_blocks = seq // MOBA_BLOCK
    n_rows = -(-n_blocks // 16) * 16
    assert n_rows <= HEAD_DIM
    pairs = C_HEADS // 2
    return pl.pallas_call(
        functools.partial(_moba_seq_kernel, n_blocks=n_blocks),
        grid=(batch, pairs),
        in_specs=[
            pl.BlockSpec((seq, HEAD_PAIR), lambda b, j: (b, j)),
            pl.BlockSpec((seq, HEAD_PAIR), lambda b, j: (b, pairs + j)),
            pl.BlockSpec((seq, HEAD_PAIR), lambda b, j: (b, 2 * pairs + j)),
        ],
        out_specs=pl.BlockSpec((seq, HEAD_PAIR), lambda b, j: (b, j)),
        out_shape=jax.ShapeDtypeStruct((batch * seq, pairs * HEAD_PAIR), BF16),
        scratch_shapes=[
            pltpu.VMEM((n_rows, HEAD_PAIR), F32),
            pltpu.VMEM((n_blocks, HEAD_PAIR, MOBA_BLOCK), BF16),
            pltpu.VMEM((n_blocks, 2, VT_ROWS, MOBA_BLOCK), BF16),
            pltpu.VMEM((n_blocks, 2, n_rows, MOBA_BLOCK), BF16),
            pltpu.VMEM((2, 2, MOBA_BLOCK, MOBA_BLOCK), F32),
        ],
        compiler_params=_params("parallel", "parallel"),
        name="moba",
    )(qkv, qkv, qkv)


def _memattn_kernel(q_ref, k_ref, v_ref, o_ref):
    s = _dot_nt(q_ref[...], k_ref[...]) * (X_HEAD_DIM ** -0.5)
    m = jnp.max(s, axis=-1, keepdims=True)
    p = jnp.exp(s - m)
    l = jnp.sum(p, axis=-1, keepdims=True)
    o_ref[...] = (_dot(p.astype(BF16), v_ref[...]) / l).astype(o_ref.dtype)


def _mem_attention(q, kv, batch, seq, mem_len, q_tile=1024):
    q_tile = min(q_tile, seq)
    tiles = seq // q_tile
    return pl.pallas_call(
        _memattn_kernel,
        grid=(batch, tiles, X_HEADS),
        in_specs=[
            pl.BlockSpec((q_tile, X_HEAD_DIM), lambda b, i, h: (b * tiles + i, h)),
            pl.BlockSpec((mem_len, X_HEAD_DIM), lambda b, i, h: (b, h)),
            pl.BlockSpec((mem_len, X_HEAD_DIM), lambda b, i, h: (b, X_HEADS + h)),
        ],
        out_specs=pl.BlockSpec((q_tile, X_HEAD_DIM), lambda b, i, h: (b * tiles + i, h)),
        out_shape=jax.ShapeDtypeStruct((batch * seq, D_MODEL), BF16),
        compiler_params=_params("parallel", "parallel", "arbitrary"),
        name="mem_attention",
    )(q, kv, kv)


def _prep_ffn(w_in, w_out):
    return w_in.astype(BF16), w_out.astype(BF16)


def _mem_block(x, mem2, w_q, w_kv, w_o, g, b, batch, seq, mem_len):
    q = _project(x, w_q.astype(BF16), BF16)
    kv = _project(mem2, w_kv.astype(BF16), BF16)
    o = _mem_attention(q, kv, batch, seq, mem_len)
    return _outproj_deepnorm([o], [w_o.astype(BF16)], x, g, b)


def kernel(x, mem, l0_ffn1_w_in, l0_ffn1_w_out, l0_ln1_g, l0_ln1_b, l0_mix_w_in, l0_gmlp_ln_g, l0_gmlp_ln_b, l0_gmlp_w_s, l0_gmlp_b_s, l0_mix_w_out, l0_ln2_g, l0_ln2_b, l0_mem_w_q, l0_mem_w_kv, l0_mem_w_o, l0_ln3_g, l0_ln3_b, l0_ffn2_w_in, l0_ffn2_w_out, l0_ln4_g, l0_ln4_b, l1_ffn1_w_in, l1_ffn1_w_out, l1_ln1_g, l1_ln1_b, l1_mix_w_in, l1_mix_w_out, l1_ln2_g, l1_ln2_b, l1_mem_w_q, l1_mem_w_kv, l1_mem_w_o, l1_ln3_g, l1_ln3_b, l1_ffn2_w_in, l1_ffn2_w_out, l1_ln4_g, l1_ln4_b):
    batch, seq, d = x.shape
    mem_len = mem.shape[1]
    h = x.reshape(batch * seq, d)
    mem2 = mem.reshape(batch * mem_len, d)

    h = _ffn_deepnorm(h, *_prep_ffn(l0_ffn1_w_in, l0_ffn1_w_out), l0_ln1_g, l0_ln1_b)
    qkv = _project(h, l0_mix_w_in[:, :A_QKV_WIDTH].astype(BF16), F32)
    uv = _project(h, l0_mix_w_in[:, A_QKV_WIDTH:].astype(BF16), F32)
    a_out = _dilated_attention(qkv, batch, seq)
    b_out = _gmlp(uv, l0_gmlp_ln_g, l0_gmlp_ln_b, l0_gmlp_w_s, l0_gmlp_b_s)
    a_width = a_out.shape[1]
    w_mix_out = [l0_mix_w_out[:a_width].astype(BF16), l0_mix_w_out[a_width:].astype(BF16)]
    h = _outproj_deepnorm([a_out, b_out], w_mix_out, h, l0_ln2_g, l0_ln2_b)
    h = _mem_block(h, mem2, l0_mem_w_q, l0_mem_w_kv, l0_mem_w_o, l0_ln3_g, l0_ln3_b, batch, seq, mem_len)
    h = _ffn_deepnorm(h, *_prep_ffn(l0_ffn2_w_in, l0_ffn2_w_out), l0_ln4_g, l0_ln4_b)

    h = _ffn_deepnorm(h, *_prep_ffn(l1_ffn1_w_in, l1_ffn1_w_out), l1_ln1_g, l1_ln1_b)
    qkv = _project(h, l1_mix_w_in.astype(BF16), BF16)
    o = _moba(qkv, batch, seq)
    h = _outproj_deepnorm([o], [l1_mix_w_out.astype(BF16)], h, l1_ln2_g, l1_ln2_b)
    h = _mem_block(h, mem2, l1_mem_w_q, l1_mem_w_kv, l1_mem_w_o, l1_ln3_g, l1_ln3_b, batch, seq, mem_len)
    h = _ffn_deepnorm(h, *_prep_ffn(l1_ffn2_w_in, l1_ffn2_w_out), l1_ln4_g, l1_ln4_b)
    return h.reshape(batch, seq, d)
```

```python
import functools

import jax
import jax.numpy as jnp
from jax import lax
from jax.experimental import pallas as pl
from jax.experimental.pallas import tpu as pltpu

D_MODEL = 2048
DEPTH = 2
HEAD_DIM = 64
HEAD_PAIR = 2 * HEAD_DIM
DIL_GROUPS = ((128, 1), (512, 4), (2048, 16))
A_HEADS_PER_GROUP = 8
A_HEADS = A_HEADS_PER_GROUP * len(DIL_GROUPS)
BAND_BLOCK = 128
CHUNK = 128
B_GROUPS = 8
B_WIDTH = 1024
B_GROUP_DIM = B_WIDTH // B_GROUPS
C_HEADS = D_MODEL // HEAD_DIM
MOBA_BLOCK = 256
MOBA_TOPK = 3
X_HEADS = 4
X_HEAD_DIM = D_MODEL // X_HEADS
D_FF = 5504
DEEPNORM_ALPHA = (2 * DEPTH) ** 0.25
LN_EPS = 1e-5
A_QKV_WIDTH = 3 * A_HEADS * HEAD_DIM

LANES = 128
FF_TILE = 512
ROW_TILE = 512
VMEM_LIMIT = 56 * 1024 * 1024
NEG_BIG = -1e30
M_INIT = -1e29

BF16 = jnp.bfloat16
F32 = jnp.float32


def _params(*sem):
    return pltpu.CompilerParams(dimension_semantics=sem, vmem_limit_bytes=VMEM_LIMIT)


def _dot(a, b):
    return jnp.dot(a, b, preferred_element_type=F32)


def _dot_nt(a, b):
    return lax.dot_general(a, b, (((1,), (1,)), ((), ())), preferred_element_type=F32)


def _layer_norm(y, g, b):
    mu = jnp.mean(y, axis=-1, keepdims=True)
    yc = y - mu
    var = jnp.mean(yc * yc, axis=-1, keepdims=True)
    return yc * lax.rsqrt(var + LN_EPS) * g + b


FF_FULL = D_FF // FF_TILE
FF_TAIL = D_FF - FF_FULL * FF_TILE
assert FF_TAIL % LANES == 0 and FF_TAIL > 0
FFN_TILES_PER_REGION = 4


def _ffn_stream_kernel(x_ref, w_in_hbm, w_out_hbm, g_ref, b_ref, o_ref,
                       xb_ref, acc_ref, y_ref, wg0, wu0, wo0, wg_buf, wu_buf, wo_buf, sem0, sems):
    i = pl.program_id(0)
    n_tiles = pl.num_programs(0) - 1

    def tile_copies(f, slot, width=FF_TILE, resident=False):
        start = f * FF_TILE
        if not isinstance(f, int):
            start = pl.multiple_of(start, FF_TILE)
        cols = pl.ds(0, width)
        dst = (wg0, wu0, wo0) if resident else (wg_buf.at[slot], wu_buf.at[slot], wo_buf.at[slot])
        sem = sem0 if resident else sems.at[slot]
        return (
            pltpu.make_async_copy(w_in_hbm.at[:, pl.ds(start, width)], dst[0].at[:, cols], sem.at[0]),
            pltpu.make_async_copy(w_in_hbm.at[:, pl.ds(D_FF + start, width)], dst[1].at[:, cols], sem.at[1]),
            pltpu.make_async_copy(w_out_hbm.at[pl.ds(start, width), :], dst[2].at[cols, :], sem.at[2]),
        )

    def contribution(wg, wu, wo, width=FF_TILE):
        xb = xb_ref[...]
        gate = _dot(xb, wg[:, :width])
        up = _dot(xb, wu[:, :width])
        act = (gate * jax.nn.sigmoid(gate)) * up
        return _dot(act.astype(BF16), wo[:width, :])

    @pl.when(i == 0)
    def _():
        for c in tile_copies(0, 0, resident=True):
            c.start()
        y_ref[...] = jnp.zeros_like(y_ref)
        for c in tile_copies(0, 0, resident=True):
            c.wait()

    @pl.when(i < n_tiles)
    def _():
        for c in tile_copies(1, 1):
            c.start()
        xb_ref[...] = x_ref[...].astype(BF16)
        o_ref[...] = _layer_norm(y_ref[...], g_ref[...], b_ref[...])
        acc_ref[...] = contribution(wg0, wu0, wo0)

        def streamed_tile(f, next_width):
            slot = f % 2
            for c in tile_copies(f + 1, 1 - slot, next_width):
                c.start()
            for c in tile_copies(f, slot):
                c.wait()
            acc_ref[...] += contribution(wg_buf.at[slot], wu_buf.at[slot], wo_buf.at[slot])

        def full_tile(f, carry):
            streamed_tile(f, FF_TILE)
            return carry

        lax.fori_loop(1, FF_FULL - 1, full_tile, 0, unroll=FFN_TILES_PER_REGION)
        streamed_tile(FF_FULL - 1, FF_TAIL)
        tail_slot = FF_FULL % 2
        for c in tile_copies(FF_FULL, tail_slot, FF_TAIL):
            c.wait()
        ffn = acc_ref[...] + contribution(wg_buf.at[tail_slot], wu_buf.at[tail_slot], wo_buf.at[tail_slot], FF_TAIL)
        y_ref[...] = DEEPNORM_ALPHA * x_ref[...] + 0.5 * ffn

    @pl.when(i == n_tiles)
    def _():
        o_ref[...] = _layer_norm(y_ref[...], g_ref[...], b_ref[...])


def _ffn_deepnorm(x, w_in, w_out, g, b):
    n, d = x.shape
    n_tiles = n // ROW_TILE
    row_tile = lambda i: (jnp.minimum(i, n_tiles - 1), 0)
    return pl.pallas_call(
        _ffn_stream_kernel,
        grid=(n_tiles + 1,),
        in_specs=[
            pl.BlockSpec((ROW_TILE, d), row_tile),
            pl.BlockSpec(memory_space=pl.ANY),
            pl.BlockSpec(memory_space=pl.ANY),
            pl.BlockSpec((1, d), lambda i: (0, 0)),
            pl.BlockSpec((1, d), lambda i: (0, 0)),
        ],
        out_specs=pl.BlockSpec((ROW_TILE, d), lambda i: (jnp.maximum(i - 1, 0), 0)),
        out_shape=jax.ShapeDtypeStruct((n, d), F32),
        scratch_shapes=[
            pltpu.VMEM((ROW_TILE, d), BF16),
            pltpu.VMEM((ROW_TILE, d), F32),
            pltpu.VMEM((ROW_TILE, d), F32),
            pltpu.VMEM((d, FF_TILE), BF16),
            pltpu.VMEM((d, FF_TILE), BF16),
            pltpu.VMEM((FF_TILE, d), BF16),
            pltpu.VMEM((2, d, FF_TILE), BF16),
            pltpu.VMEM((2, d, FF_TILE), BF16),
            pltpu.VMEM((2, FF_TILE, d), BF16),
            pltpu.SemaphoreType.DMA((3,)),
            pltpu.SemaphoreType.DMA((2, 3)),
        ],
        compiler_params=_params("arbitrary"),
        name="ffn_deepnorm",
    )(x, w_in, w_out, g.reshape(1, d), b.reshape(1, d))


def _proj_kernel(x_ref, w_ref, o_ref):
    o_ref[...] = _dot(x_ref[...].astype(BF16), w_ref[...]).astype(o_ref.dtype)


PROJ_MAX_COLS = 2048


def _project(x, w, out_dtype, row_tile=1024):
    n, k = x.shape
    m = w.shape[1]
    row_tile = min(row_tile, n)
    col_tile = max(c for c in range(LANES, PROJ_MAX_COLS + 1, LANES) if m % c == 0)
    return pl.pallas_call(
        _proj_kernel,
        grid=(n // row_tile, m // col_tile),
        in_specs=[
            pl.BlockSpec((row_tile, k), lambda i, j: (i, 0)),
            pl.BlockSpec((k, col_tile), lambda i, j: (0, j)),
        ],
        out_specs=pl.BlockSpec((row_tile, col_tile), lambda i, j: (i, j)),
        out_shape=jax.ShapeDtypeStruct((n, m), out_dtype),
        compiler_params=_params("parallel", "arbitrary"),
        name="project",
    )(x, w)


OUTPROJ_CHUNK = 256
OUTPROJ_ROWS = 1024


def _outproj_kernel(*refs, n_in):
    a_refs, w_refs = refs[:n_in], refs[n_in:2 * n_in]
    x_ref, g_ref, b_ref, o_ref = refs[2 * n_in:]
    for c in range(x_ref.shape[0] // OUTPROJ_CHUNK):
        rows = slice(c * OUTPROJ_CHUNK, (c + 1) * OUTPROJ_CHUNK)
        fx = _dot(a_refs[0][rows, :], w_refs[0][...])
        for a_ref, w_ref in zip(a_refs[1:], w_refs[1:]):
            fx = fx + _dot(a_ref[rows, :], w_ref[...])
        y = DEEPNORM_ALPHA * x_ref[rows, :] + fx
        o_ref[rows, :] = _layer_norm(y, g_ref[...], b_ref[...])


def _outproj_deepnorm(a_list, w_list, x, g, b):
    n, d = x.shape
    n_in = len(a_list)
    rows = min(OUTPROJ_ROWS, n)
    resident = pl.Buffered(1)
    in_specs = [pl.BlockSpec((rows, a.shape[1]), lambda i: (i, 0)) for a in a_list]
    in_specs += [pl.BlockSpec(w.shape, lambda i: (0, 0), pipeline_mode=resident) for w in w_list]
    in_specs += [
        pl.BlockSpec((rows, d), lambda i: (i, 0)),
        pl.BlockSpec((1, d), lambda i: (0, 0)),
        pl.BlockSpec((1, d), lambda i: (0, 0)),
    ]
    return pl.pallas_call(
        functools.partial(_outproj_kernel, n_in=n_in),
        grid=(n // rows,),
        in_specs=in_specs,
        out_specs=pl.BlockSpec((rows, d), lambda i: (i, 0)),
        out_shape=jax.ShapeDtypeStruct((n, d), F32),
        compiler_params=_params("parallel"),
        name="outproj_deepnorm",
    )(*a_list, *w_list, x, g.reshape(1, d), b.reshape(1, d))


def _dilated_kernel(*refs, seq):
    n_g = len(DIL_GROUPS)
    qkv_refs = refs[:3 * n_g]
    o_ref = refs[3 * n_g]
    o_sc, lse_sc, bias_sc = refs[3 * n_g + 1:]

    bb = BAND_BLOCK
    qi = lax.broadcasted_iota(jnp.int32, (bb, 2 * bb), 0)
    ki = lax.broadcasted_iota(jnp.int32, (bb, 2 * bb), 1)
    for first, delta in ((0, bb), (1, 0)):
        off = qi + delta - ki
        bias_sc[first] = jnp.where((off >= 0) & (off <= bb), 0.0, -jnp.inf).astype(F32)

    lane = lax.broadcasted_iota(jnp.int32, (bb, HEAD_PAIR), 1)
    head0 = lane < HEAD_DIM
    scale = HEAD_DIM ** -0.5

    for g, (window, dil) in enumerate(DIL_GROUPS):
        assert window // dil == bb
        q_ref, k_ref, v_ref = qkv_refs[3 * g:3 * g + 3]
        n_blk = seq // (dil * bb)

        def body(it, carry, q_ref=q_ref, k_ref=k_ref, v_ref=v_ref, dil=dil, g=g):
            blk = it // dil
            phase = it - blk * dil
            is_first = jnp.where(blk == 0, 1, 0)
            q_start = blk * (bb * dil) + phase
            k_start = jnp.maximum(blk - 1, 0) * (bb * dil) + phase
            if dil == 1:
                q_rows, k_rows = pl.ds(q_start, bb), pl.ds(k_start, 2 * bb)
            else:
                q_rows = pl.ds(q_start, bb, stride=dil)
                k_rows = pl.ds(k_start, 2 * bb, stride=dil)
            q = q_ref[q_rows, :] * scale
            k = k_ref[k_rows, :].astype(BF16)
            v = v_ref[k_rows, :].astype(BF16)
            bias = bias_sc[is_first]
            outs, lses = [], []
            for h0 in (True, False):
                qh = jnp.where(head0 if h0 else ~head0, q, 0.0).astype(BF16)
                s = _dot_nt(qh, k) + bias
                m = jnp.max(s, axis=-1, keepdims=True)
                p = jnp.exp(s - m)
                l = jnp.sum(p, axis=-1, keepdims=True)
                outs.append(_dot(p.astype(BF16), v) / l)
                lses.append(m + jnp.log(l))
            o_sc[g, q_rows, :] = jnp.where(head0, outs[0], outs[1])
            lse_sc[g, q_rows, :] = jnp.where(head0, lses[0], lses[1])
            return carry

        lax.fori_loop(0, n_blk * dil, body, 0, unroll=32)

    lse = [lse_sc[g] for g in range(n_g)]
    top = functools.reduce(jnp.maximum, lse)
    w = [jnp.exp(x - top) for x in lse]
    den = functools.reduce(jnp.add, w)
    acc = functools.reduce(jnp.add, [w[g] * o_sc[g] for g in range(n_g)])
    o_ref[...] = (acc / den).astype(o_ref.dtype)


def _dilated_attention(qkv, batch, seq):
    n_g = len(DIL_GROUPS)
    pairs = A_HEADS_PER_GROUP // 2
    blocks_per_part = A_HEADS * HEAD_DIM // HEAD_PAIR
    in_specs = []
    for g in range(n_g):
        for part in range(3):
            base = part * blocks_per_part + g * pairs
            in_specs.append(pl.BlockSpec((seq, HEAD_PAIR), lambda b, j, base=base: (b, base + j)))
    return pl.pallas_call(
        functools.partial(_dilated_kernel, seq=seq),
        grid=(batch, pairs),
        in_specs=in_specs,
        out_specs=pl.BlockSpec((seq, HEAD_PAIR), lambda b, j: (b, j)),
        out_shape=jax.ShapeDtypeStruct((batch * seq, pairs * HEAD_PAIR), BF16),
        scratch_shapes=[
            pltpu.VMEM((n_g, seq, HEAD_PAIR), F32),
            pltpu.VMEM((n_g, seq, HEAD_PAIR), F32),
            pltpu.VMEM((2, BAND_BLOCK, 2 * BAND_BLOCK), F32),
        ],
        compiler_params=_params("parallel", "parallel"),
        name="dilated_attention",
    )(*([qkv] * (3 * n_g)))


def _gmlp_kernel(u_ref, v_ref, g_ref, b_ref, ws_ref, bias_ref, o_ref, *, chunks):
    u = jax.nn.gelu(u_ref[...])
    v = _layer_norm(jax.nn.gelu(v_ref[...]), g_ref[...], b_ref[...]).astype(BF16)
    row = lax.broadcasted_iota(jnp.int32, (CHUNK, CHUNK), 0)
    col = lax.broadcasted_iota(jnp.int32, (CHUNK, CHUNK), 1)
    tril = row >= col
    bias = bias_ref[...]
    for grp in range(B_GROUPS):
        w = jnp.where(tril, ws_ref[grp], 0.0).astype(BF16)
        cols = slice(grp * B_GROUP_DIM, (grp + 1) * B_GROUP_DIM)
        for c in range(chunks):
            rows = slice(c * CHUNK, (c + 1) * CHUNK)
            mixed = _dot(w, v[rows, cols]) + bias[:, cols]
            o_ref[rows, cols] = (u[rows, cols] * mixed).astype(o_ref.dtype)


def _gmlp(uv, ln_g, ln_b, w_s, b_s, chunks=8):
    n = uv.shape[0]
    rows = chunks * CHUNK
    bias_full = jnp.repeat(b_s.T, B_GROUP_DIM, axis=1)
    return pl.pallas_call(
        functools.partial(_gmlp_kernel, chunks=chunks),
        grid=(n // rows,),
        in_specs=[
            pl.BlockSpec((rows, B_WIDTH), lambda i: (i, 0)),
            pl.BlockSpec((rows, B_WIDTH), lambda i: (i, 1)),
            pl.BlockSpec((1, B_WIDTH), lambda i: (0, 0)),
            pl.BlockSpec((1, B_WIDTH), lambda i: (0, 0)),
            pl.BlockSpec((B_GROUPS, CHUNK, CHUNK), lambda i: (0, 0, 0)),
            pl.BlockSpec((CHUNK, B_WIDTH), lambda i: (0, 0)),
        ],
        out_specs=pl.BlockSpec((rows, B_WIDTH), lambda i: (i, 0)),
        out_shape=jax.ShapeDtypeStruct((n, B_WIDTH), BF16),
        compiler_params=_params("parallel"),
        name="gmlp",
    )(uv, uv, ln_g.reshape(1, B_WIDTH), ln_b.reshape(1, B_WIDTH), w_s, bias_full)


KEY_LOOP_UNROLL = 16
VT_ROWS = HEAD_DIM + 16


def _moba_seq_kernel(q_ref, k_ref, v_ref, o_ref, kmean_sc, qt_sc, vt_sc, selb_sc, st_sc, *, n_blocks):
    blk = MOBA_BLOCK
    n_rows = kmean_sc.shape[0]
    scale = HEAD_DIM ** -0.5

    kmean_sc[...] = jnp.zeros_like(kmean_sc)
    ones_rows = (lax.broadcasted_iota(jnp.int32, (VT_ROWS - HEAD_DIM, blk), 0) == 0).astype(BF16)
    key_lane = lax.broadcasted_iota(jnp.int32, (blk, HEAD_PAIR), 1)
    head_lanes = [key_lane < HEAD_DIM, key_lane >= HEAD_DIM]
    bias_base = [HEAD_DIM, 0]

    def prep(m, carry):
        rows = pl.ds(m * blk, blk)
        kmean_sc[pl.ds(m, 1), :] = jnp.mean(k_ref[rows, :].astype(F32), axis=0, keepdims=True)
        qt_sc[m] = (jnp.transpose(q_ref[rows, :].astype(F32)) * scale).astype(BF16)
        vt = jnp.transpose(v_ref[rows, :].astype(F32)).astype(BF16)
        for h in range(2):
            vt_sc[m, h] = jnp.concatenate([vt[h * HEAD_DIM:(h + 1) * HEAD_DIM], ones_rows], axis=0)
        return carry

    for m in range(n_blocks):
        prep(m, 0)

    chan = lax.broadcasted_iota(jnp.int32, (HEAD_PAIR, blk), 0)
    head_rows = [chan < HEAD_DIM, chan >= HEAD_DIM]
    kmean = kmean_sc[...].astype(BF16)
    cand = lax.broadcasted_iota(jnp.int32, (n_rows, blk), 0)

    def select(n, carry):
        qt = qt_sc[n]
        for h in range(2):
            gate = _dot(kmean, jnp.where(head_rows[h], qt, 0))
            left = jnp.where(cand < n, gate, -jnp.inf)
            chosen = cand == n
            for _ in range(MOBA_TOPK):
                best = jnp.max(left, axis=0, keepdims=True)
                first = jnp.min(jnp.where(left == best, cand, n_rows), axis=0, keepdims=True)
                take = (cand == first) & (best > -jnp.inf)
                chosen = chosen | (take & (best < jnp.inf))
                left = jnp.where(take, -jnp.inf, left)
            selb_sc[n, h] = jnp.where(chosen, 0.0, NEG_BIG).astype(BF16)
        return carry

    for n in range(n_blocks):
        select(n, 0)

    key_i = lax.broadcasted_iota(jnp.int32, (blk, blk), 0)
    qry_i = lax.broadcasted_iota(jnp.int32, (blk, blk), 1)
    causal = key_i <= qry_i
    pad_rows = jnp.zeros((HEAD_DIM - n_rows, blk), BF16)

    def block_rows(m):
        start = m * blk
        return pl.ds(start if isinstance(m, int) else pl.multiple_of(start, blk), blk)

    def softmax_stage(m_run, st):
        m_new = jnp.maximum(m_run, jnp.max(st, axis=0, keepdims=True))
        return m_new, jnp.exp(m_run - m_new), jnp.exp(st - m_new).astype(BF16)

    def query_block(n, carry):
        qt = qt_sc[n]
        w_aug = [jnp.concatenate([qt[:HEAD_DIM], selb_sc[n, 0], pad_rows], axis=0),
                 jnp.concatenate([selb_sc[n, 1], pad_rows, qt[HEAD_DIM:]], axis=0)]

        def scores(m):
            k_m = k_ref[block_rows(m), :]
            out = []
            for h in range(2):
                k_aug = jnp.where(head_lanes[h], k_m, (key_lane == bias_base[h] + m).astype(BF16))
                out.append(_dot(k_aug, w_aug[h]))
            return out

        def body(i, state):
            m_run, acc = state
            st_cur = [st_sc[i % 2, h] for h in range(2)]
            st_next = scores(i + 1)
            out_m, out_acc = [], []
            for h in range(2):
                st_sc[(i + 1) % 2, h] = st_next[h]
                m_new, alpha, p = softmax_stage(m_run[h], st_cur[h])
                out_m.append(m_new)
                out_acc.append(alpha * acc[h] + _dot(vt_sc[i, h], p))
            return out_m, out_acc

        for h, st in enumerate(scores(0)):
            st_sc[0, h] = st
        init = ([jnp.full((1, blk), M_INIT, F32)] * 2, [jnp.zeros((VT_ROWS, blk), F32)] * 2)
        m_run, acc = lax.fori_loop(0, n, body, init, unroll=max(1, min(KEY_LOOP_UNROLL, n)))
        outs = []
        for h in range(2):
            _, alpha, p = softmax_stage(m_run[h], jnp.where(causal, st_sc[n % 2, h], NEG_BIG))
            acc_h = alpha * acc[h] + _dot(vt_sc[n, h], p)
            outs.append(acc_h[:HEAD_DIM] / acc_h[HEAD_DIM:HEAD_DIM + 1])
        o_ref[block_rows(n), :] = jnp.transpose(jnp.concatenate(outs, axis=0)).astype(o_ref.dtype)
        return carry

    for n in range(n_blocks):
        query_block(n, 0)


def _moba(qkv, batch, seq):
    assert seq % MOBA_BLOCK == 0
    n_blocks = seq // MOBA_BLOCK
    n_rows = -(-n_blocks // 16) * 16
    assert n_rows <= HEAD_DIM
    pairs = C_HEADS // 2
    return pl.pallas_call(
        functools.partial(_moba_seq_kernel, n_blocks=n_blocks),
        grid=(batch, pairs),
        in_specs=[
            pl.BlockSpec((seq, HEAD_PAIR), lambda b, j: (b, j)),
            pl.BlockSpec((seq, HEAD_PAIR), lambda b, j: (b, pairs + j)),
            pl.BlockSpec((seq, HEAD_PAIR), lambda b, j: (b, 2 * pairs + j)),
        ],
        out_specs=pl.BlockSpec((seq, HEAD_PAIR), lambda b, j: (b, j)),
        out_shape=jax.ShapeDtypeStruct((batch * seq, pairs * HEAD_PAIR), BF16),
        scratch_shapes=[
            pltpu.VMEM((n_rows, HEAD_PAIR), F32),
            pltpu.VMEM((n_blocks, HEAD_PAIR, MOBA_BLOCK), BF16),
            pltpu.VMEM((n_blocks, 2, VT_ROWS, MOBA_BLOCK), BF16),
            pltpu.VMEM((n_blocks, 2, n_rows, MOBA_BLOCK), BF16),
            pltpu.VMEM((2, 2, MOBA_BLOCK, MOBA_BLOCK), F32),
        ],
        compiler_params=_params("parallel", "parallel"),
        name="moba",
    )(qkv, qkv, qkv)


def _memattn_kernel(q_ref, k_ref, v_ref, o_ref):
    for h in range(X_HEADS):
        cols = slice(h * X_HEAD_DIM, (h + 1) * X_HEAD_DIM)
        s = _dot_nt(q_ref[:, cols], k_ref[:, cols]) * (X_HEAD_DIM ** -0.5)
        m = jnp.max(s, axis=-1, keepdims=True)
        p = jnp.exp(s - m)
        l = jnp.sum(p, axis=-1, keepdims=True)
        o_ref[:, cols] = (_dot(p.astype(BF16), v_ref[:, cols]) / l).astype(o_ref.dtype)


def _mem_attention(q, kv, batch, seq, mem_len, q_tile=1024):
    q_tile = min(q_tile, seq)
    tiles = seq // q_tile
    return pl.pallas_call(
        _memattn_kernel,
        grid=(batch, tiles),
        in_specs=[
            pl.BlockSpec((q_tile, D_MODEL), lambda b, i: (b * tiles + i, 0)),
            pl.BlockSpec((mem_len, D_MODEL), lambda b, i: (b, 0)),
            pl.BlockSpec((mem_len, D_MODEL), lambda b, i: (b, 1)),
        ],
        out_specs=pl.BlockSpec((q_tile, D_MODEL), lambda b, i: (b * tiles + i, 0)),
        out_shape=jax.ShapeDtypeStruct((batch * seq, D_MODEL), BF16),
        compiler_params=_params("parallel", "parallel"),
        name="mem_attention",
    )(q, kv, kv)


def _prep_ffn(w_in, w_out):
    return w_in.astype(BF16), w_out.astype(BF16)


def _mem_block(x, mem2, w_q, w_kv, w_o, g, b, batch, seq, mem_len):
    q = _project(x, w_q.astype(BF16), BF16)
    kv = _project(mem2, w_kv.astype(BF16), BF16)
    o = _mem_attention(q, kv, batch, seq, mem_len)
    return _outproj_deepnorm([o], [w_o.astype(BF16)], x, g, b)


def kernel(x, mem, l0_ffn1_w_in, l0_ffn1_w_out, l0_ln1_g, l0_ln1_b, l0_mix_w_in, l0_gmlp_ln_g, l0_gmlp_ln_b, l0_gmlp_w_s, l0_gmlp_b_s, l0_mix_w_out, l0_ln2_g, l0_ln2_b, l0_mem_w_q, l0_mem_w_kv, l0_mem_w_o, l0_ln3_g, l0_ln3_b, l0_ffn2_w_in, l0_ffn2_w_out, l0_ln4_g, l0_ln4_b, l1_ffn1_w_in, l1_ffn1_w_out, l1_ln1_g, l1_ln1_b, l1_mix_w_in, l1_mix_w_out, l1_ln2_g, l1_ln2_b, l1_mem_w_q, l1_mem_w_kv, l1_mem_w_o, l1_ln3_g, l1_ln3_b, l1_ffn2_w_in, l1_ffn2_w_out, l1_ln4_g, l1_ln4_b):
    batch, seq, d = x.shape
    mem_len = mem.shape[1]
    h = x.reshape(batch * seq, d)
    mem2 = mem.reshape(batch * mem_len, d)

    h = _ffn_deepnorm(h, *_prep_ffn(l0_ffn1_w_in, l0_ffn1_w_out), l0_ln1_g, l0_ln1_b)
    qkv = _project(h, l0_mix_w_in[:, :A_QKV_WIDTH].astype(BF16), F32)
    uv = _project(h, l0_mix_w_in[:, A_QKV_WIDTH:].astype(BF16), F32)
    a_out = _dilated_attention(qkv, batch, seq)
    b_out = _gmlp(uv, l0_gmlp_ln_g, l0_gmlp_ln_b, l0_gmlp_w_s, l0_gmlp_b_s)
    a_width = a_out.shape[1]
    w_mix_out = [l0_mix_w_out[:a_width].astype(BF16), l0_mix_w_out[a_width:].astype(BF16)]
    h = _outproj_deepnorm([a_out, b_out], w_mix_out, h, l0_ln2_g, l0_ln2_b)
    h = _mem_block(h, mem2, l0_mem_w_q, l0_mem_w_kv, l0_mem_w_o, l0_ln3_g, l0_ln3_b, batch, seq, mem_len)
    h = _ffn_deepnorm(h, *_prep_ffn(l0_ffn2_w_in, l0_ffn2_w_out), l0_ln4_g, l0_ln4_b)

    h = _ffn_deepnorm(h, *_prep_ffn(l1_ffn1_w_in, l1_ffn1_w_out), l1_ln1_g, l1_ln1_b)
    qkv = _project(h, l1_mix_w_in.astype(BF16), BF16)
    o = _moba(qkv, batch, seq)
    h = _outproj_deepnorm([o], [l1_mix_w_out.astype(BF16)], h, l1_ln2_g, l1_ln2_b)
    h = _mem_block(h, mem2, l1_mem_w_q, l1_mem_w_kv, l1_mem_w_o, l1_ln3_g, l1_ln3_b, batch, seq, mem_len)
    h = _ffn_deepnorm(h, *_prep_ffn(l1_ffn2_w_in, l1_ffn2_w_out), l1_ln4_g, l1_ln4_b)
    return h.reshape(batch, seq, d)
```

```python
import functools

import jax
import jax.numpy as jnp
from jax import lax
from jax.experimental import pallas as pl
from jax.experimental.pallas import tpu as pltpu

D_MODEL = 2048
DEPTH = 2
HEAD_DIM = 64
HEAD_PAIR = 2 * HEAD_DIM
DIL_GROUPS = ((128, 1), (512, 4), (2048, 16))
A_HEADS_PER_GROUP = 8
A_HEADS = A_HEADS_PER_GROUP * len(DIL_GROUPS)
BAND_BLOCK = 128
CHUNK = 128
B_GROUPS = 8
B_WIDTH = 1024
B_GROUP_DIM = B_WIDTH // B_GROUPS
C_HEADS = D_MODEL // HEAD_DIM
MOBA_BLOCK = 256
MOBA_TOPK = 3
X_HEADS = 4
X_HEAD_DIM = D_MODEL // X_HEADS
D_FF = 5504
DEEPNORM_ALPHA = (2 * DEPTH) ** 0.25
LN_EPS = 1e-5
A_QKV_WIDTH = 3 * A_HEADS * HEAD_DIM

LANES = 128
FF_TILE = 512
ROW_TILE = 512
VMEM_LIMIT = 56 * 1024 * 1024
NEG_BIG = -1e30
M_INIT = -1e29

BF16 = jnp.bfloat16
F32 = jnp.float32


def _params(*sem):
    return pltpu.CompilerParams(dimension_semantics=sem, vmem_limit_bytes=VMEM_LIMIT)


def _dot(a, b):
    return jnp.dot(a, b, preferred_element_type=F32)


def _dot_nt(a, b):
    return lax.dot_general(a, b, (((1,), (1,)), ((), ())), preferred_element_type=F32)


def _layer_norm(y, g, b):
    mu = jnp.mean(y, axis=-1, keepdims=True)
    yc = y - mu
    var = jnp.mean(yc * yc, axis=-1, keepdims=True)
    return yc * lax.rsqrt(var + LN_EPS) * g + b


FF_FULL = D_FF // FF_TILE
FF_TAIL = D_FF - FF_FULL * FF_TILE
assert FF_TAIL % LANES == 0 and FF_TAIL > 0
FFN_TILES_PER_REGION = 4


def _ffn_stream_kernel(x_ref, w_in_hbm, w_out_hbm, g_ref, b_ref, o_ref,
                       xb_ref, acc_ref, y_ref, wg0, wu0, wo0, wg_buf, wu_buf, wo_buf, sem0, sems):
    i = pl.program_id(0)
    n_tiles = pl.num_programs(0) - 1

    def tile_copies(f, slot, width=FF_TILE, resident=False):
        start = f * FF_TILE
        if not isinstance(f, int):
            start = pl.multiple_of(start, FF_TILE)
        cols = pl.ds(0, width)
        dst = (wg0, wu0, wo0) if resident else (wg_buf.at[slot], wu_buf.at[slot], wo_buf.at[slot])
        sem = sem0 if resident else sems.at[slot]
        return (
            pltpu.make_async_copy(w_in_hbm.at[:, pl.ds(start, width)], dst[0].at[:, cols], sem.at[0]),
            pltpu.make_async_copy(w_in_hbm.at[:, pl.ds(D_FF + start, width)], dst[1].at[:, cols], sem.at[1]),
            pltpu.make_async_copy(w_out_hbm.at[pl.ds(start, width), :], dst[2].at[cols, :], sem.at[2]),
        )

    def contribution(wg, wu, wo, width=FF_TILE):
        xb = xb_ref[...]
        gate = _dot(xb, wg[:, :width])
        up = _dot(xb, wu[:, :width])
        act = (gate * jax.nn.sigmoid(gate)) * up
        return _dot(act.astype(BF16), wo[:width, :])

    @pl.when(i == 0)
    def _():
        for c in tile_copies(0, 0, resident=True):
            c.start()
        y_ref[...] = jnp.zeros_like(y_ref)
        for c in tile_copies(0, 0, resident=True):
            c.wait()

    @pl.when(i < n_tiles)
    def _():
        for c in tile_copies(1, 1):
            c.start()
        xb_ref[...] = x_ref[...].astype(BF16)
        o_ref[...] = _layer_norm(y_ref[...], g_ref[...], b_ref[...])
        acc_ref[...] = contribution(wg0, wu0, wo0)

        def streamed_tile(f, next_width):
            slot = f % 2
            for c in tile_copies(f + 1, 1 - slot, next_width):
                c.start()
            for c in tile_copies(f, slot):
                c.wait()
            acc_ref[...] += contribution(wg_buf.at[slot], wu_buf.at[slot], wo_buf.at[slot])

        def full_tile(f, carry):
            streamed_tile(f, FF_TILE)
            return carry

        lax.fori_loop(1, FF_FULL - 1, full_tile, 0, unroll=FFN_TILES_PER_REGION)
        streamed_tile(FF_FULL - 1, FF_TAIL)
        tail_slot = FF_FULL % 2
        for c in tile_copies(FF_FULL, tail_slot, FF_TAIL):
            c.wait()
        ffn = acc_ref[...] + contribution(wg_buf.at[tail_slot], wu_buf.at[tail_slot], wo_buf.at[tail_slot], FF_TAIL)
        y_ref[...] = DEEPNORM_ALPHA * x_ref[...] + 0.5 * ffn

    @pl.when(i == n_tiles)
    def _():
        o_ref[...] = _layer_norm(y_ref[...], g_ref[...], b_ref[...])


def _ffn_deepnorm(x, w_in, w_out, g, b):
    n, d = x.shape
    n_tiles = n // ROW_TILE
    row_tile = lambda i: (jnp.minimum(i, n_tiles - 1), 0)
    return pl.pallas_call(
        _ffn_stream_kernel,
        grid=(n_tiles + 1,),
        in_specs=[
            pl.BlockSpec((ROW_TILE, d), row_tile),
            pl.BlockSpec(memory_space=pl.ANY),
            pl.BlockSpec(memory_space=pl.ANY),
            pl.BlockSpec((1, d), lambda i: (0, 0)),
            pl.BlockSpec((1, d), lambda i: (0, 0)),
        ],
        out_specs=pl.BlockSpec((ROW_TILE, d), lambda i: (jnp.maximum(i - 1, 0), 0)),
        out_shape=jax.ShapeDtypeStruct((n, d), F32),
        scratch_shapes=[
            pltpu.VMEM((ROW_TILE, d), BF16),
            pltpu.VMEM((ROW_TILE, d), F32),
            pltpu.VMEM((ROW_TILE, d), F32),
            pltpu.VMEM((d, FF_TILE), BF16),
            pltpu.VMEM((d, FF_TILE), BF16),
            pltpu.VMEM((FF_TILE, d), BF16),
            pltpu.VMEM((2, d, FF_TILE), BF16),
            pltpu.VMEM((2, d, FF_TILE), BF16),
            pltpu.VMEM((2, FF_TILE, d), BF16),
            pltpu.SemaphoreType.DMA((3,)),
            pltpu.SemaphoreType.DMA((2, 3)),
        ],
        compiler_params=_params("arbitrary"),
        name="ffn_deepnorm",
    )(x, w_in, w_out, g.reshape(1, d), b.reshape(1, d))


def _proj_kernel(x_ref, w_ref, o_ref):
    o_ref[...] = _dot(x_ref[...].astype(BF16), w_ref[...]).astype(o_ref.dtype)


PROJ_MAX_COLS = 2048


def _project(x, w, out_dtype, row_tile=1024):
    n, k = x.shape
    m = w.shape[1]
    row_tile = min(row_tile, n)
    col_tile = max(c for c in range(LANES, PROJ_MAX_COLS + 1, LANES) if m % c == 0)
    return pl.pallas_call(
        _proj_kernel,
        grid=(n // row_tile, m // col_tile),
        in_specs=[
            pl.BlockSpec((row_tile, k), lambda i, j: (i, 0)),
            pl.BlockSpec((k, col_tile), lambda i, j: (0, j)),
        ],
        out_specs=pl.BlockSpec((row_tile, col_tile), lambda i, j: (i, j)),
        out_shape=jax.ShapeDtypeStruct((n, m), out_dtype),
        compiler_params=_params("parallel", "arbitrary"),
        name="project",
    )(x, w)


OUTPROJ_CHUNK = 256


def _outproj_kernel(*refs, n_in):
    a_refs, w_refs = refs[:n_in], refs[n_in:2 * n_in]
    x_ref, g_ref, b_ref, o_ref = refs[2 * n_in:2 * n_in + 4]
    wb_refs = refs[2 * n_in + 4:]

    @pl.when(pl.program_id(0) == 0)
    def _():
        for w_ref, wb_ref in zip(w_refs, wb_refs):
            wb_ref[...] = w_ref[...].astype(BF16)

    for c in range(ROW_TILE // OUTPROJ_CHUNK):
        rows = slice(c * OUTPROJ_CHUNK, (c + 1) * OUTPROJ_CHUNK)
        fx = _dot(a_refs[0][rows, :], wb_refs[0][...])
        for a_ref, wb_ref in zip(a_refs[1:], wb_refs[1:]):
            fx = fx + _dot(a_ref[rows, :], wb_ref[...])
        y = DEEPNORM_ALPHA * x_ref[rows, :] + fx
        o_ref[rows, :] = _layer_norm(y, g_ref[...], b_ref[...])


def _outproj_deepnorm(a_list, w_list, x, g, b):
    n, d = x.shape
    n_in = len(a_list)
    resident = pl.Buffered(1)
    in_specs = [pl.BlockSpec((ROW_TILE, a.shape[1]), lambda i: (i, 0)) for a in a_list]
    in_specs += [pl.BlockSpec(w.shape, lambda i: (0, 0), pipeline_mode=resident) for w in w_list]
    in_specs += [
        pl.BlockSpec((ROW_TILE, d), lambda i: (i, 0)),
        pl.BlockSpec((1, d), lambda i: (0, 0)),
        pl.BlockSpec((1, d), lambda i: (0, 0)),
    ]
    return pl.pallas_call(
        functools.partial(_outproj_kernel, n_in=n_in),
        grid=(n // ROW_TILE,),
        in_specs=in_specs,
        out_specs=pl.BlockSpec((ROW_TILE, d), lambda i: (i, 0)),
        out_shape=jax.ShapeDtypeStruct((n, d), F32),
        scratch_shapes=[pltpu.VMEM(w.shape, BF16) for w in w_list],
        compiler_params=_params("arbitrary"),
        name="outproj_deepnorm",
    )(*a_list, *w_list, x, g.reshape(1, d), b.reshape(1, d))


def _dilated_kernel(*refs, seq):
    n_g = len(DIL_GROUPS)
    qkv_refs = refs[:3 * n_g]
    o_ref = refs[3 * n_g]
    o_sc, lse_sc, bias_sc = refs[3 * n_g + 1:]

    bb = BAND_BLOCK
    qi = lax.broadcasted_iota(jnp.int32, (bb, 2 * bb), 0)
    ki = lax.broadcasted_iota(jnp.int32, (bb, 2 * bb), 1)
    for first, delta in ((0, bb), (1, 0)):
        off = qi + delta - ki
        bias_sc[first] = jnp.where((off >= 0) & (off <= bb), 0.0, -jnp.inf).astype(F32)

    lane = lax.broadcasted_iota(jnp.int32, (bb, HEAD_PAIR), 1)
    head0 = lane < HEAD_DIM
    scale = HEAD_DIM ** -0.5

    for g, (window, dil) in enumerate(DIL_GROUPS):
        assert window // dil == bb
        q_ref, k_ref, v_ref = qkv_refs[3 * g:3 * g + 3]
        n_blk = seq // (dil * bb)

        def body(it, carry, q_ref=q_ref, k_ref=k_ref, v_ref=v_ref, dil=dil, g=g):
            blk = it // dil
            phase = it - blk * dil
            is_first = jnp.where(blk == 0, 1, 0)
            q_start = blk * (bb * dil) + phase
            k_start = jnp.maximum(blk - 1, 0) * (bb * dil) + phase
            if dil == 1:
                q_rows, k_rows = pl.ds(q_start, bb), pl.ds(k_start, 2 * bb)
            else:
                q_rows = pl.ds(q_start, bb, stride=dil)
                k_rows = pl.ds(k_start, 2 * bb, stride=dil)
            q = q_ref[q_rows, :] * scale
            k = k_ref[k_rows, :].astype(BF16)
            v = v_ref[k_rows, :].astype(BF16)
            bias = bias_sc[is_first]
            outs, lses = [], []
            for h0 in (True, False):
                qh = jnp.where(head0 if h0 else ~head0, q, 0.0).astype(BF16)
                s = _dot_nt(qh, k) + bias
                m = jnp.max(s, axis=-1, keepdims=True)
                p = jnp.exp(s - m)
                l = jnp.sum(p, axis=-1, keepdims=True)
                outs.append(_dot(p.astype(BF16), v) / l)
                lses.append(m + jnp.log(l))
            o_sc[g, q_rows, :] = jnp.where(head0, outs[0], outs[1])
            lse_sc[g, q_rows, :] = jnp.where(head0, lses[0], lses[1])
            return carry

        lax.fori_loop(0, n_blk * dil, body, 0, unroll=32)

    lse = [lse_sc[g] for g in range(n_g)]
    top = functools.reduce(jnp.maximum, lse)
    w = [jnp.exp(x - top) for x in lse]
    den = functools.reduce(jnp.add, w)
    acc = functools.reduce(jnp.add, [w[g] * o_sc[g] for g in range(n_g)])
    o_ref[...] = (acc / den).astype(o_ref.dtype)


def _dilated_attention(qkv, batch, seq):
    n_g = len(DIL_GROUPS)
    pairs = A_HEADS_PER_GROUP // 2
    blocks_per_part = A_HEADS * HEAD_DIM // HEAD_PAIR
    in_specs = []
    for g in range(n_g):
        for part in range(3):
            base = part * blocks_per_part + g * pairs
            in_specs.append(pl.BlockSpec((seq, HEAD_PAIR), lambda b, j, base=base: (b, base + j)))
    return pl.pallas_call(
        functools.partial(_dilated_kernel, seq=seq),
        grid=(batch, pairs),
        in_specs=in_specs,
        out_specs=pl.BlockSpec((seq, HEAD_PAIR), lambda b, j: (b, j)),
        out_shape=jax.ShapeDtypeStruct((batch * seq, pairs * HEAD_PAIR), BF16),
        scratch_shapes=[
            pltpu.VMEM((n_g, seq, HEAD_PAIR), F32),
            pltpu.VMEM((n_g, seq, HEAD_PAIR), F32),
            pltpu.VMEM((2, BAND_BLOCK, 2 * BAND_BLOCK), F32),
        ],
        compiler_params=_params("parallel", "parallel"),
        name="dilated_attention",
    )(*([qkv] * (3 * n_g)))


def _gmlp_kernel(u_ref, v_ref, g_ref, b_ref, ws_ref, bias_ref, o_ref, *, chunks):
    u = jax.nn.gelu(u_ref[...])
    v = _layer_norm(jax.nn.gelu(v_ref[...]), g_ref[...], b_ref[...]).astype(BF16)
    row = lax.broadcasted_iota(jnp.int32, (CHUNK, CHUNK), 0)
    col = lax.broadcasted_iota(jnp.int32, (CHUNK, CHUNK), 1)
    tril = row >= col
    bias = bias_ref[...]
    for grp in range(B_GROUPS):
        w = jnp.where(tril, ws_ref[grp], 0.0).astype(BF16)
        cols = slice(grp * B_GROUP_DIM, (grp + 1) * B_GROUP_DIM)
        for c in range(chunks):
            rows = slice(c * CHUNK, (c + 1) * CHUNK)
            mixed = _dot(w, v[rows, cols]) + bias[:, cols]
            o_ref[rows, cols] = (u[rows, cols] * mixed).astype(o_ref.dtype)


def _gmlp(uv, ln_g, ln_b, w_s, b_s, chunks=8):
    n = uv.shape[0]
    rows = chunks * CHUNK
    bias_full = jnp.repeat(b_s.T, B_GROUP_DIM, axis=1)
    return pl.pallas_call(
        functools.partial(_gmlp_kernel, chunks=chunks),
        grid=(n // rows,),
        in_specs=[
            pl.BlockSpec((rows, B_WIDTH), lambda i: (i, 0)),
            pl.BlockSpec((rows, B_WIDTH), lambda i: (i, 1)),
            pl.BlockSpec((1, B_WIDTH), lambda i: (0, 0)),
            pl.BlockSpec((1, B_WIDTH), lambda i: (0, 0)),
            pl.BlockSpec((B_GROUPS, CHUNK, CHUNK), lambda i: (0, 0, 0)),
            pl.BlockSpec((CHUNK, B_WIDTH), lambda i: (0, 0)),
        ],
        out_specs=pl.BlockSpec((rows, B_WIDTH), lambda i: (i, 0)),
        out_shape=jax.ShapeDtypeStruct((n, B_WIDTH), BF16),
        compiler_params=_params("parallel"),
        name="gmlp",
    )(uv, uv, ln_g.reshape(1, B_WIDTH), ln_b.reshape(1, B_WIDTH), w_s, bias_full)


KEY_LOOP_UNROLL = 16
VT_ROWS = HEAD_DIM + 16


def _moba_seq_kernel(q_ref, k_ref, v_ref, o_ref, kmean_sc, qt_sc, vt_sc, selb_sc, st_sc, *, n_blocks):
    blk = MOBA_BLOCK
    n_rows = kmean_sc.shape[0]
    scale = HEAD_DIM ** -0.5

    kmean_sc[...] = jnp.zeros_like(kmean_sc)
    ones_rows = (lax.broadcasted_iota(jnp.int32, (VT_ROWS - HEAD_DIM, blk), 0) == 0).astype(BF16)
    key_lane = lax.broadcasted_iota(jnp.int32, (blk, HEAD_PAIR), 1)
    head_lanes = [key_lane < HEAD_DIM, key_lane >= HEAD_DIM]
    bias_base = [HEAD_DIM, 0]

    def prep(m, carry):
        rows = pl.ds(m * blk, blk)
        kmean_sc[pl.ds(m, 1), :] = jnp.mean(k_ref[rows, :].astype(F32), axis=0, keepdims=True)
        qt_sc[m] = (jnp.transpose(q_ref[rows, :].astype(F32)) * scale).astype(BF16)
        vt = jnp.transpose(v_ref[rows, :].astype(F32)).astype(BF16)
        for h in range(2):
            vt_sc[m, h] = jnp.concatenate([vt[h * HEAD_DIM:(h + 1) * HEAD_DIM], ones_rows], axis=0)
        return carry

    for m in range(n_blocks):
        prep(m, 0)

    chan = lax.broadcasted_iota(jnp.int32, (HEAD_PAIR, blk), 0)
    head_rows = [chan < HEAD_DIM, chan >= HEAD_DIM]
    kmean = kmean_sc[...].astype(BF16)
    cand = lax.broadcasted_iota(jnp.int32, (n_rows, blk), 0)

    def select(n, carry):
        qt = qt_sc[n]
        for h in range(2):
            gate = _dot(kmean, jnp.where(head_rows[h], qt, 0))
            left = jnp.where(cand < n, gate, -jnp.inf)
            chosen = cand == n
            for _ in range(MOBA_TOPK):
                best = jnp.max(left, axis=0, keepdims=True)
                first = jnp.min(jnp.where(left == best, cand, n_rows), axis=0, keepdims=True)
                take = (cand == first) & (best > -jnp.inf)
                chosen = chosen | (take & (best < jnp.inf))
                left = jnp.where(take, -jnp.inf, left)
            selb_sc[n, h] = jnp.where(chosen, 0.0, NEG_BIG).astype(BF16)
        return carry

    for n in range(n_blocks):
        select(n, 0)

    key_i = lax.broadcasted_iota(jnp.int32, (blk, blk), 0)
    qry_i = lax.broadcasted_iota(jnp.int32, (blk, blk), 1)
    causal = key_i <= qry_i
    pad_rows = jnp.zeros((HEAD_DIM - n_rows, blk), BF16)

    def block_rows(m):
        start = m * blk
        return pl.ds(start if isinstance(m, int) else pl.multiple_of(start, blk), blk)

    def softmax_stage(m_run, st):
        m_new = jnp.maximum(m_run, jnp.max(st, axis=0, keepdims=True))
        return m_new, jnp.exp(m_run - m_new), jnp.exp(st - m_new).astype(BF16)

    def query_block(n, carry):
        qt = qt_sc[n]
        w_aug = [jnp.concatenate([qt[:HEAD_DIM], selb_sc[n, 0], pad_rows], axis=0),
                 jnp.concatenate([selb_sc[n, 1], pad_rows, qt[HEAD_DIM:]], axis=0)]

        def scores(m):
            k_m = k_ref[block_rows(m), :]
            out = []
            for h in range(2):
                k_aug = jnp.where(head_lanes[h], k_m, (key_lane == bias_base[h] + m).astype(BF16))
                out.append(_dot(k_aug, w_aug[h]))
            return out

        def body(i, state):
            m_run, acc = state
            st_cur = [st_sc[i % 2, h] for h in range(2)]
            st_next = scores(i + 1)
            out_m, out_acc = [], []
            for h in range(2):
                st_sc[(i + 1) % 2, h] = st_next[h]
                m_new, alpha, p = softmax_stage(m_run[h], st_cur[h])
                out_m.append(m_new)
                out_acc.append(alpha * acc[h] + _dot(vt_sc[i, h], p))
            return out_m, out_acc

        for h, st in enumerate(scores(0)):
            st_sc[0, h] = st
        init = ([jnp.full((1, blk), M_INIT, F32)] * 2, [jnp.zeros((VT_ROWS, blk), F32)] * 2)
        m_run, acc = lax.fori_loop(0, n, body, init, unroll=max(1, min(KEY_LOOP_UNROLL, n)))
        outs = []
        for h in range(2):
            _, alpha, p = softmax_stage(m_run[h], jnp.where(causal, st_sc[n % 2, h], NEG_BIG))
            acc_h = alpha * acc[h] + _dot(vt_sc[n, h], p)
            outs.append(acc_h[:HEAD_DIM] / acc_h[HEAD_DIM:HEAD_DIM + 1])
        o_ref[block_rows(n), :] = jnp.transpose(jnp.concatenate(outs, axis=0)).astype(o_ref.dtype)
        return carry

    for n in range(n_blocks):
        query_block(n, 0)


def _moba(qkv, batch, seq):
    assert seq % MOBA_BLOCK == 0
    n_blocks = seq // MOBA_BLOCK
    n_rows = -(-n_blocks // 16) * 16
    assert n_rows <= HEAD_DIM
    pairs = C_HEADS // 2
    return pl.pallas_call(
        functools.partial(_moba_seq_kernel, n_blocks=n_blocks),
        grid=(batch, pairs),
        in_specs=[
            pl.BlockSpec((seq, HEAD_PAIR), lambda b, j: (b, j)),
            pl.BlockSpec((seq, HEAD_PAIR), lambda b, j: (b, pairs + j)),
            pl.BlockSpec((seq, HEAD_PAIR), lambda b, j: (b, 2 * pairs + j)),
        ],
        out_specs=pl.BlockSpec((seq, HEAD_PAIR), lambda b, j: (b, j)),
        out_shape=jax.ShapeDtypeStruct((batch * seq, pairs * HEAD_PAIR), BF16),
        scratch_shapes=[
            pltpu.VMEM((n_rows, HEAD_PAIR), F32),
            pltpu.VMEM((n_blocks, HEAD_PAIR, MOBA_BLOCK), BF16),
            pltpu.VMEM((n_blocks, 2, VT_ROWS, MOBA_BLOCK), BF16),
            pltpu.VMEM((n_blocks, 2, n_rows, MOBA_BLOCK), BF16),
            pltpu.VMEM((2, 2, MOBA_BLOCK, MOBA_BLOCK), F32),
        ],
        compiler_params=_params("parallel", "parallel"),
        name="moba",
    )(qkv, qkv, qkv)


def _memattn_kernel(q_ref, k_ref, v_ref, o_ref):
    for h in range(X_HEADS):
        cols = slice(h * X_HEAD_DIM, (h + 1) * X_HEAD_DIM)
        s = _dot_nt(q_ref[:, cols], k_ref[:, cols]) * (X_HEAD_DIM ** -0.5)
        m = jnp.max(s, axis=-1, keepdims=True)
        p = jnp.exp(s - m)
        l = jnp.sum(p, axis=-1, keepdims=True)
        o_ref[:, cols] = (_dot(p.astype(BF16), v_ref[:, cols]) / l).astype(o_ref.dtype)


def _mem_attention(q, kv, batch, seq, mem_len, q_tile=1024):
    q_tile = min(q_tile, seq)
    tiles = seq // q_tile
    return pl.pallas_call(
        _memattn_kernel,
        grid=(batch, tiles),
        in_specs=[
            pl.BlockSpec((q_tile, D_MODEL), lambda b, i: (b * tiles + i, 0)),
            pl.BlockSpec((mem_len, D_MODEL), lambda b, i: (b, 0)),
            pl.BlockSpec((mem_len, D_MODEL), lambda b, i: (b, 1)),
        ],
        out_specs=pl.BlockSpec((q_tile, D_MODEL), lambda b, i: (b * tiles + i, 0)),
        out_shape=jax.ShapeDtypeStruct((batch * seq, D_MODEL), BF16),
        compiler_params=_params("parallel", "parallel"),
        name="mem_attention",
    )(q, kv, kv)


def _prep_ffn(w_in, w_out):
    return w_in.astype(BF16), w_out.astype(BF16)


def _mem_block(x, mem2, w_q, w_kv, w_o, g, b, batch, seq, mem_len):
    q = _project(x, w_q.astype(BF16), BF16)
    kv = _project(mem2, w_kv.astype(BF16), BF16)
    o = _mem_attention(q, kv, batch, seq, mem_len)
    return _outproj_deepnorm([o], [w_o], x, g, b)


def kernel(x, mem, l0_ffn1_w_in, l0_ffn1_w_out, l0_ln1_g, l0_ln1_b, l0_mix_w_in, l0_gmlp_ln_g, l0_gmlp_ln_b, l0_gmlp_w_s, l0_gmlp_b_s, l0_mix_w_out, l0_ln2_g, l0_ln2_b, l0_mem_w_q, l0_mem_w_kv, l0_mem_w_o, l0_ln3_g, l0_ln3_b, l0_ffn2_w_in, l0_ffn2_w_out, l0_ln4_g, l0_ln4_b, l1_ffn1_w_in, l1_ffn1_w_out, l1_ln1_g, l1_ln1_b, l1_mix_w_in, l1_mix_w_out, l1_ln2_g, l1_ln2_b, l1_mem_w_q, l1_mem_w_kv, l1_mem_w_o, l1_ln3_g, l1_ln3_b, l1_ffn2_w_in, l1_ffn2_w_out, l1_ln4_g, l1_ln4_b):
    batch, seq, d = x.shape
    mem_len = mem.shape[1]
    h = x.reshape(batch * seq, d)
    mem2 = mem.reshape(batch * mem_len, d)

    h = _ffn_deepnorm(h, *_prep_ffn(l0_ffn1_w_in, l0_ffn1_w_out), l0_ln1_g, l0_ln1_b)
    qkv = _project(h, l0_mix_w_in[:, :A_QKV_WIDTH].astype(BF16), F32)
    uv = _project(h, l0_mix_w_in[:, A_QKV_WIDTH:].astype(BF16), F32)
    a_out = _dilated_attention(qkv, batch, seq)
    b_out = _gmlp(uv, l0_gmlp_ln_g, l0_gmlp_ln_b, l0_gmlp_w_s, l0_gmlp_b_s)
    a_width = a_out.shape[1]
    w_mix_out = [l0_mix_w_out[:a_width], l0_mix_w_out[a_width:]]
    h = _outproj_deepnorm([a_out, b_out], w_mix_out, h, l0_ln2_g, l0_ln2_b)
    h = _mem_block(h, mem2, l0_mem_w_q, l0_mem_w_kv, l0_mem_w_o, l0_ln3_g, l0_ln3_b, batch, seq, mem_len)
    h = _ffn_deepnorm(h, *_prep_ffn(l0_ffn2_w_in, l0_ffn2_w_out), l0_ln4_g, l0_ln4_b)

    h = _ffn_deepnorm(h, *_prep_ffn(l1_ffn1_w_in, l1_ffn1_w_out), l1_ln1_g, l1_ln1_b)
    qkv = _project(h, l1_mix_w_in.astype(BF16), BF16)
    o = _moba(qkv, batch, seq)
    h = _outproj_deepnorm([o], [l1_mix_w_out], h, l1_ln2_g, l1_ln2_b)
    h = _mem_block(h, mem2, l1_mem_w_q, l1_mem_w_kv, l1_mem_w_o, l1_ln3_g, l1_ln3_b, batch, seq, mem_len)
    h = _ffn_deepnorm(h, *_prep_ffn(l1_ffn2_w_in, l1_ffn2_w_out), l1_ln4_g, l1_ln4_b)
    return h.reshape(batch, seq, d)
```

```python
import functools

import jax
import jax.numpy as jnp
from jax import lax
from jax.experimental import pallas as pl
from jax.experimental.pallas import tpu as pltpu

D_MODEL = 2048
DEPTH = 2
HEAD_DIM = 64
HEAD_PAIR = 2 * HEAD_DIM
DIL_GROUPS = ((128, 1), (512, 4), (2048, 16))
A_HEADS_PER_GROUP = 8
A_HEADS = A_HEADS_PER_GROUP * len(DIL_GROUPS)
BAND_BLOCK = 128
CHUNK = 128
B_GROUPS = 8
B_WIDTH = 1024
B_GROUP_DIM = B_WIDTH // B_GROUPS
C_HEADS = D_MODEL // HEAD_DIM
MOBA_BLOCK = 256
MOBA_TOPK = 3
X_HEADS = 4
X_HEAD_DIM = D_MODEL // X_HEADS
D_FF = 5504
DEEPNORM_ALPHA = (2 * DEPTH) ** 0.25
LN_EPS = 1e-5
A_QKV_WIDTH = 3 * A_HEADS * HEAD_DIM

LANES = 128
FF_TILE = 512
ROW_TILE = 512
VMEM_LIMIT = 56 * 1024 * 1024
NEG_BIG = -1e30
M_INIT = -1e29

BF16 = jnp.bfloat16
F32 = jnp.float32


def _params(*sem):
    return pltpu.CompilerParams(dimension_semantics=sem, vmem_limit_bytes=VMEM_LIMIT)


def _dot(a, b):
    return jnp.dot(a, b, preferred_element_type=F32)


def _dot_nt(a, b):
    return lax.dot_general(a, b, (((1,), (1,)), ((), ())), preferred_element_type=F32)


def _layer_norm(y, g, b):
    mu = jnp.mean(y, axis=-1, keepdims=True)
    yc = y - mu
    var = jnp.mean(yc * yc, axis=-1, keepdims=True)
    return yc * lax.rsqrt(var + LN_EPS) * g + b


FF_FULL = D_FF // FF_TILE
FF_TAIL = D_FF - FF_FULL * FF_TILE
assert FF_TAIL % LANES == 0 and FF_TAIL > 0
FFN_TILES_PER_REGION = 4


def _ffn_stream_kernel(x_ref, w_in_hbm, w_out_hbm, g_ref, b_ref, o_ref,
                       xb_ref, acc_ref, y_ref, wg0, wu0, wo0, wg_buf, wu_buf, wo_buf, sem0, sems):
    i = pl.program_id(0)
    n_tiles = pl.num_programs(0) - 1

    def tile_copies(f, slot, width=FF_TILE, resident=False):
        start = f * FF_TILE
        if not isinstance(f, int):
            start = pl.multiple_of(start, FF_TILE)
        cols = pl.ds(0, width)
        dst = (wg0, wu0, wo0) if resident else (wg_buf.at[slot], wu_buf.at[slot], wo_buf.at[slot])
        sem = (lambda k: sem0.at[k]) if resident else (lambda k: sems.at[slot, k])
        return (
            pltpu.make_async_copy(w_in_hbm.at[:, pl.ds(start, width)], dst[0].at[:, cols], sem(0)),
            pltpu.make_async_copy(w_in_hbm.at[:, pl.ds(D_FF + start, width)], dst[1].at[:, cols], sem(1)),
            pltpu.make_async_copy(w_out_hbm.at[pl.ds(start, width), :], dst[2].at[cols, :], sem(2)),
        )

    def contribution(wg, wu, wo, width=FF_TILE):
        xb = xb_ref[...]
        gate = _dot(xb, wg[:, :width])
        up = _dot(xb, wu[:, :width])
        act = (gate * jax.nn.sigmoid(gate)) * up
        return _dot(act.astype(BF16), wo[:width, :])

    @pl.when(i == 0)
    def _():
        for c in tile_copies(0, 0, resident=True):
            c.start()
        y_ref[...] = jnp.zeros_like(y_ref)
        for c in tile_copies(0, 0, resident=True):
            c.wait()

    @pl.when(i < n_tiles)
    def _():
        for c in tile_copies(1, 1):
            c.start()
        xb_ref[...] = x_ref[...].astype(BF16)
        o_ref[...] = _layer_norm(y_ref[...], g_ref[...], b_ref[...])
        acc_ref[...] = contribution(wg0, wu0, wo0)

        def streamed_tile(f, next_width):
            slot = f % 2
            for c in tile_copies(f + 1, 1 - slot, next_width):
                c.start()
            for c in tile_copies(f, slot):
                c.wait()
            acc_ref[...] += contribution(wg_buf.at[slot], wu_buf.at[slot], wo_buf.at[slot])

        def full_tile(f, carry):
            streamed_tile(f, FF_TILE)
            return carry

        lax.fori_loop(1, FF_FULL - 1, full_tile, 0, unroll=FFN_TILES_PER_REGION)
        streamed_tile(FF_FULL - 1, FF_TAIL)
        tail_slot = FF_FULL % 2
        for c in tile_copies(FF_FULL, tail_slot, FF_TAIL):
            c.wait()
        ffn = acc_ref[...] + contribution(wg_buf.at[tail_slot], wu_buf.at[tail_slot], wo_buf.at[tail_slot], FF_TAIL)
        y_ref[...] = DEEPNORM_ALPHA * x_ref[...] + 0.5 * ffn

    @pl.when(i == n_tiles)
    def _():
        o_ref[...] = _layer_norm(y_ref[...], g_ref[...], b_ref[...])


def _ffn_deepnorm(x, w_in, w_out, g, b):
    n, d = x.shape
    n_tiles = n // ROW_TILE
    row_tile = lambda i: (jnp.minimum(i, n_tiles - 1), 0)
    return pl.pallas_call(
        _ffn_stream_kernel,
        grid=(n_tiles + 1,),
        in_specs=[
            pl.BlockSpec((ROW_TILE, d), row_tile),
            pl.BlockSpec(memory_space=pl.ANY),
            pl.BlockSpec(memory_space=pl.ANY),
            pl.BlockSpec((1, d), lambda i: (0, 0)),
            pl.BlockSpec((1, d), lambda i: (0, 0)),
        ],
        out_specs=pl.BlockSpec((ROW_TILE, d), lambda i: (jnp.maximum(i - 1, 0), 0)),
        out_shape=jax.ShapeDtypeStruct((n, d), F32),
        scratch_shapes=[
            pltpu.VMEM((ROW_TILE, d), BF16),
            pltpu.VMEM((ROW_TILE, d), F32),
            pltpu.VMEM((ROW_TILE, d), F32),
            pltpu.VMEM((d, FF_TILE), BF16),
            pltpu.VMEM((d, FF_TILE), BF16),
            pltpu.VMEM((FF_TILE, d), BF16),
            pltpu.VMEM((2, d, FF_TILE), BF16),
            pltpu.VMEM((2, d, FF_TILE), BF16),
            pltpu.VMEM((2, FF_TILE, d), BF16),
            pltpu.SemaphoreType.DMA((3,)),
            pltpu.SemaphoreType.DMA((2, 3)),
        ],
        compiler_params=_params("arbitrary"),
        name="ffn_deepnorm",
    )(x, w_in, w_out, g.reshape(1, d), b.reshape(1, d))


def _proj_kernel(x_ref, w_ref, o_ref):
    o_ref[...] = _dot(x_ref[...].astype(BF16), w_ref[...]).astype(o_ref.dtype)


PROJ_MAX_COLS = 2048


def _project(x, w, out_dtype, row_tile=1024):
    n, k = x.shape
    m = w.shape[1]
    row_tile = min(row_tile, n)
    col_tile = max(c for c in range(LANES, PROJ_MAX_COLS + 1, LANES) if m % c == 0)
    return pl.pallas_call(
        _proj_kernel,
        grid=(n // row_tile, m // col_tile),
        in_specs=[
            pl.BlockSpec((row_tile, k), lambda i, j: (i, 0)),
            pl.BlockSpec((k, col_tile), lambda i, j: (0, j)),
        ],
        out_specs=pl.BlockSpec((row_tile, col_tile), lambda i, j: (i, j)),
        out_shape=jax.ShapeDtypeStruct((n, m), out_dtype),
        compiler_params=_params("parallel", "arbitrary"),
        name="project",
    )(x, w)


OUTPROJ_CHUNK = 256


def _outproj_kernel(*refs, n_in):
    a_refs, w_refs = refs[:n_in], refs[n_in:2 * n_in]
    x_ref, g_ref, b_ref, o_ref = refs[2 * n_in:]
    for c in range(ROW_TILE // OUTPROJ_CHUNK):
        rows = slice(c * OUTPROJ_CHUNK, (c + 1) * OUTPROJ_CHUNK)
        fx = _dot(a_refs[0][rows, :], w_refs[0][...])
        for a_ref, w_ref in zip(a_refs[1:], w_refs[1:]):
            fx = fx + _dot(a_ref[rows, :], w_ref[...])
        y = DEEPNORM_ALPHA * x_ref[rows, :] + fx
        o_ref[rows, :] = _layer_norm(y, g_ref[...], b_ref[...])


def _outproj_deepnorm(a_list, w_list, x, g, b):
    n, d = x.shape
    n_in = len(a_list)
    in_specs = [pl.BlockSpec((ROW_TILE, a.shape[1]), lambda i: (i, 0)) for a in a_list]
    in_specs += [pl.BlockSpec(w.shape, lambda i: (0, 0)) for w in w_list]
    in_specs += [
        pl.BlockSpec((ROW_TILE, d), lambda i: (i, 0)),
        pl.BlockSpec((1, d), lambda i: (0, 0)),
        pl.BlockSpec((1, d), lambda i: (0, 0)),
    ]
    return pl.pallas_call(
        functools.partial(_outproj_kernel, n_in=n_in),
        grid=(n // ROW_TILE,),
        in_specs=in_specs,
        out_specs=pl.BlockSpec((ROW_TILE, d), lambda i: (i, 0)),
        out_shape=jax.ShapeDtypeStruct((n, d), F32),
        compiler_params=_params("parallel"),
        name="outproj_deepnorm",
    )(*a_list, *w_list, x, g.reshape(1, d), b.reshape(1, d))


def _dilated_kernel(*refs, seq):
    n_g = len(DIL_GROUPS)
    qkv_refs = refs[:3 * n_g]
    o_ref = refs[3 * n_g]
    o_sc, lse_sc, bias_sc = refs[3 * n_g + 1:]

    bb = BAND_BLOCK
    qi = lax.broadcasted_iota(jnp.int32, (bb, 2 * bb), 0)
    ki = lax.broadcasted_iota(jnp.int32, (bb, 2 * bb), 1)
    for first, delta in ((0, bb), (1, 0)):
        off = qi + delta - ki
        bias_sc[first] = jnp.where((off >= 0) & (off <= bb), 0.0, -jnp.inf).astype(F32)

    lane = lax.broadcasted_iota(jnp.int32, (bb, HEAD_PAIR), 1)
    head0 = lane < HEAD_DIM
    scale = HEAD_DIM ** -0.5

    for g, (window, dil) in enumerate(DIL_GROUPS):
        assert window // dil == bb
        q_ref, k_ref, v_ref = qkv_refs[3 * g:3 * g + 3]
        n_blk = seq // (dil * bb)

        def body(it, carry, q_ref=q_ref, k_ref=k_ref, v_ref=v_ref, dil=dil, g=g):
            blk = it // dil
            phase = it - blk * dil
            is_first = jnp.where(blk == 0, 1, 0)
            q_start = blk * (bb * dil) + phase
            k_start = jnp.maximum(blk - 1, 0) * (bb * dil) + phase
            if dil == 1:
                q_rows, k_rows = pl.ds(q_start, bb), pl.ds(k_start, 2 * bb)
            else:
                q_rows = pl.ds(q_start, bb, stride=dil)
                k_rows = pl.ds(k_start, 2 * bb, stride=dil)
            q = q_ref[q_rows, :] * scale
            k = k_ref[k_rows, :].astype(BF16)
            v = v_ref[k_rows, :].astype(BF16)
            bias = bias_sc[is_first]
            outs, lses = [], []
            for h0 in (True, False):
                qh = jnp.where(head0 if h0 else ~head0, q, 0.0).astype(BF16)
                s = _dot_nt(qh, k) + bias
                m = jnp.max(s, axis=-1, keepdims=True)
                p = jnp.exp(s - m)
                l = jnp.sum(p, axis=-1, keepdims=True)
                outs.append(_dot(p.astype(BF16), v) / l)
                lses.append(m + jnp.log(l))
            o_sc[g, q_rows, :] = jnp.where(head0, outs[0], outs[1])
            lse_sc[g, q_rows, :] = jnp.where(head0, lses[0], lses[1])
            return carry

        lax.fori_loop(0, n_blk * dil, body, 0, unroll=32)

    lse = [lse_sc[g] for g in range(n_g)]
    top = functools.reduce(jnp.maximum, lse)
    w = [jnp.exp(x - top) for x in lse]
    den = functools.reduce(jnp.add, w)
    acc = functools.reduce(jnp.add, [w[g] * o_sc[g] for g in range(n_g)])
    o_ref[...] = (acc / den).astype(o_ref.dtype)


def _dilated_attention(qkv, batch, seq):
    n_g = len(DIL_GROUPS)
    pairs = A_HEADS_PER_GROUP // 2
    blocks_per_part = A_HEADS * HEAD_DIM // HEAD_PAIR
    in_specs = []
    for g in range(n_g):
        for part in range(3):
            base = part * blocks_per_part + g * pairs
            in_specs.append(pl.BlockSpec((seq, HEAD_PAIR), lambda b, j, base=base: (b, base + j)))
    return pl.pallas_call(
        functools.partial(_dilated_kernel, seq=seq),
        grid=(batch, pairs),
        in_specs=in_specs,
        out_specs=pl.BlockSpec((seq, HEAD_PAIR), lambda b, j: (b, j)),
        out_shape=jax.ShapeDtypeStruct((batch * seq, pairs * HEAD_PAIR), BF16),
        scratch_shapes=[
            pltpu.VMEM((n_g, seq, HEAD_PAIR), F32),
            pltpu.VMEM((n_g, seq, HEAD_PAIR), F32),
            pltpu.VMEM((2, BAND_BLOCK, 2 * BAND_BLOCK), F32),
        ],
        compiler_params=_params("parallel", "parallel"),
        name="dilated_attention",
    )(*([qkv] * (3 * n_g)))


def _gmlp_kernel(u_ref, v_ref, g_ref, b_ref, ws_ref, bias_ref, o_ref, *, chunks):
    u = jax.nn.gelu(u_ref[...])
    v = _layer_norm(jax.nn.gelu(v_ref[...]), g_ref[...], b_ref[...]).astype(BF16)
    row = lax.broadcasted_iota(jnp.int32, (CHUNK, CHUNK), 0)
    col = lax.broadcasted_iota(jnp.int32, (CHUNK, CHUNK), 1)
    tril = row >= col
    bias = bias_ref[...]
    for grp in range(B_GROUPS):
        w = jnp.where(tril, ws_ref[grp], 0.0).astype(BF16)
        cols = slice(grp * B_GROUP_DIM, (grp + 1) * B_GROUP_DIM)
        for c in range(chunks):
            rows = slice(c * CHUNK, (c + 1) * CHUNK)
            mixed = _dot(w, v[rows, cols]) + bias[:, cols]
            o_ref[rows, cols] = (u[rows, cols] * mixed).astype(o_ref.dtype)


def _gmlp(uv, ln_g, ln_b, w_s, b_s, chunks=4):
    n = uv.shape[0]
    rows = chunks * CHUNK
    bias_full = jnp.repeat(b_s.T, B_GROUP_DIM, axis=1)
    return pl.pallas_call(
        functools.partial(_gmlp_kernel, chunks=chunks),
        grid=(n // rows,),
        in_specs=[
            pl.BlockSpec((rows, B_WIDTH), lambda i: (i, 0)),
            pl.BlockSpec((rows, B_WIDTH), lambda i: (i, 1)),
            pl.BlockSpec((1, B_WIDTH), lambda i: (0, 0)),
            pl.BlockSpec((1, B_WIDTH), lambda i: (0, 0)),
            pl.BlockSpec((B_GROUPS, CHUNK, CHUNK), lambda i: (0, 0, 0)),
            pl.BlockSpec((CHUNK, B_WIDTH), lambda i: (0, 0)),
        ],
        out_specs=pl.BlockSpec((rows, B_WIDTH), lambda i: (i, 0)),
        out_shape=jax.ShapeDtypeStruct((n, B_WIDTH), BF16),
        compiler_params=_params("parallel"),
        name="gmlp",
    )(uv, uv, ln_g.reshape(1, B_WIDTH), ln_b.reshape(1, B_WIDTH), w_s, bias_full)


VT_ROWS = HEAD_DIM + 16


def _moba_seq_kernel(q_ref, k_ref, v_ref, o_ref, kmean_sc, qt_sc, vt_sc, selb_sc, *, n_blocks):
    blk = MOBA_BLOCK
    n_rows = kmean_sc.shape[0]
    scale = HEAD_DIM ** -0.5

    kmean_sc[...] = jnp.zeros_like(kmean_sc)
    ones_rows = (lax.broadcasted_iota(jnp.int32, (VT_ROWS - HEAD_DIM, blk), 0) == 0).astype(BF16)
    key_lane = lax.broadcasted_iota(jnp.int32, (blk, HEAD_PAIR), 1)
    head_lanes = [key_lane < HEAD_DIM, key_lane >= HEAD_DIM]
    bias_base = [HEAD_DIM, 0]

    def prep(m, carry):
        rows = pl.ds(m * blk, blk)
        kmean_sc[pl.ds(m, 1), :] = jnp.mean(k_ref[rows, :].astype(F32), axis=0, keepdims=True)
        qt_sc[m] = (jnp.transpose(q_ref[rows, :].astype(F32)) * scale).astype(BF16)
        vt = jnp.transpose(v_ref[rows, :].astype(F32)).astype(BF16)
        for h in range(2):
            vt_sc[m, h] = jnp.concatenate([vt[h * HEAD_DIM:(h + 1) * HEAD_DIM], ones_rows], axis=0)
        return carry

    for m in range(n_blocks):
        prep(m, 0)

    chan = lax.broadcasted_iota(jnp.int32, (HEAD_PAIR, blk), 0)
    head_rows = [chan < HEAD_DIM, chan >= HEAD_DIM]
    kmean = kmean_sc[...].astype(BF16)
    cand = lax.broadcasted_iota(jnp.int32, (n_rows, blk), 0)

    def select(n, carry):
        qt = qt_sc[n]
        for h in range(2):
            gate = _dot(kmean, jnp.where(head_rows[h], qt, 0))
            left = jnp.where(cand < n, gate, -jnp.inf)
            chosen = cand == n
            for _ in range(MOBA_TOPK):
                best = jnp.max(left, axis=0, keepdims=True)
                first = jnp.min(jnp.where(left == best, cand, n_rows), axis=0, keepdims=True)
                take = (cand == first) & (best > -jnp.inf)
                chosen = chosen | (take & (best < jnp.inf))
                left = jnp.where(take, -jnp.inf, left)
            selb_sc[n, h] = jnp.where(chosen, 0.0, NEG_BIG).astype(BF16)
        return carry

    for n in range(n_blocks):
        select(n, 0)

    key_i = lax.broadcasted_iota(jnp.int32, (blk, blk), 0)
    qry_i = lax.broadcasted_iota(jnp.int32, (blk, blk), 1)
    causal = key_i <= qry_i
    pad_rows = jnp.zeros((HEAD_DIM - n_rows, blk), BF16)

    def block_rows(m):
        start = m * blk
        return pl.ds(start if isinstance(m, int) else pl.multiple_of(start, blk), blk)

    def softmax_stage(m_run, st):
        m_new = jnp.maximum(m_run, jnp.max(st, axis=0, keepdims=True))
        return m_new, jnp.exp(m_run - m_new), jnp.exp(st - m_new).astype(BF16)

    def query_block(n, carry):
        qt = qt_sc[n]
        w_aug = [jnp.concatenate([qt[:HEAD_DIM], selb_sc[n, 0], pad_rows], axis=0),
                 jnp.concatenate([selb_sc[n, 1], pad_rows, qt[HEAD_DIM:]], axis=0)]

        def scores(m):
            k_m = k_ref[block_rows(m), :]
            out = []
            for h in range(2):
                k_aug = jnp.where(head_lanes[h], k_m, (key_lane == bias_base[h] + m).astype(BF16))
                out.append(_dot(k_aug, w_aug[h]))
            return out

        m_run = [jnp.full((1, blk), M_INIT, F32)] * 2
        acc = [jnp.zeros((VT_ROWS, blk), F32)] * 2
        st = scores(0)
        for i in range(n):
            st_next = scores(i + 1)
            for h in range(2):
                m_new, alpha, p = softmax_stage(m_run[h], st[h])
                m_run = m_run[:h] + [m_new] + m_run[h + 1:]
                acc = acc[:h] + [alpha * acc[h] + _dot(vt_sc[i, h], p)] + acc[h + 1:]
            st = st_next
        outs = []
        for h in range(2):
            _, alpha, p = softmax_stage(m_run[h], jnp.where(causal, st[h], NEG_BIG))
            acc_h = alpha * acc[h] + _dot(vt_sc[n, h], p)
            outs.append(acc_h[:HEAD_DIM] / acc_h[HEAD_DIM:HEAD_DIM + 1])
        o_ref[block_rows(n), :] = jnp.transpose(jnp.concatenate(outs, axis=0)).astype(o_ref.dtype)
        return carry

    for n in range(n_blocks):
        query_block(n, 0)


def _moba(qkv, batch, seq):
    assert seq % MOBA_BLOCK == 0
    n_blocks = seq // MOBA_BLOCK
    n_rows = -(-n_blocks // 16) * 16
    assert n_rows <= HEAD_DIM
    pairs = C_HEADS // 2
    return pl.pallas_call(
        functools.partial(_moba_seq_kernel, n_blocks=n_blocks),
        grid=(batch, pairs),
        in_specs=[
            pl.BlockSpec((seq, HEAD_PAIR), lambda b, j: (b, j)),
            pl.BlockSpec((seq, HEAD_PAIR), lambda b, j: (b, pairs + j)),
            pl.BlockSpec((seq, HEAD_PAIR), lambda b, j: (b, 2 * pairs + j)),
        ],
        out_specs=pl.BlockSpec((seq, HEAD_PAIR), lambda b, j: (b, j)),
        out_shape=jax.ShapeDtypeStruct((batch * seq, pairs * HEAD_PAIR), BF16),
        scratch_shapes=[
            pltpu.VMEM((n_rows, HEAD_PAIR), F32),
            pltpu.VMEM((n_blocks, HEAD_PAIR, MOBA_BLOCK), BF16),
            pltpu.VMEM((n_blocks, 2, VT_ROWS, MOBA_BLOCK), BF16),
            pltpu.VMEM((n_blocks, 2, n_rows, MOBA_BLOCK), BF16),
        ],
        compiler_params=_params("parallel", "parallel"),
        name="moba",
    )(qkv, qkv, qkv)


def _memattn_kernel(x_ref, wq_ref, k_ref, v_ref, o_ref):
    xb = x_ref[...].astype(BF16)
    for h in range(X_HEADS):
        cols = slice(h * X_HEAD_DIM, (h + 1) * X_HEAD_DIM)
        q = _dot(xb, wq_ref[:, cols]).astype(BF16)
        s = _dot_nt(q, k_ref[:, cols]) * (X_HEAD_DIM ** -0.5)
        m = jnp.max(s, axis=-1, keepdims=True)
        p = jnp.exp(s - m)
        l = jnp.sum(p, axis=-1, keepdims=True)
        o_ref[:, cols] = (_dot(p.astype(BF16), v_ref[:, cols]) / l).astype(o_ref.dtype)


def _mem_attention(x, w_q, kv, batch, seq, mem_len, q_tile=1024):
    q_tile = min(q_tile, seq)
    tiles = seq // q_tile
    return pl.pallas_call(
        _memattn_kernel,
        grid=(batch, tiles),
        in_specs=[
            pl.BlockSpec((q_tile, D_MODEL), lambda b, i: (b * tiles + i, 0)),
            pl.BlockSpec((D_MODEL, D_MODEL), lambda b, i: (0, 0)),
            pl.BlockSpec((mem_len, D_MODEL), lambda b, i: (b, 0)),
            pl.BlockSpec((mem_len, D_MODEL), lambda b, i: (b, 1)),
        ],
        out_specs=pl.BlockSpec((q_tile, D_MODEL), lambda b, i: (b * tiles + i, 0)),
        out_shape=jax.ShapeDtypeStruct((batch * seq, D_MODEL), BF16),
        compiler_params=_params("parallel", "parallel"),
        name="mem_attention",
    )(x, w_q, kv, kv)


def _prep_ffn(w_in, w_out):
    return w_in.astype(BF16), w_out.astype(BF16)


def _mem_block(x, mem2, w_q, w_kv, w_o, g, b, batch, seq, mem_len):
    kv = _project(mem2, w_kv.astype(BF16), BF16)
    o = _mem_attention(x, w_q.astype(BF16), kv, batch, seq, mem_len)
    return _outproj_deepnorm([o], [w_o.astype(BF16)], x, g, b)


def kernel(x, mem, l0_ffn1_w_in, l0_ffn1_w_out, l0_ln1_g, l0_ln1_b, l0_mix_w_in, l0_gmlp_ln_g, l0_gmlp_ln_b, l0_gmlp_w_s, l0_gmlp_b_s, l0_mix_w_out, l0_ln2_g, l0_ln2_b, l0_mem_w_q, l0_mem_w_kv, l0_mem_w_o, l0_ln3_g, l0_ln3_b, l0_ffn2_w_in, l0_ffn2_w_out, l0_ln4_g, l0_ln4_b, l1_ffn1_w_in, l1_ffn1_w_out, l1_ln1_g, l1_ln1_b, l1_mix_w_in, l1_mix_w_out, l1_ln2_g, l1_ln2_b, l1_mem_w_q, l1_mem_w_kv, l1_mem_w_o, l1_ln3_g, l1_ln3_b, l1_ffn2_w_in, l1_ffn2_w_out, l1_ln4_g, l1_ln4_b):
    batch, seq, d = x.shape
    mem_len = mem.shape[1]
    h = x.reshape(batch * seq, d)
    mem2 = mem.reshape(batch * mem_len, d)

    h = _ffn_deepnorm(h, *_prep_ffn(l0_ffn1_w_in, l0_ffn1_w_out), l0_ln1_g, l0_ln1_b)
    qkv = _project(h, l0_mix_w_in[:, :A_QKV_WIDTH].astype(BF16), F32)
    uv = _project(h, l0_mix_w_in[:, A_QKV_WIDTH:].astype(BF16), F32)
    a_out = _dilated_attention(qkv, batch, seq)
    b_out = _gmlp(uv, l0_gmlp_ln_g, l0_gmlp_ln_b, l0_gmlp_w_s, l0_gmlp_b_s)
    a_width = a_out.shape[1]
    w_mix_out = [l0_mix_w_out[:a_width].astype(BF16), l0_mix_w_out[a_width:].astype(BF16)]
    h = _outproj_deepnorm([a_out, b_out], w_mix_out, h, l0_ln2_g, l0_ln2_b)
    h = _mem_block(h, mem2, l0_mem_w_q, l0_mem_w_kv, l0_mem_w_o, l0_ln3_g, l0_ln3_b, batch, seq, mem_len)
    h = _ffn_deepnorm(h, *_prep_ffn(l0_ffn2_w_in, l0_ffn2_w_out), l0_ln4_g, l0_ln4_b)

    h = _ffn_deepnorm(h, *_prep_ffn(l1_ffn1_w_in, l1_ffn1_w_out), l1_ln1_g, l1_ln1_b)
    qkv = _project(h, l1_mix_w_in.astype(BF16), BF16)
    o = _moba(qkv, batch, seq)
    h = _outproj_deepnorm([o], [l1_mix_w_out.astype(BF16)], h, l1_ln2_g, l1_ln2_b)
    h = _mem_block(h, mem2, l1_mem_w_q, l1_mem_w_kv, l1_mem_w_o, l1_ln3_g, l1_ln3_b, batch, seq, mem_len)
    h = _ffn_deepnorm(h, *_prep_ffn(l1_ffn2_w_in, l1_ffn2_w_out), l1_ln4_g, l1_ln4_b)
    return h.reshape(batch, seq, d)
```

```python
import functools

import jax
import jax.numpy as jnp
from jax import lax
from jax.experimental import pallas as pl
from jax.experimental.pallas import tpu as pltpu

D_MODEL = 2048
DEPTH = 2
HEAD_DIM = 64
HEAD_PAIR = 2 * HEAD_DIM
DIL_GROUPS = ((128, 1), (512, 4), (2048, 16))
A_HEADS_PER_GROUP = 8
A_HEADS = A_HEADS_PER_GROUP * len(DIL_GROUPS)
BAND_BLOCK = 128
CHUNK = 128
B_GROUPS = 8
B_WIDTH = 1024
B_GROUP_DIM = B_WIDTH // B_GROUPS
C_HEADS = D_MODEL // HEAD_DIM
MOBA_BLOCK = 256
MOBA_TOPK = 3
X_HEADS = 4
X_HEAD_DIM = D_MODEL // X_HEADS
D_FF = 5504
DEEPNORM_ALPHA = (2 * DEPTH) ** 0.25
LN_EPS = 1e-5
A_QKV_WIDTH = 3 * A_HEADS * HEAD_DIM

LANES = 128
FF_TILE = 512
ROW_TILE = 512
VMEM_LIMIT = 56 * 1024 * 1024
NEG_BIG = -1e30
M_INIT = -1e29

BF16 = jnp.bfloat16
F32 = jnp.float32


def _params(*sem):
    return pltpu.CompilerParams(dimension_semantics=sem, vmem_limit_bytes=VMEM_LIMIT)


def _dot(a, b):
    return jnp.dot(a, b, preferred_element_type=F32)


def _dot_nt(a, b):
    return lax.dot_general(a, b, (((1,), (1,)), ((), ())), preferred_element_type=F32)


def _layer_norm(y, g, b):
    mu = jnp.mean(y, axis=-1, keepdims=True)
    yc = y - mu
    var = jnp.mean(yc * yc, axis=-1, keepdims=True)
    return yc * lax.rsqrt(var + LN_EPS) * g + b


FF_FULL = D_FF // FF_TILE
FF_TAIL = D_FF - FF_FULL * FF_TILE
assert FF_TAIL % LANES == 0 and FF_TAIL > 0
FFN_TILES_PER_REGION = 4


def _ffn_stream_kernel(x_ref, w_in_hbm, w_out_hbm, g_ref, b_ref, o_ref,
                       xb_ref, acc_ref, y_ref, wg0, wu0, wo0, wg_buf, wu_buf, wo_buf, sem0, sems):
    i = pl.program_id(0)
    n_tiles = pl.num_programs(0) - 1

    def tile_copies(f, slot, width=FF_TILE, resident=False):
        start = f * FF_TILE
        if not isinstance(f, int):
            start = pl.multiple_of(start, FF_TILE)
        cols = pl.ds(0, width)
        dst = (wg0, wu0, wo0) if resident else (wg_buf.at[slot], wu_buf.at[slot], wo_buf.at[slot])
        sem = (lambda k: sem0.at[k]) if resident else (lambda k: sems.at[slot, k])
        return (
            pltpu.make_async_copy(w_in_hbm.at[:, pl.ds(start, width)], dst[0].at[:, cols], sem(0)),
            pltpu.make_async_copy(w_in_hbm.at[:, pl.ds(D_FF + start, width)], dst[1].at[:, cols], sem(1)),
            pltpu.make_async_copy(w_out_hbm.at[pl.ds(start, width), :], dst[2].at[cols, :], sem(2)),
        )

    def contribution(wg, wu, wo, width=FF_TILE):
        xb = xb_ref[...]
        gate = _dot(xb, wg[:, :width])
        up = _dot(xb, wu[:, :width])
        act = (gate * jax.nn.sigmoid(gate)) * up
        return _dot(act.astype(BF16), wo[:width, :])

    @pl.when(i == 0)
    def _():
        for c in tile_copies(0, 0, resident=True):
            c.start()
        y_ref[...] = jnp.zeros_like(y_ref)
        for c in tile_copies(0, 0, resident=True):
            c.wait()

    @pl.when(i < n_tiles)
    def _():
        for c in tile_copies(1, 1):
            c.start()
        xb_ref[...] = x_ref[...].astype(BF16)
        o_ref[...] = _layer_norm(y_ref[...], g_ref[...], b_ref[...])
        acc_ref[...] = contribution(wg0, wu0, wo0)

        def streamed_tile(f, next_width):
            slot = f % 2
            for c in tile_copies(f + 1, 1 - slot, next_width):
                c.start()
            for c in tile_copies(f, slot):
                c.wait()
            acc_ref[...] += contribution(wg_buf.at[slot], wu_buf.at[slot], wo_buf.at[slot])

        def full_tile(f, carry):
            streamed_tile(f, FF_TILE)
            return carry

        lax.fori_loop(1, FF_FULL - 1, full_tile, 0, unroll=FFN_TILES_PER_REGION)
        streamed_tile(FF_FULL - 1, FF_TAIL)
        tail_slot = FF_FULL % 2
        for c in tile_copies(FF_FULL, tail_slot, FF_TAIL):
            c.wait()
        ffn = acc_ref[...] + contribution(wg_buf.at[tail_slot], wu_buf.at[tail_slot], wo_buf.at[tail_slot], FF_TAIL)
        y_ref[...] = DEEPNORM_ALPHA * x_ref[...] + 0.5 * ffn

    @pl.when(i == n_tiles)
    def _():
        o_ref[...] = _layer_norm(y_ref[...], g_ref[...], b_ref[...])


def _ffn_deepnorm(x, w_in, w_out, g, b):
    n, d = x.shape
    n_tiles = n // ROW_TILE
    row_tile = lambda i: (jnp.minimum(i, n_tiles - 1), 0)
    return pl.pallas_call(
        _ffn_stream_kernel,
        grid=(n_tiles + 1,),
        in_specs=[
            pl.BlockSpec((ROW_TILE, d), row_tile),
            pl.BlockSpec(memory_space=pl.ANY),
            pl.BlockSpec(memory_space=pl.ANY),
            pl.BlockSpec((1, d), lambda i: (0, 0)),
            pl.BlockSpec((1, d), lambda i: (0, 0)),
        ],
        out_specs=pl.BlockSpec((ROW_TILE, d), lambda i: (jnp.maximum(i - 1, 0), 0)),
        out_shape=jax.ShapeDtypeStruct((n, d), F32),
        scratch_shapes=[
            pltpu.VMEM((ROW_TILE, d), BF16),
            pltpu.VMEM((ROW_TILE, d), F32),
            pltpu.VMEM((ROW_TILE, d), F32),
            pltpu.VMEM((d, FF_TILE), BF16),
            pltpu.VMEM((d, FF_TILE), BF16),
            pltpu.VMEM((FF_TILE, d), BF16),
            pltpu.VMEM((2, d, FF_TILE), BF16),
            pltpu.VMEM((2, d, FF_TILE), BF16),
            pltpu.VMEM((2, FF_TILE, d), BF16),
            pltpu.SemaphoreType.DMA((3,)),
            pltpu.SemaphoreType.DMA((2, 3)),
        ],
        compiler_params=_params("arbitrary"),
        name="ffn_deepnorm",
    )(x, w_in, w_out, g.reshape(1, d), b.reshape(1, d))


def _proj_kernel(x_ref, w_ref, o_ref):
    o_ref[...] = _dot(x_ref[...].astype(BF16), w_ref[...]).astype(o_ref.dtype)


PROJ_MAX_COLS = 2048


def _project(x, w, out_dtype, row_tile=1024):
    n, k = x.shape
    m = w.shape[1]
    row_tile = min(row_tile, n)
    col_tile = max(c for c in range(LANES, PROJ_MAX_COLS + 1, LANES) if m % c == 0)
    return pl.pallas_call(
        _proj_kernel,
        grid=(n // row_tile, m // col_tile),
        in_specs=[
            pl.BlockSpec((row_tile, k), lambda i, j: (i, 0)),
            pl.BlockSpec((k, col_tile), lambda i, j: (0, j)),
        ],
        out_specs=pl.BlockSpec((row_tile, col_tile), lambda i, j: (i, j)),
        out_shape=jax.ShapeDtypeStruct((n, m), out_dtype),
        compiler_params=_params("parallel", "arbitrary"),
        name="project",
    )(x, w)


OUTPROJ_CHUNK = 256


def _outproj_kernel(*refs, n_in):
    a_refs, w_refs = refs[:n_in], refs[n_in:2 * n_in]
    x_ref, g_ref, b_ref, o_ref = refs[2 * n_in:]
    for c in range(ROW_TILE // OUTPROJ_CHUNK):
        rows = slice(c * OUTPROJ_CHUNK, (c + 1) * OUTPROJ_CHUNK)
        fx = _dot(a_refs[0][rows, :], w_refs[0][...])
        for a_ref, w_ref in zip(a_refs[1:], w_refs[1:]):
            fx = fx + _dot(a_ref[rows, :], w_ref[...])
        y = DEEPNORM_ALPHA * x_ref[rows, :] + fx
        o_ref[rows, :] = _layer_norm(y, g_ref[...], b_ref[...])


def _outproj_deepnorm(a_list, w_list, x, g, b):
    n, d = x.shape
    n_in = len(a_list)
    in_specs = [pl.BlockSpec((ROW_TILE, a.shape[1]), lambda i: (i, 0)) for a in a_list]
    in_specs += [pl.BlockSpec(w.shape, lambda i: (0, 0)) for w in w_list]
    in_specs += [
        pl.BlockSpec((ROW_TILE, d), lambda i: (i, 0)),
        pl.BlockSpec((1, d), lambda i: (0, 0)),
        pl.BlockSpec((1, d), lambda i: (0, 0)),
    ]
    return pl.pallas_call(
        functools.partial(_outproj_kernel, n_in=n_in),
        grid=(n // ROW_TILE,),
        in_specs=in_specs,
        out_specs=pl.BlockSpec((ROW_TILE, d), lambda i: (i, 0)),
        out_shape=jax.ShapeDtypeStruct((n, d), F32),
        compiler_params=_params("parallel"),
        name="outproj_deepnorm",
    )(*a_list, *w_list, x, g.reshape(1, d), b.reshape(1, d))


def _dilated_kernel(*refs, seq):
    n_g = len(DIL_GROUPS)
    qkv_refs = refs[:3 * n_g]
    o_ref = refs[3 * n_g]
    o_sc, lse_sc, bias_sc = refs[3 * n_g + 1:]

    bb = BAND_BLOCK
    qi = lax.broadcasted_iota(jnp.int32, (bb, 2 * bb), 0)
    ki = lax.broadcasted_iota(jnp.int32, (bb, 2 * bb), 1)
    for first, delta in ((0, bb), (1, 0)):
        off = qi + delta - ki
        bias_sc[first] = jnp.where((off >= 0) & (off <= bb), 0.0, -jnp.inf).astype(F32)

    lane = lax.broadcasted_iota(jnp.int32, (bb, HEAD_PAIR), 1)
    head0 = lane < HEAD_DIM
    scale = HEAD_DIM ** -0.5

    for g, (window, dil) in enumerate(DIL_GROUPS):
        assert window // dil == bb
        q_ref, k_ref, v_ref = qkv_refs[3 * g:3 * g + 3]
        n_blk = seq // (dil * bb)

        def body(it, carry, q_ref=q_ref, k_ref=k_ref, v_ref=v_ref, dil=dil, g=g):
            blk = it // dil
            phase = it - blk * dil
            is_first = jnp.where(blk == 0, 1, 0)
            q_start = blk * (bb * dil) + phase
            k_start = jnp.maximum(blk - 1, 0) * (bb * dil) + phase
            if dil == 1:
                q_rows, k_rows = pl.ds(q_start, bb), pl.ds(k_start, 2 * bb)
            else:
                q_rows = pl.ds(q_start, bb, stride=dil)
                k_rows = pl.ds(k_start, 2 * bb, stride=dil)
            q = q_ref[q_rows, :] * scale
            k = k_ref[k_rows, :].astype(BF16)
            v = v_ref[k_rows, :].astype(BF16)
            bias = bias_sc[is_first]
            outs, lses = [], []
            for h0 in (True, False):
                qh = jnp.where(head0 if h0 else ~head0, q, 0.0).astype(BF16)
                s = _dot_nt(qh, k) + bias
                m = jnp.max(s, axis=-1, keepdims=True)
                p = jnp.exp(s - m)
                l = jnp.sum(p, axis=-1, keepdims=True)
                outs.append(_dot(p.astype(BF16), v) / l)
                lses.append(m + jnp.log(l))
            o_sc[g, q_rows, :] = jnp.where(head0, outs[0], outs[1])
            lse_sc[g, q_rows, :] = jnp.where(head0, lses[0], lses[1])
            return carry

        lax.fori_loop(0, n_blk * dil, body, 0, unroll=32)

    lse = [lse_sc[g] for g in range(n_g)]
    top = functools.reduce(jnp.maximum, lse)
    w = [jnp.exp(x - top) for x in lse]
    den = functools.reduce(jnp.add, w)
    acc = functools.reduce(jnp.add, [w[g] * o_sc[g] for g in range(n_g)])
    o_ref[...] = (acc / den).astype(o_ref.dtype)


def _dilated_attention(qkv, batch, seq):
    n_g = len(DIL_GROUPS)
    pairs = A_HEADS_PER_GROUP // 2
    blocks_per_part = A_HEADS * HEAD_DIM // HEAD_PAIR
    in_specs = []
    for g in range(n_g):
        for part in range(3):
            base = part * blocks_per_part + g * pairs
            in_specs.append(pl.BlockSpec((seq, HEAD_PAIR), lambda b, j, base=base: (b, base + j)))
    return pl.pallas_call(
        functools.partial(_dilated_kernel, seq=seq),
        grid=(batch, pairs),
        in_specs=in_specs,
        out_specs=pl.BlockSpec((seq, HEAD_PAIR), lambda b, j: (b, j)),
        out_shape=jax.ShapeDtypeStruct((batch * seq, pairs * HEAD_PAIR), BF16),
        scratch_shapes=[
            pltpu.VMEM((n_g, seq, HEAD_PAIR), F32),
            pltpu.VMEM((n_g, seq, HEAD_PAIR), F32),
            pltpu.VMEM((2, BAND_BLOCK, 2 * BAND_BLOCK), F32),
        ],
        compiler_params=_params("parallel", "parallel"),
        name="dilated_attention",
    )(*([qkv] * (3 * n_g)))


def _gmlp_kernel(u_ref, v_ref, g_ref, b_ref, ws_ref, bias_ref, o_ref, *, chunks):
    u = jax.nn.gelu(u_ref[...])
    v = _layer_norm(jax.nn.gelu(v_ref[...]), g_ref[...], b_ref[...]).astype(BF16)
    row = lax.broadcasted_iota(jnp.int32, (CHUNK, CHUNK), 0)
    col = lax.broadcasted_iota(jnp.int32, (CHUNK, CHUNK), 1)
    tril = row >= col
    bias = bias_ref[...]
    for grp in range(B_GROUPS):
        w = jnp.where(tril, ws_ref[grp], 0.0).astype(BF16)
        cols = slice(grp * B_GROUP_DIM, (grp + 1) * B_GROUP_DIM)
        for c in range(chunks):
            rows = slice(c * CHUNK, (c + 1) * CHUNK)
            mixed = _dot(w, v[rows, cols]) + bias[:, cols]
            o_ref[rows, cols] = (u[rows, cols] * mixed).astype(o_ref.dtype)


def _gmlp(uv, ln_g, ln_b, w_s, b_s, chunks=4):
    n = uv.shape[0]
    rows = chunks * CHUNK
    bias_full = jnp.repeat(b_s.T, B_GROUP_DIM, axis=1)
    return pl.pallas_call(
        functools.partial(_gmlp_kernel, chunks=chunks),
        grid=(n // rows,),
        in_specs=[
            pl.BlockSpec((rows, B_WIDTH), lambda i: (i, 0)),
            pl.BlockSpec((rows, B_WIDTH), lambda i: (i, 1)),
            pl.BlockSpec((1, B_WIDTH), lambda i: (0, 0)),
            pl.BlockSpec((1, B_WIDTH), lambda i: (0, 0)),
            pl.BlockSpec((B_GROUPS, CHUNK, CHUNK), lambda i: (0, 0, 0)),
            pl.BlockSpec((CHUNK, B_WIDTH), lambda i: (0, 0)),
        ],
        out_specs=pl.BlockSpec((rows, B_WIDTH), lambda i: (i, 0)),
        out_shape=jax.ShapeDtypeStruct((n, B_WIDTH), BF16),
        compiler_params=_params("parallel"),
        name="gmlp",
    )(uv, uv, ln_g.reshape(1, B_WIDTH), ln_b.reshape(1, B_WIDTH), w_s, bias_full)


VT_ROWS = HEAD_DIM + 16


def _moba_seq_kernel(q_ref, k_ref, v_ref, o_ref, kmean_sc, qt_sc, vt_sc, selb_sc, *, n_blocks):
    blk = MOBA_BLOCK
    n_rows = kmean_sc.shape[0]
    scale = HEAD_DIM ** -0.5

    kmean_sc[...] = jnp.zeros_like(kmean_sc)
    ones_rows = (lax.broadcasted_iota(jnp.int32, (VT_ROWS - HEAD_DIM, blk), 0) == 0).astype(BF16)
    key_lane = lax.broadcasted_iota(jnp.int32, (blk, HEAD_PAIR), 1)
    head_lanes = [key_lane < HEAD_DIM, key_lane >= HEAD_DIM]
    bias_base = [HEAD_DIM, 0]

    def prep(m, carry):
        rows = pl.ds(m * blk, blk)
        kmean_sc[pl.ds(m, 1), :] = jnp.mean(k_ref[rows, :].astype(F32), axis=0, keepdims=True)
        qt_sc[m] = (jnp.transpose(q_ref[rows, :].astype(F32)) * scale).astype(BF16)
        vt = jnp.transpose(v_ref[rows, :].astype(F32)).astype(BF16)
        for h in range(2):
            vt_sc[m, h] = jnp.concatenate([vt[h * HEAD_DIM:(h + 1) * HEAD_DIM], ones_rows], axis=0)
        return carry

    for m in range(n_blocks):
        prep(m, 0)

    chan = lax.broadcasted_iota(jnp.int32, (HEAD_PAIR, blk), 0)
    head_rows = [chan < HEAD_DIM, chan >= HEAD_DIM]
    kmean = kmean_sc[...].astype(BF16)
    cand = lax.broadcasted_iota(jnp.int32, (n_rows, blk), 0)

    def select(n, carry):
        qt = qt_sc[n]
        for h in range(2):
            gate = _dot(kmean, jnp.where(head_rows[h], qt, 0))
            left = jnp.where(cand < n, gate, -jnp.inf)
            chosen = cand == n
            for _ in range(MOBA_TOPK):
                best = jnp.max(left, axis=0, keepdims=True)
                first = jnp.min(jnp.where(left == best, cand, n_rows), axis=0, keepdims=True)
                take = (cand == first) & (best > -jnp.inf)
                chosen = chosen | (take & (best < jnp.inf))
                left = jnp.where(take, -jnp.inf, left)
            selb_sc[n, h] = jnp.where(chosen, 0.0, NEG_BIG).astype(BF16)
        return carry

    for n in range(n_blocks):
        select(n, 0)

    key_i = lax.broadcasted_iota(jnp.int32, (blk, blk), 0)
    qry_i = lax.broadcasted_iota(jnp.int32, (blk, blk), 1)
    causal = key_i <= qry_i
    pad_rows = jnp.zeros((HEAD_DIM - n_rows, blk), BF16)

    def block_rows(m):
        start = m * blk
        return pl.ds(start if isinstance(m, int) else pl.multiple_of(start, blk), blk)

    def softmax_stage(m_run, st):
        m_new = jnp.maximum(m_run, jnp.max(st, axis=0, keepdims=True))
        return m_new, jnp.exp(m_run - m_new), jnp.exp(st - m_new).astype(BF16)

    def query_block(n, carry):
        qt = qt_sc[n]
        w_aug = [jnp.concatenate([qt[:HEAD_DIM], selb_sc[n, 0], pad_rows], axis=0),
                 jnp.concatenate([selb_sc[n, 1], pad_rows, qt[HEAD_DIM:]], axis=0)]

        def scores(m):
            k_m = k_ref[block_rows(m), :]
            out = []
            for h in range(2):
                k_aug = jnp.where(head_lanes[h], k_m, (key_lane == bias_base[h] + m).astype(BF16))
                out.append(_dot(k_aug, w_aug[h]))
            return out

        m_run = [jnp.full((1, blk), M_INIT, F32)] * 2
        acc = [jnp.zeros((VT_ROWS, blk), F32)] * 2
        st = scores(0)
        for i in range(n):
            st_next = scores(i + 1)
            for h in range(2):
                m_new, alpha, p = softmax_stage(m_run[h], st[h])
                m_run = m_run[:h] + [m_new] + m_run[h + 1:]
                acc = acc[:h] + [alpha * acc[h] + _dot(vt_sc[i, h], p)] + acc[h + 1:]
            st = st_next
        outs = []
        for h in range(2):
            _, alpha, p = softmax_stage(m_run[h], jnp.where(causal, st[h], NEG_BIG))
            acc_h = alpha * acc[h] + _dot(vt_sc[n, h], p)
            outs.append(acc_h[:HEAD_DIM] / acc_h[HEAD_DIM:HEAD_DIM + 1])
        o_ref[block_rows(n), :] = jnp.transpose(jnp.concatenate(outs, axis=0)).astype(o_ref.dtype)
        return carry

    for n in range(n_blocks):
        query_block(n, 0)


def _moba(qkv, batch, seq):
    assert seq % MOBA_BLOCK == 0
    n_blocks = seq // MOBA_BLOCK
    n_rows = -(-n_blocks // 16) * 16
    assert n_rows <= HEAD_DIM
    pairs = C_HEADS // 2
    return pl.pallas_call(
        functools.partial(_moba_seq_kernel, n_blocks=n_blocks),
        grid=(batch, pairs),
        in_specs=[
            pl.BlockSpec((seq, HEAD_PAIR), lambda b, j: (b, j)),
            pl.BlockSpec((seq, HEAD_PAIR), lambda b, j: (b, pairs + j)),
            pl.BlockSpec((seq, HEAD_PAIR), lambda b, j: (b, 2 * pairs + j)),
        ],
        out_specs=pl.BlockSpec((seq, HEAD_PAIR), lambda b, j: (b, j)),
        out_shape=jax.ShapeDtypeStruct((batch * seq, pairs * HEAD_PAIR), BF16),
        scratch_shapes=[
            pltpu.VMEM((n_rows, HEAD_PAIR), F32),
            pltpu.VMEM((n_blocks, HEAD_PAIR, MOBA_BLOCK), BF16),
            pltpu.VMEM((n_blocks, 2, VT_ROWS, MOBA_BLOCK), BF16),
            pltpu.VMEM((n_blocks, 2, n_rows, MOBA_BLOCK), BF16),
        ],
        compiler_params=_params("parallel", "parallel"),
        name="moba",
    )(qkv, qkv, qkv)


def _memattn_kernel(x_ref, wq_ref, k_ref, v_ref, o_ref):
    xb = x_ref[...].astype(BF16)
    for h in range(X_HEADS):
        cols = slice(h * X_HEAD_DIM, (h + 1) * X_HEAD_DIM)
        q = _dot(xb, wq_ref[:, cols]).astype(BF16)
        s = _dot_nt(q, k_ref[:, cols]) * (X_HEAD_DIM ** -0.5)
        m = jnp.max(s, axis=-1, keepdims=True)
        p = jnp.exp(s - m)
        l = jnp.sum(p, axis=-1, keepdims=True)
        o_ref[:, cols] = (_dot(p.astype(BF16), v_ref[:, cols]) / l).astype(o_ref.dtype)


def _memblock_kernel(x_ref, wq_ref, k_ref, v_ref, wo_ref, g_ref, b_ref, o_ref):
    x = x_ref[...]
    xb = x.astype(BF16)
    fx = None
    for h in range(X_HEADS):
        cols = slice(h * X_HEAD_DIM, (h + 1) * X_HEAD_DIM)
        q = _dot(xb, wq_ref[:, cols]).astype(BF16)
        s = _dot_nt(q, k_ref[:, cols]) * (X_HEAD_DIM ** -0.5)
        m = jnp.max(s, axis=-1, keepdims=True)
        p = jnp.exp(s - m)
        l = jnp.sum(p, axis=-1, keepdims=True)
        o_h = (_dot(p.astype(BF16), v_ref[:, cols]) / l).astype(BF16)
        part = _dot(o_h, wo_ref[cols, :])
        fx = part if fx is None else fx + part
    o_ref[...] = _layer_norm(DEEPNORM_ALPHA * x + fx, g_ref[...], b_ref[...])


def _mem_block_fused(x, w_q, kv, w_o, g, b, batch, seq, mem_len, q_tile=ROW_TILE):
    tiles = seq // q_tile
    d = D_MODEL
    resident = pl.Buffered(1)
    return pl.pallas_call(
        _memblock_kernel,
        grid=(batch, tiles),
        in_specs=[
            pl.BlockSpec((q_tile, d), lambda bi, i: (bi * tiles + i, 0)),
            pl.BlockSpec((d, d), lambda bi, i: (0, 0), pipeline_mode=resident),
            pl.BlockSpec((mem_len, d), lambda bi, i: (bi, 0)),
            pl.BlockSpec((mem_len, d), lambda bi, i: (bi, 1)),
            pl.BlockSpec((d, d), lambda bi, i: (0, 0), pipeline_mode=resident),
            pl.BlockSpec((1, d), lambda bi, i: (0, 0)),
            pl.BlockSpec((1, d), lambda bi, i: (0, 0)),
        ],
        out_specs=pl.BlockSpec((q_tile, d), lambda bi, i: (bi * tiles + i, 0)),
        out_shape=jax.ShapeDtypeStruct((batch * seq, d), F32),
        compiler_params=_params("parallel", "parallel"),
        name="mem_block",
    )(x, w_q, kv, kv, w_o, g.reshape(1, d), b.reshape(1, d))


def _mem_attention(x, w_q, kv, batch, seq, mem_len, q_tile=1024):
    q_tile = min(q_tile, seq)
    tiles = seq // q_tile
    return pl.pallas_call(
        _memattn_kernel,
        grid=(batch, tiles),
        in_specs=[
            pl.BlockSpec((q_tile, D_MODEL), lambda b, i: (b * tiles + i, 0)),
            pl.BlockSpec((D_MODEL, D_MODEL), lambda b, i: (0, 0)),
            pl.BlockSpec((mem_len, D_MODEL), lambda b, i: (b, 0)),
            pl.BlockSpec((mem_len, D_MODEL), lambda b, i: (b, 1)),
        ],
        out_specs=pl.BlockSpec((q_tile, D_MODEL), lambda b, i: (b * tiles + i, 0)),
        out_shape=jax.ShapeDtypeStruct((batch * seq, D_MODEL), BF16),
        compiler_params=_params("parallel", "parallel"),
        name="mem_attention",
    )(x, w_q, kv, kv)


def _prep_ffn(w_in, w_out):
    return w_in.astype(BF16), w_out.astype(BF16)


def _mem_block(x, mem2, w_q, w_kv, w_o, g, b, batch, seq, mem_len):
    kv = _project(mem2, w_kv.astype(BF16), BF16)
    return _mem_block_fused(x, w_q.astype(BF16), kv, w_o.astype(BF16), g, b, batch, seq, mem_len)


def kernel(x, mem, l0_ffn1_w_in, l0_ffn1_w_out, l0_ln1_g, l0_ln1_b, l0_mix_w_in, l0_gmlp_ln_g, l0_gmlp_ln_b, l0_gmlp_w_s, l0_gmlp_b_s, l0_mix_w_out, l0_ln2_g, l0_ln2_b, l0_mem_w_q, l0_mem_w_kv, l0_mem_w_o, l0_ln3_g, l0_ln3_b, l0_ffn2_w_in, l0_ffn2_w_out, l0_ln4_g, l0_ln4_b, l1_ffn1_w_in, l1_ffn1_w_out, l1_ln1_g, l1_ln1_b, l1_mix_w_in, l1_mix_w_out, l1_ln2_g, l1_ln2_b, l1_mem_w_q, l1_mem_w_kv, l1_mem_w_o, l1_ln3_g, l1_ln3_b, l1_ffn2_w_in, l1_ffn2_w_out, l1_ln4_g, l1_ln4_b):
    batch, seq, d = x.shape
    mem_len = mem.shape[1]
    h = x.reshape(batch * seq, d)
    mem2 = mem.reshape(batch * mem_len, d)

    h = _ffn_deepnorm(h, *_prep_ffn(l0_ffn1_w_in, l0_ffn1_w_out), l0_ln1_g, l0_ln1_b)
    qkv = _project(h, l0_mix_w_in[:, :A_QKV_WIDTH].astype(BF16), F32)
    uv = _project(h, l0_mix_w_in[:, A_QKV_WIDTH:].astype(BF16), F32)
    a_out = _dilated_attention(qkv, batch, seq)
    b_out = _gmlp(uv, l0_gmlp_ln_g, l0_gmlp_ln_b, l0_gmlp_w_s, l0_gmlp_b_s)
    a_width = a_out.shape[1]
    w_mix_out = [l0_mix_w_out[:a_width].astype(BF16), l0_mix_w_out[a_width:].astype(BF16)]
    h = _outproj_deepnorm([a_out, b_out], w_mix_out, h, l0_ln2_g, l0_ln2_b)
    h = _mem_block(h, mem2, l0_mem_w_q, l0_mem_w_kv, l0_mem_w_o, l0_ln3_g, l0_ln3_b, batch, seq, mem_len)
    h = _ffn_deepnorm(h, *_prep_ffn(l0_ffn2_w_in, l0_ffn2_w_out), l0_ln4_g, l0_ln4_b)

    h = _ffn_deepnorm(h, *_prep_ffn(l1_ffn1_w_in, l1_ffn1_w_out), l1_ln1_g, l1_ln1_b)
    qkv = _project(h, l1_mix_w_in.astype(BF16), BF16)
    o = _moba(qkv, batch, seq)
    h = _outproj_deepnorm([o], [l1_mix_w_out.astype(BF16)], h, l1_ln2_g, l1_ln2_b)
    h = _mem_block(h, mem2, l1_mem_w_q, l1_mem_w_kv, l1_mem_w_o, l1_ln3_g, l1_ln3_b, batch, seq, mem_len)
    h = _ffn_deepnorm(h, *_prep_ffn(l1_ffn2_w_in, l1_ffn2_w_out), l1_ln4_g, l1_ln4_b)
    return h.reshape(batch, seq, d)
```
